```python
import jax, jax.numpy as jnp
from jax import lax
import numpy as np

D_MODEL = 1024
BATCH = 2
SEQ = 8192
DEPTH = 1

N_MEM = 256
XATTN_HEADS = 4
XATTN_HEAD_DIM = D_MODEL // XATTN_HEADS
D_FF = 2816
GLA_HEADS = 4
GLA_DK = 64
GLA_DV = 128
GLA_KEY_W = GLA_HEADS * GLA_DK
GLA_VAL_W = GLA_HEADS * GLA_DV
GATE_RANK = 16
GATE_TAU = 16.0
CHUNK = 64
CONV_W = 512
CONV_K = 3
D_MIX = GLA_VAL_W + CONV_W
EPS = 1e-6
IN_SPLITS = (GLA_KEY_W, GLA_KEY_W, GLA_VAL_W, GLA_VAL_W, GATE_RANK, GATE_RANK,
             CONV_W, CONV_W, CONV_W)
D_IN = sum(IN_SPLITS)

kernel_name = "hybrid_gla_shortconv_macaron_memxattn"


def rms_norm(x, g):
    xf = x.astype(jnp.float32)
    y = xf * lax.rsqrt(jnp.mean(xf * xf, axis=-1, keepdims=True) + EPS)
    return (y * g.astype(jnp.float32)).astype(x.dtype)


def swiglu(h, w_gate, w_up, w_down):
    return (jax.nn.silu(h @ w_gate) * (h @ w_up)) @ w_down


def gla_chunked(q, k, v, g):
    b_, h_, s_, dk = q.shape
    dv = v.shape[-1]
    n = s_ // CHUNK
    q = q.reshape(b_, h_, n, CHUNK, dk)
    k = k.reshape(b_, h_, n, CHUNK, dk)
    v = v.reshape(b_, h_, n, CHUNK, dv)
    bcum = jnp.cumsum(g.reshape(b_, h_, n, CHUNK, dk), axis=3)
    b_last = bcum[:, :, :, -1:, :]
    q_in = q * jnp.exp(bcum)
    k_in = k * jnp.exp(-bcum)
    mask = jnp.tril(jnp.ones((CHUNK, CHUNK), dtype=bool))
    att = jnp.einsum('bhncd,bhnmd->bhncm', q_in, k_in)
    att = jnp.where(mask, att, 0.0)
    o_intra = jnp.einsum('bhncm,bhnmv->bhncv', att, v)
    k_st = k * jnp.exp(b_last - bcum)
    d_state = jnp.einsum('bhncd,bhncv->bhndv', k_st, v)
    decay = jnp.exp(b_last[:, :, :, 0, :])

    def step(state, inp):
        dec, ds = inp
        return dec[..., None] * state + ds, state

    init = jnp.zeros((b_, h_, dk, dv), jnp.float32)
    _, states = lax.scan(step, init, (jnp.moveaxis(decay, 2, 0), jnp.moveaxis(d_state, 2, 0)))
    states = jnp.moveaxis(states, 0, 2)
    o_inter = jnp.einsum('bhncd,bhndv->bhncv', q_in, states)
    return (o_intra + o_inter).reshape(b_, h_, s_, dv)


def low_rank_log_gate(z, w2, b2):
    zf = z.astype(jnp.float32) @ w2.astype(jnp.float32) + b2.astype(jnp.float32)
    lg = jax.nn.log_sigmoid(zf) / GATE_TAU
    bsz, s_ = z.shape[0], z.shape[1]
    return lg.reshape(bsz, s_, GLA_HEADS, GLA_DK).transpose(0, 2, 1, 3)


def to_heads(t, nh):
    bsz, s_, w = t.shape
    return t.reshape(bsz, s_, nh, w // nh).transpose(0, 2, 1, 3)


def token_mixing(h, w_in, gf_w, gf_b, gb_w, gb_b, gla_norm, conv_w, conv_b, w_out):
    bsz, s_, _ = h.shape
    proj = h @ w_in
    idx = np.cumsum(IN_SPLITS)[:-1].tolist()
    q, k, v, r, zf, zb, bg, cg, xv = jnp.split(proj, idx, axis=-1)
    qh = to_heads(q, GLA_HEADS).astype(jnp.float32) * (GLA_DK ** -0.5)
    kh = to_heads(k, GLA_HEADS).astype(jnp.float32)
    vh = to_heads(v, GLA_HEADS).astype(jnp.float32)
    g_fwd = low_rank_log_gate(zf, gf_w, gf_b)
    g_bwd = low_rank_log_gate(zb, gb_w, gb_b)
    o_fwd = gla_chunked(qh, kh, vh, g_fwd)
    flip = lambda t: jnp.flip(t, axis=2)
    o_bwd = flip(gla_chunked(flip(qh), flip(kh), flip(vh), flip(g_bwd)))
    o = o_fwd + o_bwd
    o = o * lax.rsqrt(jnp.mean(o * o, axis=-1, keepdims=True) + EPS)
    o = o * gla_norm.astype(jnp.float32).reshape(GLA_HEADS, 1, GLA_DV)
    o = o.transpose(0, 2, 1, 3).reshape(bsz, s_, GLA_VAL_W).astype(h.dtype)
    a_out = o * jax.nn.silu(r)
    u = cg * xv
    u_prev = jnp.pad(u[:, :-1], ((0, 0), (1, 0), (0, 0)))
    u_next = jnp.pad(u[:, 1:], ((0, 0), (0, 1), (0, 0)))
    conv = conv_w[0] * u_prev + conv_w[1] * u + conv_w[2] * u_next + conv_b
    c_out = bg * conv
    return jnp.concatenate([a_out, c_out], axis=-1) @ w_out


def memory_cross_attention(h, m, w_q, w_kv, w_o):
    q = to_heads(h @ w_q, XATTN_HEADS)
    kv = m @ w_kv
    k, v = jnp.split(kv, 2, axis=-1)
    k = to_heads(k, XATTN_HEADS)
    v = to_heads(v, XATTN_HEADS)
    s = jnp.einsum('bhsd,bhmd->bhsm', q.astype(jnp.float32), k.astype(jnp.float32))
    p = jax.nn.softmax(s * (XATTN_HEAD_DIM ** -0.5), axis=-1)
    o = jnp.einsum('bhsm,bhmd->bhsd', p, v.astype(jnp.float32)).astype(h.dtype)
    bsz, _, s_, _ = o.shape
    o = o.transpose(0, 2, 1, 3).reshape(bsz, s_, D_MODEL)
    return o @ w_o


def setup_inputs(seed: int = 0) -> dict:
    key = jax.random.key(seed)
    ks = iter(jax.random.split(key, 40))

    def dense(shape, fan_in):
        return jax.random.normal(next(ks), shape, jnp.float32) * (fan_in ** -0.5)

    def gain(shape):
        return 1.0 + 0.02 * jax.random.normal(next(ks), shape, jnp.float32)

    def small(shape, scale=0.02, offset=0.0):
        return offset + scale * jax.random.normal(next(ks), shape, jnp.float32)

    L = DEPTH
    return {
        "x": jax.random.normal(next(ks), (BATCH, SEQ, D_MODEL), jnp.float32),
        "mem": jax.random.normal(next(ks), (BATCH, N_MEM, D_MODEL), jnp.float32),
        "ffn1_norm": gain((L, D_MODEL)),
        "ffn1_w_gate": dense((L, D_MODEL, D_FF), D_MODEL),
        "ffn1_w_up": dense((L, D_MODEL, D_FF), D_MODEL),
        "ffn1_w_down": dense((L, D_FF, D_MODEL), D_FF),
        "mix_norm": gain((L, D_MODEL)),
        "w_in": dense((L, D_MODEL, D_IN), D_MODEL),
        "gate_fwd_w": dense((L, GATE_RANK, GLA_KEY_W), GATE_RANK),
        "gate_fwd_b": small((L, GLA_KEY_W), 0.5, 1.0),
        "gate_bwd_w": dense((L, GATE_RANK, GLA_KEY_W), GATE_RANK),
        "gate_bwd_b": small((L, GLA_KEY_W), 0.5, 1.0),
        "gla_norm": gain((L, GLA_VAL_W)),
        "conv_w": dense((L, CONV_K, CONV_W), CONV_K),
        "conv_b": small((L, CONV_W)),
        "w_out": dense((L, D_MIX, D_MODEL), D_MIX),
        "xattn_norm": gain((L, D_MODEL)),
        "mem_norm": gain((L, D_MODEL)),
        "xattn_w_q": dense((L, D_MODEL, D_MODEL), D_MODEL),
        "xattn_w_kv": dense((L, D_MODEL, 2 * D_MODEL), D_MODEL),
        "xattn_w_o": dense((L, D_MODEL, D_MODEL), D_MODEL),
        "ffn2_norm": gain((L, D_MODEL)),
        "ffn2_w_gate": dense((L, D_MODEL, D_FF), D_MODEL),
        "ffn2_w_up": dense((L, D_MODEL, D_FF), D_MODEL),
        "ffn2_w_down": dense((L, D_FF, D_MODEL), D_FF),
        "final_norm": gain((D_MODEL,)),
    }


def reference(x, mem, ffn1_norm, ffn1_w_gate, ffn1_w_up, ffn1_w_down, mix_norm, w_in,
              gate_fwd_w, gate_fwd_b, gate_bwd_w, gate_bwd_b, gla_norm, conv_w, conv_b,
              w_out, xattn_norm, mem_norm, xattn_w_q, xattn_w_kv, xattn_w_o,
              ffn2_norm, ffn2_w_gate, ffn2_w_up, ffn2_w_down, final_norm):
    for l in range(DEPTH):
        x = x + 0.5 * swiglu(rms_norm(x, ffn1_norm[l]), ffn1_w_gate[l], ffn1_w_up[l], ffn1_w_down[l])
        x = x + token_mixing(rms_norm(x, mix_norm[l]), w_in[l], gate_fwd_w[l], gate_fwd_b[l],
                             gate_bwd_w[l], gate_bwd_b[l], gla_norm[l], conv_w[l], conv_b[l], w_out[l])
        x = x + memory_cross_attention(rms_norm(x, xattn_norm[l]), rms_norm(mem, mem_norm[l]),
                                       xattn_w_q[l], xattn_w_kv[l], xattn_w_o[l])
        x = x + 0.5 * swiglu(rms_norm(x, ffn2_norm[l]), ffn2_w_gate[l], ffn2_w_up[l], ffn2_w_down[l])
    return rms_norm(x, final_norm)
```

```python
import functools

import jax
import jax.numpy as jnp
from jax import lax
from jax.experimental import pallas as pl
from jax.experimental.pallas import tpu as pltpu

F32 = jnp.float32
BF16 = jnp.bfloat16

XATTN_HEADS = 4
GLA_HEADS = 4
GLA_DK = 64
GLA_DV = 128
GLA_KEY_W = GLA_HEADS * GLA_DK
GLA_VAL_W = GLA_HEADS * GLA_DV
GATE_RANK = 16
GATE_TAU = 16.0
CHUNK = 64
CONV_W = 512
EPS = 1e-6

LANES = 128
SUBLANES = 8
TOKEN_TILE = 512
GLA_BLOCK = 512
FFN_CHUNK = 256
VMEM_LIMIT_BYTES = 56 * 1024 * 1024


def _cparams(n_axes):
    return pltpu.CompilerParams(
        dimension_semantics=("arbitrary",) * n_axes,
        vmem_limit_bytes=VMEM_LIMIT_BYTES,
    )


def _const_spec(shape):
    nd = len(shape)
    return pl.BlockSpec(shape, lambda *_: (0,) * nd, pipeline_mode=pl.Buffered(1))


def _rms(x, g):
    ms = jnp.mean(x * x, axis=-1, keepdims=True)
    return x * lax.rsqrt(ms + EPS) * g


def _dot(a, b):
    return jnp.dot(a, b, preferred_element_type=F32)


def _dot_nt(a, b):
    return lax.dot_general(a, b, (((1,), (1,)), ((), ())), preferred_element_type=F32)


def _ffn_body(*refs, d_ff, final):
    if final:
        x_ref, n_ref, wg_ref, wu_ref, wd_ref, fn_ref, o_ref, a_ref = refs
    else:
        x_ref, n_ref, wg_ref, wu_ref, wd_ref, o_ref, a_ref = refs
    x = x_ref[...]
    h = _rms(x, n_ref[...]).astype(BF16)
    for c in range(d_ff // FFN_CHUNK):
        sl = slice(c * FFN_CHUNK, (c + 1) * FFN_CHUNK)
        g = _dot(h, wg_ref[:, sl])
        u = _dot(h, wu_ref[:, sl])
        a_ref[:, sl] = (g * jax.nn.sigmoid(g) * u).astype(BF16)
    y = x + 0.5 * _dot(a_ref[...], wd_ref[...])
    if final:
        y = _rms(y, fn_ref[...])
    o_ref[...] = y


def _ffn(x, norm, wg, wu, wd, final_norm=None):
    t, d = x.shape
    d_ff = wg.shape[1]
    assert t % TOKEN_TILE == 0 and d_ff % FFN_CHUNK == 0
    final = final_norm is not None
    tile = pl.BlockSpec((TOKEN_TILE, d), lambda i: (i, 0))
    in_specs = [tile, _const_spec((1, d)), _const_spec((d, d_ff)), _const_spec((d, d_ff)), _const_spec((d_ff, d))]
    args = [x, norm.reshape(1, d), wg, wu, wd]
    if final:
        in_specs.append(_const_spec((1, d)))
        args.append(final_norm.reshape(1, d))
    return pl.pallas_call(
        functools.partial(_ffn_body, d_ff=d_ff, final=final),
        grid=(t // TOKEN_TILE,),
        in_specs=in_specs,
        out_specs=tile,
        out_shape=jax.ShapeDtypeStruct((t, d), F32),
        scratch_shapes=[pltpu.VMEM((TOKEN_TILE, d_ff), BF16)],
        compiler_params=_cparams(1),
        name="ffn_final" if final else "ffn",
    )(*args)


_P_Q, _P_K, _P_V, _P_R = 0, GLA_KEY_W, 2 * GLA_KEY_W, 2 * GLA_KEY_W + GLA_VAL_W
_P_BG = _P_R + GLA_VAL_W
_P_CG = _P_BG + CONV_W
_P_XV = _P_CG + CONV_W
_P_Z = _P_XV + CONV_W
_P_END = _P_Z + LANES


def _mix_in_body(x_ref, n_ref, wp_ref, wgate_ref, gbias_ref,
                 q_ref, k_ref, v_ref, r_ref, bg_ref, u_ref, bcf_ref, bcb_ref):
    h = _rms(x_ref[...], n_ref[...]).astype(BF16)
    p = _dot(h, wp_ref[...])
    q_ref[...] = p[:, _P_Q:_P_K] * (GLA_DK ** -0.5)
    k_ref[...] = p[:, _P_K:_P_V]
    v_ref[...] = p[:, _P_V:_P_R].astype(BF16)
    r_ref[...] = p[:, _P_R:_P_BG]
    bg_ref[...] = p[:, _P_BG:_P_CG]
    u_ref[...] = p[:, _P_CG:_P_XV] * p[:, _P_XV:_P_Z]
    gz = _dot(p[:, _P_Z:_P_END].astype(BF16), wgate_ref[...]) + gbias_ref[...]
    lg = jax.nn.log_sigmoid(gz) * (1.0 / GATE_TAU)
    row = lax.broadcasted_iota(jnp.int32, (CHUNK, CHUNK), 0)
    col = lax.broadcasted_iota(jnp.int32, (CHUNK, CHUNK), 1)
    tri_f = (col <= row).astype(BF16)
    tri_b = (col >= row).astype(BF16)
    for c in range(x_ref.shape[0] // CHUNK):
        rows = slice(c * CHUNK, (c + 1) * CHUNK)
        for tri, lo_lane, out_ref in ((tri_f, 0, bcf_ref), (tri_b, GLA_KEY_W, bcb_ref)):
            g = lg[rows, lo_lane:lo_lane + GLA_KEY_W]
            g_hi = g.astype(BF16)
            g_lo = (g - g_hi.astype(F32)).astype(BF16)
            s = _dot(tri, jnp.concatenate([g_hi, g_lo], axis=1))
            out_ref[rows, :] = s[:, :GLA_KEY_W] + s[:, GLA_KEY_W:]


def _mix_in(x, norm, wp, wgate, gbias):
    t, d = x.shape
    tile = lambda w: pl.BlockSpec((TOKEN_TILE, w), lambda i: (i, 0))
    out_w = [(GLA_KEY_W, F32), (GLA_KEY_W, F32), (GLA_VAL_W, BF16), (GLA_VAL_W, F32),
             (CONV_W, F32), (CONV_W, F32), (GLA_KEY_W, F32), (GLA_KEY_W, F32)]
    return pl.pallas_call(
        _mix_in_body,
        grid=(t // TOKEN_TILE,),
        in_specs=[tile(d), _const_spec((1, d)), _const_spec(wp.shape), _const_spec(wgate.shape),
                  _const_spec(gbias.shape)],
        out_specs=[tile(w) for w, _ in out_w],
        out_shape=[jax.ShapeDtypeStruct((t, w), dt) for w, dt in out_w],
        compiler_params=_cparams(1),
        name="mix_in",
    )(x, norm.reshape(1, d), wp, wgate, gbias)


def _gla_direction(q_ref, k_ref, v_ref, bc_ref, o_ref, s_ref, *, reverse):
    n_chunks = q_ref.shape[0] // CHUNK
    kw, vw = GLA_KEY_W, GLA_VAL_W
    assert GLA_DK == CHUNK and CHUNK & (CHUNK - 1) == 0
    shift = CHUNK.bit_length() - 1
    row_head = lax.broadcasted_iota(jnp.int32, (kw, kw), 0) >> shift
    lane_head = lax.broadcasted_iota(jnp.int32, (kw, kw), 1) >> shift
    same_head = row_head == lane_head
    pos_q = lax.broadcasted_iota(jnp.int32, (CHUNK, kw), 0)
    pos_k = lax.broadcasted_iota(jnp.int32, (CHUNK, kw), 1) & (CHUNK - 1)
    causal = (pos_k >= pos_q) if reverse else (pos_k <= pos_q)
    zero_v = jnp.zeros((CHUNK, GLA_DV), BF16)

    order = range(n_chunks - 1, -1, -1) if reverse else range(n_chunks)
    for c in order:
        rows = slice(c * CHUNK, (c + 1) * CHUNK)
        bc = bc_ref[rows, :]
        q = q_ref[rows, :]
        k = k_ref[rows, :]
        v = v_ref[rows, :]
        b_last = bc[0:1, :] if reverse else bc[CHUNK - 1:CHUNK, :]
        q_in = (q * jnp.exp(bc)).astype(BF16)
        k_in = k * jnp.exp(-bc)
        k_st = k * jnp.exp(b_last - bc)
        k_blk = jnp.where(same_head, jnp.concatenate([k_in] * GLA_HEADS, axis=0), 0.0).astype(BF16)
        att = _dot_nt(q_in, k_blk)
        att = jnp.where(causal, att, 0.0).astype(BF16)
        v_blk = jnp.concatenate(
            [jnp.concatenate([v[:, h * GLA_DV:(h + 1) * GLA_DV] if h == hh else zero_v
                              for hh in range(GLA_HEADS)], axis=1) for h in range(GLA_HEADS)], axis=0)
        s_in = s_ref[...]
        o_ref[rows, :] = _dot(att, v_blk) + _dot(q_in, s_in.astype(BF16))
        decay_rows = jnp.broadcast_to(jnp.exp(b_last), (SUBLANES, kw))
        pad = jnp.zeros((LANES - CHUNK - SUBLANES, kw), F32)
        xt = jnp.concatenate([k_st, decay_rows, pad], axis=0).T
        v_aug = jnp.concatenate([v, jnp.zeros((LANES - CHUNK, vw), BF16)], axis=0)
        ds = _dot(xt.astype(BF16), v_aug)
        decay = xt[:, CHUNK:CHUNK + 1]
        for h in range(GLA_HEADS):
            r_sl = slice(h * GLA_DK, (h + 1) * GLA_DK)
            c_sl = slice(h * GLA_DV, (h + 1) * GLA_DV)
            s_ref[r_sl, c_sl] = decay[r_sl, :] * s_in[r_sl, c_sl] + ds[r_sl, c_sl]


def _gla_body(qf_ref, kf_ref, vf_ref, bf_ref, qb_ref, kb_ref, vb_ref, bb_ref, of_ref, ob_ref, sf_ref, sb_ref):
    @pl.when(pl.program_id(1) == 0)
    def _():
        sf_ref[...] = jnp.zeros_like(sf_ref)
        sb_ref[...] = jnp.zeros_like(sb_ref)

    _gla_direction(qf_ref, kf_ref, vf_ref, bf_ref, of_ref, sf_ref, reverse=False)
    _gla_direction(qb_ref, kb_ref, vb_ref, bb_ref, ob_ref, sb_ref, reverse=True)


def _gla(q, k, v, bcf, bcb, batch, seq):
    nb = seq // GLA_BLOCK
    fwd = lambda w: pl.BlockSpec((GLA_BLOCK, w), lambda b, j: (b * nb + j, 0))
    bwd = lambda w: pl.BlockSpec((GLA_BLOCK, w), lambda b, j: (b * nb + nb - 1 - j, 0))
    t = batch * seq
    return pl.pallas_call(
        _gla_body,
        grid=(batch, nb),
        in_specs=[fwd(GLA_KEY_W), fwd(GLA_KEY_W), fwd(GLA_VAL_W), fwd(GLA_KEY_W),
                  bwd(GLA_KEY_W), bwd(GLA_KEY_W), bwd(GLA_VAL_W), bwd(GLA_KEY_W)],
        out_specs=[fwd(GLA_VAL_W), bwd(GLA_VAL_W)],
        out_shape=[jax.ShapeDtypeStruct((t, GLA_VAL_W), F32)] * 2,
        scratch_shapes=[pltpu.VMEM((GLA_KEY_W, GLA_VAL_W), F32)] * 2,
        compiler_params=_cparams(2),
        name="gla",
    )(q, k, v, bcf, q, k, v, bcb)


def _mix_out_body(x_ref, of_ref, ob_ref, r_ref, bg_ref, u_ref, up_ref, un_ref, gn_ref, cw_ref, cb_ref, wo_ref,
                  o_ref, *, seq):
    tm = x_ref.shape[0]
    i = pl.program_id(0)
    o = of_ref[...] + ob_ref[...]
    heads = []
    for h in range(GLA_HEADS):
        oh = o[:, h * GLA_DV:(h + 1) * GLA_DV]
        heads.append(oh * lax.rsqrt(jnp.mean(oh * oh, axis=-1, keepdims=True) + EPS))
    r = r_ref[...]
    a_out = jnp.concatenate(heads, axis=1) * gn_ref[...] * (r * jax.nn.sigmoid(r))
    u = u_ref[...]
    at_seq_start = (i * tm) % seq == 0
    at_seq_end = ((i + 1) * tm) % seq == 0
    halo_prev = jnp.where(at_seq_start, 0.0, up_ref[SUBLANES - 1:SUBLANES, :])
    halo_next = jnp.where(at_seq_end, 0.0, un_ref[0:1, :])
    row = lax.broadcasted_iota(jnp.int32, u.shape, 0)
    u_prev = jnp.where(row == 0, halo_prev, pltpu.roll(u, 1, 0))
    u_next = jnp.where(row == tm - 1, halo_next, pltpu.roll(u, tm - 1, 0))
    conv = cw_ref[0:1, :] * u_prev + cw_ref[1:2, :] * u + cw_ref[2:3, :] * u_next + cb_ref[...]
    c_out = bg_ref[...] * conv
    mixed = jnp.concatenate([a_out, c_out], axis=1).astype(BF16)
    o_ref[...] = x_ref[...] + _dot(mixed, wo_ref[...])


def _mix_out(x, o_f, o_b, r, bg, u, gla_norm, conv_w, conv_b, w_out, seq):
    t, d = x.shape
    tm = TOKEN_TILE
    assert seq % tm == 0 and tm % SUBLANES == 0
    tile = lambda w: pl.BlockSpec((tm, w), lambda i: (i, 0))
    hb = tm // SUBLANES
    n_halo = t // SUBLANES
    halo_prev = pl.BlockSpec((SUBLANES, CONV_W), lambda i: (jnp.maximum(i * hb - 1, 0), 0))
    halo_next = pl.BlockSpec((SUBLANES, CONV_W), lambda i: (jnp.minimum((i + 1) * hb, n_halo - 1), 0))
    return pl.pallas_call(
        functools.partial(_mix_out_body, seq=seq),
        grid=(t // tm,),
        in_specs=[tile(d), tile(GLA_VAL_W), tile(GLA_VAL_W), tile(GLA_VAL_W), tile(CONV_W), tile(CONV_W),
                  halo_prev, halo_next, _const_spec((1, GLA_VAL_W)), _const_spec(conv_w.shape),
                  _const_spec((1, CONV_W)), _const_spec(w_out.shape)],
        out_specs=tile(d),
        out_shape=jax.ShapeDtypeStruct((t, d), F32),
        compiler_params=_cparams(1),
        name="mix_out",
    )(x, o_f, o_b, r, bg, u, u, u, gla_norm.reshape(1, -1), conv_w, conv_b.reshape(1, -1), w_out)


def _kv_body(m_ref, n_ref, w_ref, k_ref, v_ref):
    d = m_ref.shape[-1]
    h = _rms(m_ref[0], n_ref[...]).astype(BF16)
    kv = _dot(h, w_ref[...])
    k_ref[0] = kv[:, :d].astype(BF16)
    v_ref[0] = kv[:, d:].astype(BF16)


def _kv(mem, norm, w_kv):
    b, m, d = mem.shape
    blk = pl.BlockSpec((1, m, d), lambda i: (i, 0, 0))
    return pl.pallas_call(
        _kv_body,
        grid=(b,),
        in_specs=[blk, _const_spec((1, d)), _const_spec(w_kv.shape)],
        out_specs=[blk, blk],
        out_shape=[jax.ShapeDtypeStruct((b, m, d), BF16)] * 2,
        compiler_params=_cparams(1),
        name="kv",
    )(mem, norm.reshape(1, d), w_kv)


def _xattn_body(x_ref, n_ref, wq_ref, k_ref, v_ref, wo_ref, o_ref):
    x = x_ref[...]
    d = x.shape[-1]
    hd = d // XATTN_HEADS
    h = _rms(x, n_ref[...]).astype(BF16)
    q = _dot(h, wq_ref[...])
    outs = []
    for a in range(XATTN_HEADS):
        sl = slice(a * hd, (a + 1) * hd)
        s = _dot_nt(q[:, sl].astype(BF16), k_ref[0, :, sl]) * (hd ** -0.5)
        p = jnp.exp(s - jnp.max(s, axis=-1, keepdims=True))
        denom = jnp.sum(p, axis=-1, keepdims=True)
        outs.append(_dot(p.astype(BF16), v_ref[0, :, sl]) / denom)
    o = jnp.concatenate(outs, axis=1).astype(BF16)
    o_ref[...] = x + _dot(o, wo_ref[...])


def _xattn(x, norm, w_q, k, v, w_o, seq):
    t, d = x.shape
    tm = TOKEN_TILE
    m = k.shape[1]
    tile = pl.BlockSpec((tm, d), lambda i: (i, 0))
    kv_blk = pl.BlockSpec((1, m, d), lambda i: (i // (seq // tm), 0, 0))
    return pl.pallas_call(
        _xattn_body,
        grid=(t // tm,),
        in_specs=[tile, _const_spec((1, d)), _const_spec(w_q.shape), kv_blk, kv_blk, _const_spec(w_o.shape)],
        out_specs=tile,
        out_shape=jax.ShapeDtypeStruct((t, d), F32),
        compiler_params=_cparams(1),
        name="xattn",
    )(x, norm.reshape(1, d), w_q, k, v, w_o)


def _pack_w_in(w_in):
    d = w_in.shape[0]
    main = 2 * GLA_KEY_W + 2 * GLA_VAL_W
    z_w = 2 * GATE_RANK
    z = w_in[:, main:main + z_w]
    conv = w_in[:, main + z_w:]
    pad = jnp.zeros((d, LANES - z_w), w_in.dtype)
    return jnp.concatenate([w_in[:, :main], conv, z, pad], axis=1).astype(BF16)


def _pack_gate(gf_w, gf_b, gb_w, gb_b):
    w = jnp.zeros((LANES, 2 * GLA_KEY_W), F32)
    w = w.at[:GATE_RANK, :GLA_KEY_W].set(gf_w)
    w = w.at[GATE_RANK:2 * GATE_RANK, GLA_KEY_W:].set(gb_w)
    return w.astype(BF16), jnp.concatenate([gf_b, gb_b]).reshape(1, -1).astype(F32)


def kernel(x, mem, ffn1_norm, ffn1_w_gate, ffn1_w_up, ffn1_w_down, mix_norm, w_in, gate_fwd_w, gate_fwd_b,
           gate_bwd_w, gate_bwd_b, gla_norm, conv_w, conv_b, w_out, xattn_norm, mem_norm, xattn_w_q,
           xattn_w_kv, xattn_w_o, ffn2_norm, ffn2_w_gate, ffn2_w_up, ffn2_w_down, final_norm):
    batch, seq, d = x.shape
    depth = w_in.shape[0]
    assert seq % GLA_BLOCK == 0 and GLA_BLOCK % CHUNK == 0 and seq % TOKEN_TILE == 0
    h = x.reshape(batch * seq, d)
    bf = lambda w: w.astype(BF16)
    for l in range(depth):
        h = _ffn(h, ffn1_norm[l], bf(ffn1_w_gate[l]), bf(ffn1_w_up[l]), bf(ffn1_w_down[l]))
        wgate, gbias = _pack_gate(gate_fwd_w[l], gate_fwd_b[l], gate_bwd_w[l], gate_bwd_b[l])
        q, k, v, r, bg, u, bcf, bcb = _mix_in(h, mix_norm[l], _pack_w_in(w_in[l]), wgate, gbias)
        o_f, o_b = _gla(q, k, v, bcf, bcb, batch, seq)
        h = _mix_out(h, o_f, o_b, r, bg, u, gla_norm[l], conv_w[l], conv_b[l], bf(w_out[l]), seq)
        mk, mv = _kv(mem, mem_norm[l], bf(xattn_w_kv[l]))
        h = _xattn(h, xattn_norm[l], bf(xattn_w_q[l]), mk, mv, bf(xattn_w_o[l]), seq)
        h = _ffn(h, ffn2_norm[l], bf(ffn2_w_gate[l]), bf(ffn2_w_up[l]), bf(ffn2_w_down[l]),
                 final_norm=final_norm if l == depth - 1 else None)
    return h.reshape(batch, seq, d)
```

```python
import functools

import jax
import jax.numpy as jnp
from jax import lax
from jax.experimental import pallas as pl
from jax.experimental.pallas import tpu as pltpu

F32 = jnp.float32
BF16 = jnp.bfloat16

XATTN_HEADS = 4
GLA_HEADS = 4
GLA_DK = 64
GLA_DV = 128
GLA_KEY_W = GLA_HEADS * GLA_DK
GLA_VAL_W = GLA_HEADS * GLA_DV
GATE_RANK = 16
GATE_TAU = 16.0
CHUNK = 64
CONV_W = 512
EPS = 1e-6

LANES = 128
SUBLANES = 8
BF16_ROWS = 16
TOKEN_TILE = 512
GLA_BLOCK = 512
FFN_CHUNK = 256
VMEM_LIMIT_BYTES = 58 * 1024 * 1024


def _cparams(n_axes):
    return pltpu.CompilerParams(
        dimension_semantics=("arbitrary",) * n_axes,
        vmem_limit_bytes=VMEM_LIMIT_BYTES,
    )


def _const_spec(shape):
    nd = len(shape)
    return pl.BlockSpec(shape, lambda *_: (0,) * nd, pipeline_mode=pl.Buffered(1))


def _rms(x, g):
    ms = jnp.mean(x * x, axis=-1, keepdims=True)
    return x * lax.rsqrt(ms + EPS) * g


def _dot(a, b):
    return jnp.dot(a, b, preferred_element_type=F32)


def _dot_nt(a, b):
    return lax.dot_general(a, b, (((1,), (1,)), ((), ())), preferred_element_type=F32)


def _swiglu_half_step(x, n_ref, wg_ref, wu_ref, wd_ref, a_ref):
    h = _rms(x, n_ref[...]).astype(BF16)
    for c in range(wg_ref.shape[1] // FFN_CHUNK):
        sl = slice(c * FFN_CHUNK, (c + 1) * FFN_CHUNK)
        g = _dot(h, wg_ref[:, sl])
        u = _dot(h, wu_ref[:, sl])
        a_ref[:, sl] = (g * jax.nn.sigmoid(g) * u).astype(BF16)
    return x + 0.5 * _dot(a_ref[...], wd_ref[...])


_P_Q, _P_K, _P_V, _P_R = 0, GLA_KEY_W, 2 * GLA_KEY_W, 2 * GLA_KEY_W + GLA_VAL_W
_P_BG = _P_R + GLA_VAL_W
_P_CG = _P_BG + CONV_W
_P_XV = _P_CG + CONV_W
_P_Z = _P_XV + CONV_W
_P_END = _P_Z + LANES


def _pre_body(x_ref, fn_ref, wg_ref, wu_ref, wd_ref, mn_ref, wp_ref, wgate_ref, gbias_ref,
              x1_ref, q_ref, k_ref, v_ref, r_ref, bg_ref, u_ref, bcf_ref, bcb_ref, a_ref):
    x1 = _swiglu_half_step(x_ref[...], fn_ref, wg_ref, wu_ref, wd_ref, a_ref)
    x1_ref[...] = x1
    h = _rms(x1, mn_ref[...]).astype(BF16)
    p = _dot(h, wp_ref[...])
    q_ref[...] = p[:, _P_Q:_P_K] * (GLA_DK ** -0.5)
    k_ref[...] = p[:, _P_K:_P_V]
    v_ref[...] = p[:, _P_V:_P_R].astype(BF16)
    r_ref[...] = p[:, _P_R:_P_BG].astype(BF16)
    bg_ref[...] = p[:, _P_BG:_P_CG].astype(BF16)
    u_ref[...] = (p[:, _P_CG:_P_XV] * p[:, _P_XV:_P_Z]).astype(BF16)
    gz = _dot(p[:, _P_Z:_P_END].astype(BF16), wgate_ref[...]) + gbias_ref[...]
    lg = jax.nn.log_sigmoid(gz) * (1.0 / GATE_TAU)
    row = lax.broadcasted_iota(jnp.int32, (CHUNK, CHUNK), 0)
    col = lax.broadcasted_iota(jnp.int32, (CHUNK, CHUNK), 1)
    tri_f = (col <= row).astype(BF16)
    tri_b = (col >= row).astype(BF16)
    for c in range(x_ref.shape[0] // CHUNK):
        rows = slice(c * CHUNK, (c + 1) * CHUNK)
        for tri, lo_lane, out_ref in ((tri_f, 0, bcf_ref), (tri_b, GLA_KEY_W, bcb_ref)):
            g = lg[rows, lo_lane:lo_lane + GLA_KEY_W]
            g_hi = g.astype(BF16)
            g_lo = (g - g_hi.astype(F32)).astype(BF16)
            s = _dot(tri, jnp.concatenate([g_hi, g_lo], axis=1))
            out_ref[rows, :] = s[:, :GLA_KEY_W] + s[:, GLA_KEY_W:]


def _pre(x, ffn_norm, wg, wu, wd, mix_norm, wp, wgate, gbias):
    t, d = x.shape
    d_ff = wg.shape[1]
    assert t % TOKEN_TILE == 0 and d_ff % FFN_CHUNK == 0 and TOKEN_TILE % CHUNK == 0
    tile = lambda w: pl.BlockSpec((TOKEN_TILE, w), lambda i: (i, 0))
    out_w = [(d, F32), (GLA_KEY_W, F32), (GLA_KEY_W, F32), (GLA_VAL_W, BF16), (GLA_VAL_W, BF16),
             (CONV_W, BF16), (CONV_W, BF16), (GLA_KEY_W, F32), (GLA_KEY_W, F32)]
    return pl.pallas_call(
        _pre_body,
        grid=(t // TOKEN_TILE,),
        in_specs=[tile(d), _const_spec((1, d)), _const_spec(wg.shape), _const_spec(wu.shape),
                  _const_spec(wd.shape), _const_spec((1, d)), _const_spec(wp.shape), _const_spec(wgate.shape),
                  _const_spec(gbias.shape)],
        out_specs=[tile(w) for w, _ in out_w],
        out_shape=[jax.ShapeDtypeStruct((t, w), dt) for w, dt in out_w],
        scratch_shapes=[pltpu.VMEM((TOKEN_TILE, d_ff), BF16)],
        compiler_params=_cparams(1),
        name="pre",
    )(x, ffn_norm.reshape(1, d), wg, wu, wd, mix_norm.reshape(1, d), wp, wgate, gbias)


def _gla_direction(q_ref, k_ref, v_ref, bc_ref, o_ref, s_ref, *, reverse):
    n_chunks = q_ref.shape[0] // CHUNK
    kw, vw = GLA_KEY_W, GLA_VAL_W
    assert GLA_DK == CHUNK and CHUNK & (CHUNK - 1) == 0
    shift = CHUNK.bit_length() - 1
    row_head = lax.broadcasted_iota(jnp.int32, (kw, kw), 0) >> shift
    lane_head = lax.broadcasted_iota(jnp.int32, (kw, kw), 1) >> shift
    same_head = row_head == lane_head
    pos_q = lax.broadcasted_iota(jnp.int32, (CHUNK, kw), 0)
    pos_k = lax.broadcasted_iota(jnp.int32, (CHUNK, kw), 1) & (CHUNK - 1)
    causal = (pos_k >= pos_q) if reverse else (pos_k <= pos_q)
    zero_v = jnp.zeros((CHUNK, GLA_DV), BF16)

    order = range(n_chunks - 1, -1, -1) if reverse else range(n_chunks)
    for c in order:
        rows = slice(c * CHUNK, (c + 1) * CHUNK)
        bc = bc_ref[rows, :]
        q = q_ref[rows, :]
        k = k_ref[rows, :]
        v = v_ref[rows, :]
        b_last = bc[0:1, :] if reverse else bc[CHUNK - 1:CHUNK, :]
        q_in = (q * jnp.exp(bc)).astype(BF16)
        k_in = k * jnp.exp(-bc)
        k_st = k * jnp.exp(b_last - bc)
        k_blk = jnp.where(same_head, jnp.concatenate([k_in] * GLA_HEADS, axis=0), 0.0).astype(BF16)
        att = _dot_nt(q_in, k_blk)
        att = jnp.where(causal, att, 0.0).astype(BF16)
        v_blk = jnp.concatenate(
            [jnp.concatenate([v[:, h * GLA_DV:(h + 1) * GLA_DV] if h == hh else zero_v
                              for hh in range(GLA_HEADS)], axis=1) for h in range(GLA_HEADS)], axis=0)
        s_in = s_ref[...]
        o_ref[rows, :] = (_dot(att, v_blk) + _dot(q_in, s_in.astype(BF16))).astype(o_ref.dtype)
        decay_rows = jnp.broadcast_to(jnp.exp(b_last), (SUBLANES, kw))
        pad = jnp.zeros((LANES - CHUNK - SUBLANES, kw), F32)
        xt = jnp.concatenate([k_st, decay_rows, pad], axis=0).T
        v_aug = jnp.concatenate([v, jnp.zeros((LANES - CHUNK, vw), BF16)], axis=0)
        ds = _dot(xt.astype(BF16), v_aug)
        decay = xt[:, CHUNK:CHUNK + 1]
        for h in range(GLA_HEADS):
            r_sl = slice(h * GLA_DK, (h + 1) * GLA_DK)
            c_sl = slice(h * GLA_DV, (h + 1) * GLA_DV)
            s_ref[r_sl, c_sl] = decay[r_sl, :] * s_in[r_sl, c_sl] + ds[r_sl, c_sl]


def _gla_body(qf_ref, kf_ref, vf_ref, bf_ref, qb_ref, kb_ref, vb_ref, bb_ref, of_ref, ob_ref, sf_ref, sb_ref):
    @pl.when(pl.program_id(1) == 0)
    def _():
        sf_ref[...] = jnp.zeros_like(sf_ref)
        sb_ref[...] = jnp.zeros_like(sb_ref)

    _gla_direction(qf_ref, kf_ref, vf_ref, bf_ref, of_ref, sf_ref, reverse=False)
    _gla_direction(qb_ref, kb_ref, vb_ref, bb_ref, ob_ref, sb_ref, reverse=True)


def _gla(q, k, v, bcf, bcb, batch, seq):
    nb = seq // GLA_BLOCK
    fwd = lambda w: pl.BlockSpec((GLA_BLOCK, w), lambda b, j: (b * nb + j, 0))
    bwd = lambda w: pl.BlockSpec((GLA_BLOCK, w), lambda b, j: (b * nb + nb - 1 - j, 0))
    t = batch * seq
    return pl.pallas_call(
        _gla_body,
        grid=(batch, nb),
        in_specs=[fwd(GLA_KEY_W), fwd(GLA_KEY_W), fwd(GLA_VAL_W), fwd(GLA_KEY_W),
                  bwd(GLA_KEY_W), bwd(GLA_KEY_W), bwd(GLA_VAL_W), bwd(GLA_KEY_W)],
        out_specs=[fwd(GLA_VAL_W), bwd(GLA_VAL_W)],
        out_shape=[jax.ShapeDtypeStruct((t, GLA_VAL_W), BF16)] * 2,
        scratch_shapes=[pltpu.VMEM((GLA_KEY_W, GLA_VAL_W), F32)] * 2,
        compiler_params=_cparams(2),
        name="gla",
    )(q, k, v, bcf, q, k, v, bcb)


def _kv_body(m_ref, n_ref, w_ref, k_ref, v_ref):
    d = m_ref.shape[-1]
    h = _rms(m_ref[0], n_ref[...]).astype(BF16)
    kv = _dot(h, w_ref[...])
    k_ref[0] = kv[:, :d].astype(BF16)
    v_ref[0] = kv[:, d:].astype(BF16)


def _kv(mem, norm, w_kv):
    b, m, d = mem.shape
    blk = pl.BlockSpec((1, m, d), lambda i: (i, 0, 0))
    return pl.pallas_call(
        _kv_body,
        grid=(b,),
        in_specs=[blk, _const_spec((1, d)), _const_spec(w_kv.shape)],
        out_specs=[blk, blk],
        out_shape=[jax.ShapeDtypeStruct((b, m, d), BF16)] * 2,
        compiler_params=_cparams(1),
        name="kv",
    )(mem, norm.reshape(1, d), w_kv)


def _token_mix_out(x, of_ref, ob_ref, r_ref, bg_ref, u_ref, up_ref, un_ref, gn_ref, cw_ref, cb_ref, wo_ref, seq):
    tm = x.shape[0]
    i = pl.program_id(0)
    o = of_ref[...].astype(F32) + ob_ref[...].astype(F32)
    heads = []
    for h in range(GLA_HEADS):
        oh = o[:, h * GLA_DV:(h + 1) * GLA_DV]
        heads.append(oh * lax.rsqrt(jnp.mean(oh * oh, axis=-1, keepdims=True) + EPS))
    r = r_ref[...].astype(F32)
    a_out = jnp.concatenate(heads, axis=1) * gn_ref[...] * (r * jax.nn.sigmoid(r))
    u = u_ref[...].astype(F32)
    at_seq_start = (i * tm) % seq == 0
    at_seq_end = ((i + 1) * tm) % seq == 0
    halo_prev = jnp.where(at_seq_start, 0.0, up_ref[BF16_ROWS - 1:BF16_ROWS, :].astype(F32))
    halo_next = jnp.where(at_seq_end, 0.0, un_ref[0:1, :].astype(F32))
    row = lax.broadcasted_iota(jnp.int32, u.shape, 0)
    u_prev = jnp.where(row == 0, halo_prev, pltpu.roll(u, 1, 0))
    u_next = jnp.where(row == tm - 1, halo_next, pltpu.roll(u, tm - 1, 0))
    conv = cw_ref[0:1, :] * u_prev + cw_ref[1:2, :] * u + cw_ref[2:3, :] * u_next + cb_ref[...]
    c_out = bg_ref[...].astype(F32) * conv
    mixed = jnp.concatenate([a_out, c_out], axis=1).astype(BF16)
    return x + _dot(mixed, wo_ref[...])


def _cross_attention(x, n_ref, wq_ref, k_ref, v_ref, wo_ref):
    d = x.shape[-1]
    hd = d // XATTN_HEADS
    h = _rms(x, n_ref[...]).astype(BF16)
    q = _dot(h, wq_ref[...])
    outs = []
    for a in range(XATTN_HEADS):
        sl = slice(a * hd, (a + 1) * hd)
        s = _dot_nt(q[:, sl].astype(BF16), k_ref[0, :, sl]) * (hd ** -0.5)
        p = jnp.exp(s - jnp.max(s, axis=-1, keepdims=True))
        denom = jnp.sum(p, axis=-1, keepdims=True)
        outs.append(_dot(p.astype(BF16), v_ref[0, :, sl]) / denom)
    o = jnp.concatenate(outs, axis=1).astype(BF16)
    return x + _dot(o, wo_ref[...])


def _post_body(*refs, seq, final):
    (x_ref, of_ref, ob_ref, r_ref, bg_ref, u_ref, up_ref, un_ref, gn_ref, cw_ref, cb_ref, wo_ref,
     xn_ref, wq_ref, k_ref, v_ref, wxo_ref, fn_ref, wg_ref, wu_ref, wd_ref) = refs[:21]
    if final:
        last_ref, o_ref, a_ref = refs[21:]
    else:
        o_ref, a_ref = refs[21:]
    x2 = _token_mix_out(x_ref[...], of_ref, ob_ref, r_ref, bg_ref, u_ref, up_ref, un_ref, gn_ref, cw_ref, cb_ref,
                        wo_ref, seq)
    x3 = _cross_attention(x2, xn_ref, wq_ref, k_ref, v_ref, wxo_ref)
    y = _swiglu_half_step(x3, fn_ref, wg_ref, wu_ref, wd_ref, a_ref)
    if final:
        y = _rms(y, last_ref[...])
    o_ref[...] = y


def _post(x, o_f, o_b, r, bg, u, gla_norm, conv_w, conv_b, w_out, xattn_norm, w_q, mk, mv, w_xo,
          ffn_norm, wg, wu, wd, seq, final_norm=None):
    t, d = x.shape
    tm = TOKEN_TILE
    d_ff = wg.shape[1]
    m = mk.shape[1]
    assert seq % tm == 0 and tm % BF16_ROWS == 0 and d_ff % FFN_CHUNK == 0
    final = final_norm is not None
    tile = lambda w: pl.BlockSpec((tm, w), lambda i: (i, 0))
    hb = tm // BF16_ROWS
    n_halo = t // BF16_ROWS
    halo_prev = pl.BlockSpec((BF16_ROWS, CONV_W), lambda i: (jnp.maximum(i * hb - 1, 0), 0))
    halo_next = pl.BlockSpec((BF16_ROWS, CONV_W), lambda i: (jnp.minimum((i + 1) * hb, n_halo - 1), 0))
    kv_blk = pl.BlockSpec((1, m, d), lambda i: (i // (seq // tm), 0, 0))
    in_specs = [tile(d), tile(GLA_VAL_W), tile(GLA_VAL_W), tile(GLA_VAL_W), tile(CONV_W), tile(CONV_W),
                halo_prev, halo_next, _const_spec((1, GLA_VAL_W)), _const_spec(conv_w.shape),
                _const_spec((1, CONV_W)), _const_spec(w_out.shape),
                _const_spec((1, d)), _const_spec(w_q.shape), kv_blk, kv_blk, _const_spec(w_xo.shape),
                _const_spec((1, d)), _const_spec(wg.shape), _const_spec(wu.shape), _const_spec(wd.shape)]
    args = [x, o_f, o_b, r, bg, u, u, u, gla_norm.reshape(1, -1), conv_w, conv_b.reshape(1, -1), w_out,
            xattn_norm.reshape(1, d), w_q, mk, mv, w_xo, ffn_norm.reshape(1, d), wg, wu, wd]
    if final:
        in_specs.append(_const_spec((1, d)))
        args.append(final_norm.reshape(1, d))
    return pl.pallas_call(
        functools.partial(_post_body, seq=seq, final=final),
        grid=(t // tm,),
        in_specs=in_specs,
        out_specs=tile(d),
        out_shape=jax.ShapeDtypeStruct((t, d), F32),
        scratch_shapes=[pltpu.VMEM((tm, d_ff), BF16)],
        compiler_params=_cparams(1),
        name="post_final" if final else "post",
    )(*args)


def _pack_w_in(w_in):
    d = w_in.shape[0]
    main = 2 * GLA_KEY_W + 2 * GLA_VAL_W
    z_w = 2 * GATE_RANK
    z = w_in[:, main:main + z_w]
    conv = w_in[:, main + z_w:]
    pad = jnp.zeros((d, LANES - z_w), w_in.dtype)
    return jnp.concatenate([w_in[:, :main], conv, z, pad], axis=1).astype(BF16)


def _pack_gate(gf_w, gf_b, gb_w, gb_b):
    w = jnp.zeros((LANES, 2 * GLA_KEY_W), F32)
    w = w.at[:GATE_RANK, :GLA_KEY_W].set(gf_w)
    w = w.at[GATE_RANK:2 * GATE_RANK, GLA_KEY_W:].set(gb_w)
    return w.astype(BF16), jnp.concatenate([gf_b, gb_b]).reshape(1, -1).astype(F32)


def kernel(x, mem, ffn1_norm, ffn1_w_gate, ffn1_w_up, ffn1_w_down, mix_norm, w_in, gate_fwd_w, gate_fwd_b,
           gate_bwd_w, gate_bwd_b, gla_norm, conv_w, conv_b, w_out, xattn_norm, mem_norm, xattn_w_q,
           xattn_w_kv, xattn_w_o, ffn2_norm, ffn2_w_gate, ffn2_w_up, ffn2_w_down, final_norm):
    batch, seq, d = x.shape
    depth = w_in.shape[0]
    assert seq % GLA_BLOCK == 0 and GLA_BLOCK % CHUNK == 0 and seq % TOKEN_TILE == 0
    h = x.reshape(batch * seq, d)
    bf = lambda w: w.astype(BF16)
    for l in range(depth):
        wgate, gbias = _pack_gate(gate_fwd_w[l], gate_fwd_b[l], gate_bwd_w[l], gate_bwd_b[l])
        h, q, k, v, r, bg, u, bcf, bcb = _pre(
            h, ffn1_norm[l], bf(ffn1_w_gate[l]), bf(ffn1_w_up[l]), bf(ffn1_w_down[l]),
            mix_norm[l], _pack_w_in(w_in[l]), wgate, gbias)
        o_f, o_b = _gla(q, k, v, bcf, bcb, batch, seq)
        mk, mv = _kv(mem, mem_norm[l], bf(xattn_w_kv[l]))
        h = _post(h, o_f, o_b, r, bg, u, gla_norm[l], conv_w[l], conv_b[l], bf(w_out[l]),
                  xattn_norm[l], bf(xattn_w_q[l]), mk, mv, bf(xattn_w_o[l]),
                  ffn2_norm[l], bf(ffn2_w_gate[l]), bf(ffn2_w_up[l]), bf(ffn2_w_down[l]), seq,
                  final_norm=final_norm if l == depth - 1 else None)
    return h.reshape(batch, seq, d)
```

```python
import functools

import jax
import jax.numpy as jnp
from jax import lax
from jax.experimental import pallas as pl
from jax.experimental.pallas import tpu as pltpu

F32 = jnp.float32
BF16 = jnp.bfloat16

XATTN_HEADS = 4
GLA_HEADS = 4
GLA_DK = 64
GLA_DV = 128
GLA_KEY_W = GLA_HEADS * GLA_DK
GLA_VAL_W = GLA_HEADS * GLA_DV
GATE_RANK = 16
GATE_TAU = 16.0
CHUNK = 64
CONV_W = 512
EPS = 1e-6

LANES = 128
SUBLANES = 8
BF16_ROWS = 16
TOKEN_TILE = 512
GLA_BLOCK = 512
FFN_CHUNK = 256
STAGE_ROWS = 256
VMEM_LIMIT_BYTES = 58 * 1024 * 1024


def _cparams(n_axes):
    return pltpu.CompilerParams(
        dimension_semantics=("arbitrary",) * n_axes,
        vmem_limit_bytes=VMEM_LIMIT_BYTES,
    )


def _const_spec(shape):
    nd = len(shape)
    return pl.BlockSpec(shape, lambda *_: (0,) * nd, pipeline_mode=pl.Buffered(1))


_ANY_SPEC = pl.BlockSpec(memory_space=pl.ANY)


def _load_weight_bf16(src_hbm, dst_ref, stage_ref, sem_ref, src_col0=0, dst_col0=0, cols=None):
    rows = stage_ref.shape[1]
    cols = src_hbm.shape[1] if cols is None else cols
    n_chunks = src_hbm.shape[0] // rows
    assert src_hbm.shape[0] % rows == 0 and cols <= stage_ref.shape[2]
    assert cols % LANES == 0 and src_col0 % LANES == 0 and dst_col0 % LANES == 0

    def copy(chunk, slot):
        return pltpu.make_async_copy(src_hbm.at[pl.ds(chunk * rows, rows), pl.ds(src_col0, cols)],
                                     stage_ref.at[slot, :, pl.ds(0, cols)], sem_ref.at[slot])

    copy(0, 0).start()

    def body(c, carry):
        slot = c % 2

        @pl.when(c + 1 < n_chunks)
        def _():
            copy(c + 1, 1 - slot).start()

        copy(c, slot).wait()
        r0 = pl.multiple_of(c * rows, rows)
        dst_ref[pl.ds(r0, rows), pl.ds(dst_col0, cols)] = stage_ref[slot, :, 0:cols].astype(BF16)
        return carry

    lax.fori_loop(0, n_chunks, body, 0)


def _rms(x, g):
    ms = jnp.mean(x * x, axis=-1, keepdims=True)
    return x * lax.rsqrt(ms + EPS) * g


def _dot(a, b):
    return jnp.dot(a, b, preferred_element_type=F32)


def _dot_nt(a, b):
    return lax.dot_general(a, b, (((1,), (1,)), ((), ())), preferred_element_type=F32)


def _swiglu_half_step(x, n_ref, wg_ref, wu_ref, wd_ref, a_ref):
    h = _rms(x, n_ref[...]).astype(BF16)
    for c in range(wg_ref.shape[1] // FFN_CHUNK):
        sl = slice(c * FFN_CHUNK, (c + 1) * FFN_CHUNK)
        g = _dot(h, wg_ref[:, sl])
        u = _dot(h, wu_ref[:, sl])
        a_ref[:, sl] = (g * jax.nn.sigmoid(g) * u).astype(BF16)
    return x + 0.5 * _dot(a_ref[...], wd_ref[...])


_P_Q, _P_K, _P_V, _P_R = 0, GLA_KEY_W, 2 * GLA_KEY_W, 2 * GLA_KEY_W + GLA_VAL_W
_P_BG = _P_R + GLA_VAL_W
_P_CG = _P_BG + CONV_W
_P_XV = _P_CG + CONV_W
_P_Z = _P_XV + CONV_W
_P_END = _P_Z + LANES


def _pre_body(x_ref, fn_ref, wg_hbm, wu_hbm, wd_hbm, mn_ref, win_hbm, wconv_hbm, wz_hbm, wgate_ref, gbias_ref,
              x1_ref, q_ref, k_ref, v_ref, r_ref, bg_ref, u_ref, bcf_ref, bcb_ref,
              a_ref, wg_ref, wu_ref, wd_ref, wp_ref, stage_ref, sem_ref):
    @pl.when(pl.program_id(0) == 0)
    def _():
        _load_weight_bf16(wg_hbm, wg_ref, stage_ref, sem_ref)
        _load_weight_bf16(wu_hbm, wu_ref, stage_ref, sem_ref)
        _load_weight_bf16(wd_hbm, wd_ref, stage_ref, sem_ref)
        _load_weight_bf16(win_hbm, wp_ref, stage_ref, sem_ref, cols=_P_BG)
        _load_weight_bf16(wconv_hbm, wp_ref, stage_ref, sem_ref, dst_col0=_P_BG)
        _load_weight_bf16(wz_hbm, wp_ref, stage_ref, sem_ref, dst_col0=_P_Z)

    x1 = _swiglu_half_step(x_ref[...], fn_ref, wg_ref, wu_ref, wd_ref, a_ref)
    x1_ref[...] = x1
    h = _rms(x1, mn_ref[...]).astype(BF16)
    p = _dot(h, wp_ref[...])
    q_ref[...] = p[:, _P_Q:_P_K] * (GLA_DK ** -0.5)
    k_ref[...] = p[:, _P_K:_P_V]
    v_ref[...] = p[:, _P_V:_P_R].astype(BF16)
    r_ref[...] = p[:, _P_R:_P_BG].astype(BF16)
    bg_ref[...] = p[:, _P_BG:_P_CG].astype(BF16)
    u_ref[...] = (p[:, _P_CG:_P_XV] * p[:, _P_XV:_P_Z]).astype(BF16)
    gz = _dot(p[:, _P_Z:_P_END].astype(BF16), wgate_ref[...]) + gbias_ref[...]
    lg = jax.nn.log_sigmoid(gz) * (1.0 / GATE_TAU)
    row = lax.broadcasted_iota(jnp.int32, (CHUNK, CHUNK), 0)
    col = lax.broadcasted_iota(jnp.int32, (CHUNK, CHUNK), 1)
    tri_f = (col <= row).astype(BF16)
    tri_b = (col >= row).astype(BF16)
    for c in range(x_ref.shape[0] // CHUNK):
        rows = slice(c * CHUNK, (c + 1) * CHUNK)
        for tri, lo_lane, out_ref in ((tri_f, 0, bcf_ref), (tri_b, GLA_KEY_W, bcb_ref)):
            g = lg[rows, lo_lane:lo_lane + GLA_KEY_W]
            g_hi = g.astype(BF16)
            g_lo = (g - g_hi.astype(F32)).astype(BF16)
            s = _dot(tri, jnp.concatenate([g_hi, g_lo], axis=1))
            out_ref[rows, :] = s[:, :GLA_KEY_W] + s[:, GLA_KEY_W:]


def _pre(x, ffn_norm, wg, wu, wd, mix_norm, w_in, wgate, gbias):
    t, d = x.shape
    d_ff = wg.shape[1]
    assert t % TOKEN_TILE == 0 and d_ff % FFN_CHUNK == 0 and TOKEN_TILE % CHUNK == 0
    tile = lambda w: pl.BlockSpec((TOKEN_TILE, w), lambda i: (i, 0))
    out_w = [(d, F32), (GLA_KEY_W, F32), (GLA_KEY_W, F32), (GLA_VAL_W, BF16), (GLA_VAL_W, BF16),
             (CONV_W, BF16), (CONV_W, BF16), (GLA_KEY_W, F32), (GLA_KEY_W, F32)]
    main = _P_BG
    z_w = 2 * GATE_RANK
    assert w_in.shape[1] == main + z_w + 3 * CONV_W
    w_conv = w_in[:, main + z_w:]
    w_z = jnp.pad(w_in[:, main:main + z_w], ((0, 0), (0, LANES - z_w)))
    stage_cols = max(d_ff, d, main, 3 * CONV_W)
    return pl.pallas_call(
        _pre_body,
        grid=(t // TOKEN_TILE,),
        in_specs=[tile(d), _const_spec((1, d)), _ANY_SPEC, _ANY_SPEC, _ANY_SPEC, _const_spec((1, d)),
                  _ANY_SPEC, _ANY_SPEC, _ANY_SPEC, _const_spec(wgate.shape), _const_spec(gbias.shape)],
        out_specs=[tile(w) for w, _ in out_w],
        out_shape=[jax.ShapeDtypeStruct((t, w), dt) for w, dt in out_w],
        scratch_shapes=[pltpu.VMEM((TOKEN_TILE, d_ff), BF16),
                        pltpu.VMEM((d, d_ff), BF16), pltpu.VMEM((d, d_ff), BF16), pltpu.VMEM((d_ff, d), BF16),
                        pltpu.VMEM((d, _P_END), BF16),
                        pltpu.VMEM((2, STAGE_ROWS, stage_cols), F32), pltpu.SemaphoreType.DMA((2,))],
        compiler_params=_cparams(1),
        name="pre",
    )(x, ffn_norm.reshape(1, d), wg, wu, wd, mix_norm.reshape(1, d), w_in, w_conv, w_z, wgate, gbias)


def _gla_direction(q_ref, k_ref, v_ref, bc_ref, o_ref, s_ref, *, reverse):
    n_chunks = q_ref.shape[0] // CHUNK
    kw, vw = GLA_KEY_W, GLA_VAL_W
    assert GLA_DK == CHUNK and CHUNK & (CHUNK - 1) == 0
    shift = CHUNK.bit_length() - 1
    row_head = lax.broadcasted_iota(jnp.int32, (kw, kw), 0) >> shift
    lane_head = lax.broadcasted_iota(jnp.int32, (kw, kw), 1) >> shift
    same_head = row_head == lane_head
    pos_q = lax.broadcasted_iota(jnp.int32, (CHUNK, kw), 0)
    pos_k = lax.broadcasted_iota(jnp.int32, (CHUNK, kw), 1) & (CHUNK - 1)
    causal = (pos_k >= pos_q) if reverse else (pos_k <= pos_q)
    zero_v = jnp.zeros((CHUNK, GLA_DV), BF16)

    order = range(n_chunks - 1, -1, -1) if reverse else range(n_chunks)
    for c in order:
        rows = slice(c * CHUNK, (c + 1) * CHUNK)
        bc = bc_ref[rows, :]
        q = q_ref[rows, :]
        k = k_ref[rows, :]
        v = v_ref[rows, :]
        b_last = bc[0:1, :] if reverse else bc[CHUNK - 1:CHUNK, :]
        q_in = (q * jnp.exp(bc)).astype(BF16)
        k_in = k * jnp.exp(-bc)
        k_st = k * jnp.exp(b_last - bc)
        k_blk = jnp.where(same_head, jnp.concatenate([k_in] * GLA_HEADS, axis=0), 0.0).astype(BF16)
        att = _dot_nt(q_in, k_blk)
        att = jnp.where(causal, att, 0.0).astype(BF16)
        v_blk = jnp.concatenate(
            [jnp.concatenate([v[:, h * GLA_DV:(h + 1) * GLA_DV] if h == hh else zero_v
                              for hh in range(GLA_HEADS)], axis=1) for h in range(GLA_HEADS)], axis=0)
        s_in = s_ref[...]
        o_ref[rows, :] = (_dot(att, v_blk) + _dot(q_in, s_in.astype(BF16))).astype(o_ref.dtype)
        decay_rows = jnp.broadcast_to(jnp.exp(b_last), (SUBLANES, kw))
        pad = jnp.zeros((LANES - CHUNK - SUBLANES, kw), F32)
        xt = jnp.concatenate([k_st, decay_rows, pad], axis=0).T
        v_aug = jnp.concatenate([v, jnp.zeros((LANES - CHUNK, vw), BF16)], axis=0)
        ds = _dot(xt.astype(BF16), v_aug)
        decay = xt[:, CHUNK:CHUNK + 1]
        for h in range(GLA_HEADS):
            r_sl = slice(h * GLA_DK, (h + 1) * GLA_DK)
            c_sl = slice(h * GLA_DV, (h + 1) * GLA_DV)
            s_ref[r_sl, c_sl] = decay[r_sl, :] * s_in[r_sl, c_sl] + ds[r_sl, c_sl]


def _gla_body(qf_ref, kf_ref, vf_ref, bf_ref, qb_ref, kb_ref, vb_ref, bb_ref, of_ref, ob_ref, sf_ref, sb_ref):
    @pl.when(pl.program_id(1) == 0)
    def _():
        sf_ref[...] = jnp.zeros_like(sf_ref)
        sb_ref[...] = jnp.zeros_like(sb_ref)

    _gla_direction(qf_ref, kf_ref, vf_ref, bf_ref, of_ref, sf_ref, reverse=False)
    _gla_direction(qb_ref, kb_ref, vb_ref, bb_ref, ob_ref, sb_ref, reverse=True)


def _gla(q, k, v, bcf, bcb, batch, seq):
    nb = seq // GLA_BLOCK
    fwd = lambda w: pl.BlockSpec((GLA_BLOCK, w), lambda b, j: (b * nb + j, 0))
    bwd = lambda w: pl.BlockSpec((GLA_BLOCK, w), lambda b, j: (b * nb + nb - 1 - j, 0))
    t = batch * seq
    return pl.pallas_call(
        _gla_body,
        grid=(batch, nb),
        in_specs=[fwd(GLA_KEY_W), fwd(GLA_KEY_W), fwd(GLA_VAL_W), fwd(GLA_KEY_W),
                  bwd(GLA_KEY_W), bwd(GLA_KEY_W), bwd(GLA_VAL_W), bwd(GLA_KEY_W)],
        out_specs=[fwd(GLA_VAL_W), bwd(GLA_VAL_W)],
        out_shape=[jax.ShapeDtypeStruct((t, GLA_VAL_W), BF16)] * 2,
        scratch_shapes=[pltpu.VMEM((GLA_KEY_W, GLA_VAL_W), F32)] * 2,
        compiler_params=_cparams(2),
        name="gla",
    )(q, k, v, bcf, q, k, v, bcb)


def _kv_body(m_ref, n_ref, w_ref, k_ref, v_ref):
    d = m_ref.shape[-1]
    h = _rms(m_ref[...], n_ref[...]).astype(BF16)
    kv = _dot(h, w_ref[...].astype(BF16))
    k_ref[...] = kv[:, :d].astype(BF16)
    v_ref[...] = kv[:, d:].astype(BF16)


def _kv(mem, norm, w_kv):
    b, m, d = mem.shape
    k, v = pl.pallas_call(
        _kv_body,
        grid=(1,),
        in_specs=[_const_spec((b * m, d)), _const_spec((1, d)), _const_spec(w_kv.shape)],
        out_specs=[_const_spec((b * m, d))] * 2,
        out_shape=[jax.ShapeDtypeStruct((b * m, d), BF16)] * 2,
        compiler_params=_cparams(1),
        name="kv",
    )(mem.reshape(b * m, d), norm.reshape(1, d), w_kv)
    return k.reshape(b, m, d), v.reshape(b, m, d)


def _token_mix_out(x, of_ref, ob_ref, r_ref, bg_ref, u_ref, up_ref, un_ref, gn_ref, cw_ref, cb_ref, wo_ref, seq):
    tm = x.shape[0]
    i = pl.program_id(0)
    o = of_ref[...].astype(F32) + ob_ref[...].astype(F32)
    heads = []
    for h in range(GLA_HEADS):
        oh = o[:, h * GLA_DV:(h + 1) * GLA_DV]
        heads.append(oh * lax.rsqrt(jnp.mean(oh * oh, axis=-1, keepdims=True) + EPS))
    r = r_ref[...].astype(F32)
    a_out = jnp.concatenate(heads, axis=1) * gn_ref[...] * (r * jax.nn.sigmoid(r))
    u = u_ref[...].astype(F32)
    at_seq_start = (i * tm) % seq == 0
    at_seq_end = ((i + 1) * tm) % seq == 0
    halo_prev = jnp.where(at_seq_start, 0.0, up_ref[BF16_ROWS - 1:BF16_ROWS, :].astype(F32))
    halo_next = jnp.where(at_seq_end, 0.0, un_ref[0:1, :].astype(F32))
    row = lax.broadcasted_iota(jnp.int32, u.shape, 0)
    u_prev = jnp.where(row == 0, halo_prev, pltpu.roll(u, 1, 0))
    u_next = jnp.where(row == tm - 1, halo_next, pltpu.roll(u, tm - 1, 0))
    conv = cw_ref[0:1, :] * u_prev + cw_ref[1:2, :] * u + cw_ref[2:3, :] * u_next + cb_ref[...]
    c_out = bg_ref[...].astype(F32) * conv
    mixed = jnp.concatenate([a_out, c_out], axis=1).astype(BF16)
    return x + _dot(mixed, wo_ref[...])


def _cross_attention(x, n_ref, wq_ref, k_ref, v_ref, wo_ref):
    d = x.shape[-1]
    hd = d // XATTN_HEADS
    h = _rms(x, n_ref[...]).astype(BF16)
    q = _dot(h, wq_ref[...])
    outs = []
    for a in range(XATTN_HEADS):
        sl = slice(a * hd, (a + 1) * hd)
        s = _dot_nt(q[:, sl].astype(BF16), k_ref[0, :, sl]) * (hd ** -0.5)
        p = jnp.exp(s - jnp.max(s, axis=-1, keepdims=True))
        denom = jnp.sum(p, axis=-1, keepdims=True)
        outs.append(_dot(p.astype(BF16), v_ref[0, :, sl]) / denom)
    o = jnp.concatenate(outs, axis=1).astype(BF16)
    return x + _dot(o, wo_ref[...])


def _post_body(*refs, seq, final):
    (x_ref, of_ref, ob_ref, r_ref, bg_ref, u_ref, up_ref, un_ref, gn_ref, cw_ref, cb_ref, wo_hbm,
     xn_ref, wq_hbm, k_ref, v_ref, wxo_hbm, fn_ref, wg_hbm, wu_hbm, wd_hbm) = refs[:21]
    refs = refs[21:]
    if final:
        last_ref, refs = refs[0], refs[1:]
    o_ref, a_ref, wo_ref, wq_ref, wxo_ref, wg_ref, wu_ref, wd_ref, stage_ref, sem_ref = refs

    @pl.when(pl.program_id(0) == 0)
    def _():
        for src, dst in ((wo_hbm, wo_ref), (wq_hbm, wq_ref), (wxo_hbm, wxo_ref),
                         (wg_hbm, wg_ref), (wu_hbm, wu_ref), (wd_hbm, wd_ref)):
            _load_weight_bf16(src, dst, stage_ref, sem_ref)

    x2 = _token_mix_out(x_ref[...], of_ref, ob_ref, r_ref, bg_ref, u_ref, up_ref, un_ref, gn_ref, cw_ref, cb_ref,
                        wo_ref, seq)
    x3 = _cross_attention(x2, xn_ref, wq_ref, k_ref, v_ref, wxo_ref)
    y = _swiglu_half_step(x3, fn_ref, wg_ref, wu_ref, wd_ref, a_ref)
    if final:
        y = _rms(y, last_ref[...])
    o_ref[...] = y


def _post(x, o_f, o_b, r, bg, u, gla_norm, conv_w, conv_b, w_out, xattn_norm, w_q, mk, mv, w_xo,
          ffn_norm, wg, wu, wd, seq, final_norm=None):
    t, d = x.shape
    tm = TOKEN_TILE
    d_ff = wg.shape[1]
    m = mk.shape[1]
    assert seq % tm == 0 and tm % BF16_ROWS == 0 and d_ff % FFN_CHUNK == 0
    final = final_norm is not None
    tile = lambda w: pl.BlockSpec((tm, w), lambda i: (i, 0))
    hb = tm // BF16_ROWS
    n_halo = t // BF16_ROWS
    halo_prev = pl.BlockSpec((BF16_ROWS, CONV_W), lambda i: (jnp.maximum(i * hb - 1, 0), 0))
    halo_next = pl.BlockSpec((BF16_ROWS, CONV_W), lambda i: (jnp.minimum((i + 1) * hb, n_halo - 1), 0))
    kv_blk = pl.BlockSpec((1, m, d), lambda i: (i // (seq // tm), 0, 0))
    in_specs = [tile(d), tile(GLA_VAL_W), tile(GLA_VAL_W), tile(GLA_VAL_W), tile(CONV_W), tile(CONV_W),
                halo_prev, halo_next, _const_spec((1, GLA_VAL_W)), _const_spec(conv_w.shape),
                _const_spec((1, CONV_W)), _ANY_SPEC,
                _const_spec((1, d)), _ANY_SPEC, kv_blk, kv_blk, _ANY_SPEC,
                _const_spec((1, d)), _ANY_SPEC, _ANY_SPEC, _ANY_SPEC]
    args = [x, o_f, o_b, r, bg, u, u, u, gla_norm.reshape(1, -1), conv_w, conv_b.reshape(1, -1), w_out,
            xattn_norm.reshape(1, d), w_q, mk, mv, w_xo, ffn_norm.reshape(1, d), wg, wu, wd]
    if final:
        in_specs.append(_const_spec((1, d)))
        args.append(final_norm.reshape(1, d))
    return pl.pallas_call(
        functools.partial(_post_body, seq=seq, final=final),
        grid=(t // tm,),
        in_specs=in_specs,
        out_specs=tile(d),
        out_shape=jax.ShapeDtypeStruct((t, d), F32),
        scratch_shapes=[pltpu.VMEM((tm, d_ff), BF16),
                        pltpu.VMEM(w_out.shape, BF16), pltpu.VMEM(w_q.shape, BF16), pltpu.VMEM(w_xo.shape, BF16),
                        pltpu.VMEM((d, d_ff), BF16), pltpu.VMEM((d, d_ff), BF16), pltpu.VMEM((d_ff, d), BF16),
                        pltpu.VMEM((2, STAGE_ROWS, max(d_ff, d)), F32), pltpu.SemaphoreType.DMA((2,))],
        compiler_params=_cparams(1),
        name="post_final" if final else "post",
    )(*args)


def _pack_gate(gf_w, gf_b, gb_w, gb_b):
    w = jnp.zeros((LANES, 2 * GLA_KEY_W), F32)
    w = w.at[:GATE_RANK, :GLA_KEY_W].set(gf_w)
    w = w.at[GATE_RANK:2 * GATE_RANK, GLA_KEY_W:].set(gb_w)
    return w.astype(BF16), jnp.concatenate([gf_b, gb_b]).reshape(1, -1).astype(F32)


def kernel(x, mem, ffn1_norm, ffn1_w_gate, ffn1_w_up, ffn1_w_down, mix_norm, w_in, gate_fwd_w, gate_fwd_b,
           gate_bwd_w, gate_bwd_b, gla_norm, conv_w, conv_b, w_out, xattn_norm, mem_norm, xattn_w_q,
           xattn_w_kv, xattn_w_o, ffn2_norm, ffn2_w_gate, ffn2_w_up, ffn2_w_down, final_norm):
    batch, seq, d = x.shape
    depth = w_in.shape[0]
    assert seq % GLA_BLOCK == 0 and GLA_BLOCK % CHUNK == 0 and seq % TOKEN_TILE == 0
    h = x.reshape(batch * seq, d)
    for l in range(depth):
        wgate, gbias = _pack_gate(gate_fwd_w[l], gate_fwd_b[l], gate_bwd_w[l], gate_bwd_b[l])
        h, q, k, v, r, bg, u, bcf, bcb = _pre(
            h, ffn1_norm[l], ffn1_w_gate[l], ffn1_w_up[l], ffn1_w_down[l], mix_norm[l], w_in[l], wgate, gbias)
        o_f, o_b = _gla(q, k, v, bcf, bcb, batch, seq)
        mk, mv = _kv(mem, mem_norm[l], xattn_w_kv[l])
        h = _post(h, o_f, o_b, r, bg, u, gla_norm[l], conv_w[l], conv_b[l], w_out[l],
                  xattn_norm[l], xattn_w_q[l], mk, mv, xattn_w_o[l],
                  ffn2_norm[l], ffn2_w_gate[l], ffn2_w_up[l], ffn2_w_down[l], seq,
                  final_norm=final_norm if l == depth - 1 else None)
    return h.reshape(batch, seq, d)
```

```python
import functools

import jax
import jax.numpy as jnp
from jax import lax
from jax.experimental import pallas as pl
from jax.experimental.pallas import tpu as pltpu

F32 = jnp.float32
BF16 = jnp.bfloat16

XATTN_HEADS = 4
GLA_HEADS = 4
GLA_DK = 64
GLA_DV = 128
GLA_KEY_W = GLA_HEADS * GLA_DK
GLA_VAL_W = GLA_HEADS * GLA_DV
GATE_RANK = 16
GATE_TAU = 16.0
CHUNK = 64
CONV_W = 512
EPS = 1e-6

LANES = 128
SUBLANES = 8
BF16_ROWS = 16
TOKEN_TILE = 512
GLA_BLOCK = 512
FFN_CHUNK = 256
STAGE_ROWS = 128
STAGE_SLOTS = 4
VMEM_LIMIT_BYTES = 58 * 1024 * 1024


def _cparams(n_axes):
    return pltpu.CompilerParams(
        dimension_semantics=("arbitrary",) * n_axes,
        vmem_limit_bytes=VMEM_LIMIT_BYTES,
    )


def _const_spec(shape):
    nd = len(shape)
    return pl.BlockSpec(shape, lambda *_: (0,) * nd, pipeline_mode=pl.Buffered(1))


_ANY_SPEC = pl.BlockSpec(memory_space=pl.ANY)


def _load_weights_bf16(jobs, stage_ref, sem_ref):
    n_slots, rows = stage_ref.shape[0], stage_ref.shape[1]
    ahead = n_slots - 1
    counts = [n_rows // rows for _, _, n_rows, _, _ in jobs]
    for (src, _, n_rows, dst, _), n in zip(jobs, counts):
        assert n_rows % rows == 0 and n >= ahead and src.shape[1] == dst.shape[1] <= stage_ref.shape[2]
    bases = [sum(counts[:j]) for j in range(len(jobs))]
    stream = [(j, c) for j, n in enumerate(counts) for c in range(n)]

    def copy(j, c, g):
        src, src_row0, _, _, _ = jobs[j]
        r0 = pl.multiple_of(src_row0 + c * rows, SUBLANES)
        return pltpu.make_async_copy(src.at[pl.ds(r0, rows), :],
                                     stage_ref.at[g % n_slots, :, pl.ds(0, src.shape[1])], sem_ref.at[g % n_slots])

    def finish(j, c, g):
        _, _, _, dst, dst_row0 = jobs[j]
        copy(j, c, g).wait()
        r0 = pl.multiple_of(dst_row0 + c * rows, BF16_ROWS)
        dst[pl.ds(r0, rows), :] = stage_ref[g % n_slots, :, 0:dst.shape[1]].astype(BF16)

    for g in range(ahead):
        copy(*stream[g], g).start()
    for j, n in enumerate(counts):
        def body(c, carry, j=j):
            copy(j, c + ahead, bases[j] + c + ahead).start()
            finish(j, c, bases[j] + c)
            return carry

        lax.fori_loop(0, n - ahead, body, 0)
        for c in range(n - ahead, n):
            g = bases[j] + c
            if g + ahead < len(stream):
                copy(*stream[g + ahead], g + ahead).start()
            finish(j, c, g)


def _rms(x, g):
    ms = jnp.mean(x * x, axis=-1, keepdims=True)
    return x * lax.rsqrt(ms + EPS) * g


def _dot(a, b):
    return jnp.dot(a, b, preferred_element_type=F32)


def _dot_nt(a, b):
    return lax.dot_general(a, b, (((1,), (1,)), ((), ())), preferred_element_type=F32)


def _swiglu_half_step(x, n_ref, wg_ref, wu_ref, wd_ref, a_ref):
    h = _rms(x, n_ref[...]).astype(BF16)
    for c in range(wg_ref.shape[1] // FFN_CHUNK):
        sl = slice(c * FFN_CHUNK, (c + 1) * FFN_CHUNK)
        g = _dot(h, wg_ref[:, sl])
        u = _dot(h, wu_ref[:, sl])
        a_ref[:, sl] = (g * jax.nn.sigmoid(g) * u).astype(BF16)
    return x + 0.5 * _dot(a_ref[...], wd_ref[...])


_P_Q, _P_K, _P_V, _P_R = 0, GLA_KEY_W, 2 * GLA_KEY_W, 2 * GLA_KEY_W + GLA_VAL_W
_P_BG = _P_R + GLA_VAL_W
_P_CG = _P_BG + CONV_W
_P_XV = _P_CG + CONV_W
_P_Z = _P_XV + CONV_W
_P_END = _P_Z + LANES


def _pre_body(x_ref, fn_ref, wg_hbm, wu_hbm, wd_hbm, mn_ref, wint_hbm, wgate_ref, gbias_ref,
              x1_ref, q_ref, k_ref, v_ref, r_ref, bg_ref, u_ref, bcf_ref, bcb_ref,
              a_ref, wg_ref, wu_ref, wd_ref, wp_ref, stage_ref, sem_ref):
    @pl.when(pl.program_id(0) == 0)
    def _():
        d, d_ff = wg_hbm.shape
        z_w = 2 * GATE_RANK
        main = _P_BG
        conv = _P_Z - _P_BG
        _load_weights_bf16(
            [(wg_hbm, 0, d, wg_ref, 0), (wu_hbm, 0, d, wu_ref, 0), (wd_hbm, 0, d_ff, wd_ref, 0),
             (wint_hbm, 0, main, wp_ref, 0), (wint_hbm, main + z_w, conv, wp_ref, _P_BG)],
            stage_ref, sem_ref)
        z_copy = pltpu.make_async_copy(wint_hbm.at[pl.ds(main, z_w), :],
                                       stage_ref.at[0, pl.ds(0, z_w), pl.ds(0, d)], sem_ref.at[0])
        z_copy.start()
        z_copy.wait()
        wp_ref[_P_Z:_P_Z + z_w, :] = stage_ref[0, 0:z_w, 0:d].astype(BF16)
        wp_ref[_P_Z + z_w:_P_END, :] = jnp.zeros((LANES - z_w, d), BF16)

    x1 = _swiglu_half_step(x_ref[...], fn_ref, wg_ref, wu_ref, wd_ref, a_ref)
    x1_ref[...] = x1
    h = _rms(x1, mn_ref[...]).astype(BF16)
    p = _dot_nt(h, wp_ref[...])
    q_ref[...] = p[:, _P_Q:_P_K] * (GLA_DK ** -0.5)
    k_ref[...] = p[:, _P_K:_P_V]
    v_ref[...] = p[:, _P_V:_P_R].astype(BF16)
    r_ref[...] = p[:, _P_R:_P_BG].astype(BF16)
    bg_ref[...] = p[:, _P_BG:_P_CG].astype(BF16)
    u_ref[...] = (p[:, _P_CG:_P_XV] * p[:, _P_XV:_P_Z]).astype(BF16)
    gz = _dot(p[:, _P_Z:_P_END].astype(BF16), wgate_ref[...]) + gbias_ref[...]
    lg = jax.nn.log_sigmoid(gz) * (1.0 / GATE_TAU)
    row = lax.broadcasted_iota(jnp.int32, (CHUNK, CHUNK), 0)
    col = lax.broadcasted_iota(jnp.int32, (CHUNK, CHUNK), 1)
    tri_f = (col <= row).astype(BF16)
    tri_b = (col >= row).astype(BF16)
    for c in range(x_ref.shape[0] // CHUNK):
        rows = slice(c * CHUNK, (c + 1) * CHUNK)
        for tri, lo_lane, out_ref in ((tri_f, 0, bcf_ref), (tri_b, GLA_KEY_W, bcb_ref)):
            g = lg[rows, lo_lane:lo_lane + GLA_KEY_W]
            g_hi = g.astype(BF16)
            g_lo = (g - g_hi.astype(F32)).astype(BF16)
            s = _dot(tri, jnp.concatenate([g_hi, g_lo], axis=1))
            out_ref[rows, :] = s[:, :GLA_KEY_W] + s[:, GLA_KEY_W:]


def _pre(x, ffn_norm, wg, wu, wd, mix_norm, w_in, wgate, gbias):
    t, d = x.shape
    d_ff = wg.shape[1]
    assert t % TOKEN_TILE == 0 and d_ff % FFN_CHUNK == 0 and TOKEN_TILE % CHUNK == 0
    tile = lambda w: pl.BlockSpec((TOKEN_TILE, w), lambda i: (i, 0))
    out_w = [(d, F32), (GLA_KEY_W, F32), (GLA_KEY_W, F32), (GLA_VAL_W, BF16), (GLA_VAL_W, BF16),
             (CONV_W, BF16), (CONV_W, BF16), (GLA_KEY_W, F32), (GLA_KEY_W, F32)]
    assert w_in.shape[1] == _P_Z + 2 * GATE_RANK
    return pl.pallas_call(
        _pre_body,
        grid=(t // TOKEN_TILE,),
        in_specs=[tile(d), _const_spec((1, d)), _ANY_SPEC, _ANY_SPEC, _ANY_SPEC, _const_spec((1, d)),
                  _ANY_SPEC, _const_spec(wgate.shape), _const_spec(gbias.shape)],
        out_specs=[tile(w) for w, _ in out_w],
        out_shape=[jax.ShapeDtypeStruct((t, w), dt) for w, dt in out_w],
        scratch_shapes=[pltpu.VMEM((TOKEN_TILE, d_ff), BF16),
                        pltpu.VMEM((d, d_ff), BF16), pltpu.VMEM((d, d_ff), BF16), pltpu.VMEM((d_ff, d), BF16),
                        pltpu.VMEM((_P_END, d), BF16),
                        pltpu.VMEM((STAGE_SLOTS, STAGE_ROWS, max(d_ff, d)), F32),
                        pltpu.SemaphoreType.DMA((STAGE_SLOTS,))],
        compiler_params=_cparams(1),
        name="pre",
    )(x, ffn_norm.reshape(1, d), wg, wu, wd, mix_norm.reshape(1, d), w_in.T, wgate, gbias)


def _gla_direction(q_ref, k_ref, v_ref, bc_ref, o_ref, s_ref, *, reverse):
    n_chunks = q_ref.shape[0] // CHUNK
    kw, vw = GLA_KEY_W, GLA_VAL_W
    assert GLA_DK == CHUNK and CHUNK & (CHUNK - 1) == 0
    shift = CHUNK.bit_length() - 1
    row_head = lax.broadcasted_iota(jnp.int32, (kw, kw), 0) >> shift
    lane_head = lax.broadcasted_iota(jnp.int32, (kw, kw), 1) >> shift
    same_head = row_head == lane_head
    pos_q = lax.broadcasted_iota(jnp.int32, (CHUNK, kw), 0)
    pos_k = lax.broadcasted_iota(jnp.int32, (CHUNK, kw), 1) & (CHUNK - 1)
    causal = (pos_k >= pos_q) if reverse else (pos_k <= pos_q)
    zero_v = jnp.zeros((CHUNK, GLA_DV), BF16)

    order = range(n_chunks - 1, -1, -1) if reverse else range(n_chunks)
    for c in order:
        rows = slice(c * CHUNK, (c + 1) * CHUNK)
        bc = bc_ref[rows, :]
        q = q_ref[rows, :]
        k = k_ref[rows, :]
        v = v_ref[rows, :]
        b_last = bc[0:1, :] if reverse else bc[CHUNK - 1:CHUNK, :]
        q_in = (q * jnp.exp(bc)).astype(BF16)
        k_in = k * jnp.exp(-bc)
        k_st = k * jnp.exp(b_last - bc)
        k_blk = jnp.where(same_head, jnp.concatenate([k_in] * GLA_HEADS, axis=0), 0.0).astype(BF16)
        att = _dot_nt(q_in, k_blk)
        att = jnp.where(causal, att, 0.0).astype(BF16)
        v_blk = jnp.concatenate(
            [jnp.concatenate([v[:, h * GLA_DV:(h + 1) * GLA_DV] if h == hh else zero_v
                              for hh in range(GLA_HEADS)], axis=1) for h in range(GLA_HEADS)], axis=0)
        s_in = s_ref[...]
        o_ref[rows, :] = (_dot(att, v_blk) + _dot(q_in, s_in.astype(BF16))).astype(o_ref.dtype)
        decay_rows = jnp.broadcast_to(jnp.exp(b_last), (SUBLANES, kw))
        pad = jnp.zeros((LANES - CHUNK - SUBLANES, kw), F32)
        xt = jnp.concatenate([k_st, decay_rows, pad], axis=0).T
        v_aug = jnp.concatenate([v, jnp.zeros((LANES - CHUNK, vw), BF16)], axis=0)
        ds = _dot(xt.astype(BF16), v_aug)
        decay = xt[:, CHUNK:CHUNK + 1]
        for h in range(GLA_HEADS):
            r_sl = slice(h * GLA_DK, (h + 1) * GLA_DK)
            c_sl = slice(h * GLA_DV, (h + 1) * GLA_DV)
            s_ref[r_sl, c_sl] = decay[r_sl, :] * s_in[r_sl, c_sl] + ds[r_sl, c_sl]


def _gla_body(qf_ref, kf_ref, vf_ref, bf_ref, qb_ref, kb_ref, vb_ref, bb_ref, of_ref, ob_ref, sf_ref, sb_ref):
    @pl.when(pl.program_id(1) == 0)
    def _():
        sf_ref[...] = jnp.zeros_like(sf_ref)
        sb_ref[...] = jnp.zeros_like(sb_ref)

    _gla_direction(qf_ref, kf_ref, vf_ref, bf_ref, of_ref, sf_ref, reverse=False)
    _gla_direction(qb_ref, kb_ref, vb_ref, bb_ref, ob_ref, sb_ref, reverse=True)


def _gla(q, k, v, bcf, bcb, batch, seq):
    nb = seq // GLA_BLOCK
    fwd = lambda w: pl.BlockSpec((GLA_BLOCK, w), lambda b, j: (b * nb + j, 0))
    bwd = lambda w: pl.BlockSpec((GLA_BLOCK, w), lambda b, j: (b * nb + nb - 1 - j, 0))
    t = batch * seq
    return pl.pallas_call(
        _gla_body,
        grid=(batch, nb),
        in_specs=[fwd(GLA_KEY_W), fwd(GLA_KEY_W), fwd(GLA_VAL_W), fwd(GLA_KEY_W),
                  bwd(GLA_KEY_W), bwd(GLA_KEY_W), bwd(GLA_VAL_W), bwd(GLA_KEY_W)],
        out_specs=[fwd(GLA_VAL_W), bwd(GLA_VAL_W)],
        out_shape=[jax.ShapeDtypeStruct((t, GLA_VAL_W), BF16)] * 2,
        scratch_shapes=[pltpu.VMEM((GLA_KEY_W, GLA_VAL_W), F32)] * 2,
        compiler_params=_cparams(2),
        name="gla",
    )(q, k, v, bcf, q, k, v, bcb)


def _kv_body(m_ref, n_ref, w_ref, k_ref, v_ref):
    d = m_ref.shape[-1]
    h = _rms(m_ref[...], n_ref[...]).astype(BF16)
    kv = _dot(h, w_ref[...].astype(BF16))
    k_ref[...] = kv[:, :d].astype(BF16)
    v_ref[...] = kv[:, d:].astype(BF16)


def _kv(mem, norm, w_kv):
    b, m, d = mem.shape
    k, v = pl.pallas_call(
        _kv_body,
        grid=(1,),
        in_specs=[_const_spec((b * m, d)), _const_spec((1, d)), _const_spec(w_kv.shape)],
        out_specs=[_const_spec((b * m, d))] * 2,
        out_shape=[jax.ShapeDtypeStruct((b * m, d), BF16)] * 2,
        compiler_params=_cparams(1),
        name="kv",
    )(mem.reshape(b * m, d), norm.reshape(1, d), w_kv)
    return k.reshape(b, m, d), v.reshape(b, m, d)


def _token_mix_out(x, of_ref, ob_ref, r_ref, bg_ref, u_ref, up_ref, un_ref, gn_ref, cw_ref, cb_ref, wo_ref, seq):
    tm = x.shape[0]
    i = pl.program_id(0)
    o = of_ref[...].astype(F32) + ob_ref[...].astype(F32)
    heads = []
    for h in range(GLA_HEADS):
        oh = o[:, h * GLA_DV:(h + 1) * GLA_DV]
        heads.append(oh * lax.rsqrt(jnp.mean(oh * oh, axis=-1, keepdims=True) + EPS))
    r = r_ref[...].astype(F32)
    a_out = jnp.concatenate(heads, axis=1) * gn_ref[...] * (r * jax.nn.sigmoid(r))
    u = u_ref[...].astype(F32)
    at_seq_start = (i * tm) % seq == 0
    at_seq_end = ((i + 1) * tm) % seq == 0
    halo_prev = jnp.where(at_seq_start, 0.0, up_ref[BF16_ROWS - 1:BF16_ROWS, :].astype(F32))
    halo_next = jnp.where(at_seq_end, 0.0, un_ref[0:1, :].astype(F32))
    row = lax.broadcasted_iota(jnp.int32, u.shape, 0)
    u_prev = jnp.where(row == 0, halo_prev, pltpu.roll(u, 1, 0))
    u_next = jnp.where(row == tm - 1, halo_next, pltpu.roll(u, tm - 1, 0))
    conv = cw_ref[0:1, :] * u_prev + cw_ref[1:2, :] * u + cw_ref[2:3, :] * u_next + cb_ref[...]
    c_out = bg_ref[...].astype(F32) * conv
    mixed = jnp.concatenate([a_out, c_out], axis=1).astype(BF16)
    return x + _dot(mixed, wo_ref[...])


def _cross_attention(x, n_ref, wq_ref, k_ref, v_ref, wo_ref):
    d = x.shape[-1]
    hd = d // XATTN_HEADS
    h = _rms(x, n_ref[...]).astype(BF16)
    q = _dot(h, wq_ref[...])
    outs = []
    for a in range(XATTN_HEADS):
        sl = slice(a * hd, (a + 1) * hd)
        s = _dot_nt(q[:, sl].astype(BF16), k_ref[0, :, sl]) * (hd ** -0.5)
        p = jnp.exp(s - jnp.max(s, axis=-1, keepdims=True))
        denom = jnp.sum(p, axis=-1, keepdims=True)
        outs.append(_dot(p.astype(BF16), v_ref[0, :, sl]) / denom)
    o = jnp.concatenate(outs, axis=1).astype(BF16)
    return x + _dot(o, wo_ref[...])


def _post_body(*refs, seq, final):
    (x_ref, of_ref, ob_ref, r_ref, bg_ref, u_ref, up_ref, un_ref, gn_ref, cw_ref, cb_ref, wo_hbm,
     xn_ref, wq_hbm, k_ref, v_ref, wxo_hbm, fn_ref, wg_hbm, wu_hbm, wd_hbm) = refs[:21]
    refs = refs[21:]
    if final:
        last_ref, refs = refs[0], refs[1:]
    o_ref, a_ref, wo_ref, wq_ref, wxo_ref, wg_ref, wu_ref, wd_ref, stage_ref, sem_ref = refs

    @pl.when(pl.program_id(0) == 0)
    def _():
        _load_weights_bf16(
            [(src, 0, src.shape[0], dst, 0) for src, dst in
             ((wo_hbm, wo_ref), (wq_hbm, wq_ref), (wxo_hbm, wxo_ref), (wg_hbm, wg_ref), (wu_hbm, wu_ref),
              (wd_hbm, wd_ref))],
            stage_ref, sem_ref)

    x2 = _token_mix_out(x_ref[...], of_ref, ob_ref, r_ref, bg_ref, u_ref, up_ref, un_ref, gn_ref, cw_ref, cb_ref,
                        wo_ref, seq)
    x3 = _cross_attention(x2, xn_ref, wq_ref, k_ref, v_ref, wxo_ref)
    y = _swiglu_half_step(x3, fn_ref, wg_ref, wu_ref, wd_ref, a_ref)
    if final:
        y = _rms(y, last_ref[...])
    o_ref[...] = y


def _post(x, o_f, o_b, r, bg, u, gla_norm, conv_w, conv_b, w_out, xattn_norm, w_q, mk, mv, w_xo,
          ffn_norm, wg, wu, wd, seq, final_norm=None):
    t, d = x.shape
    tm = TOKEN_TILE
    d_ff = wg.shape[1]
    m = mk.shape[1]
    assert seq % tm == 0 and tm % BF16_ROWS == 0 and d_ff % FFN_CHUNK == 0
    final = final_norm is not None
    tile = lambda w: pl.BlockSpec((tm, w), lambda i: (i, 0))
    hb = tm // BF16_ROWS
    n_halo = t // BF16_ROWS
    halo_prev = pl.BlockSpec((BF16_ROWS, CONV_W), lambda i: (jnp.maximum(i * hb - 1, 0), 0))
    halo_next = pl.BlockSpec((BF16_ROWS, CONV_W), lambda i: (jnp.minimum((i + 1) * hb, n_halo - 1), 0))
    kv_blk = pl.BlockSpec((1, m, d), lambda i: (i // (seq // tm), 0, 0))
    in_specs = [tile(d), tile(GLA_VAL_W), tile(GLA_VAL_W), tile(GLA_VAL_W), tile(CONV_W), tile(CONV_W),
                halo_prev, halo_next, _const_spec((1, GLA_VAL_W)), _const_spec(conv_w.shape),
                _const_spec((1, CONV_W)), _ANY_SPEC,
                _const_spec((1, d)), _ANY_SPEC, kv_blk, kv_blk, _ANY_SPEC,
                _const_spec((1, d)), _ANY_SPEC, _ANY_SPEC, _ANY_SPEC]
    args = [x, o_f, o_b, r, bg, u, u, u, gla_norm.reshape(1, -1), conv_w, conv_b.reshape(1, -1), w_out,
            xattn_norm.reshape(1, d), w_q, mk, mv, w_xo, ffn_norm.reshape(1, d), wg, wu, wd]
    if final:
        in_specs.append(_const_spec((1, d)))
        args.append(final_norm.reshape(1, d))
    return pl.pallas_call(
        functools.partial(_post_body, seq=seq, final=final),
        grid=(t // tm,),
        in_specs=in_specs,
        out_specs=tile(d),
        out_shape=jax.ShapeDtypeStruct((t, d), F32),
        scratch_shapes=[pltpu.VMEM((tm, d_ff), BF16),
                        pltpu.VMEM(w_out.shape, BF16), pltpu.VMEM(w_q.shape, BF16), pltpu.VMEM(w_xo.shape, BF16),
                        pltpu.VMEM((d, d_ff), BF16), pltpu.VMEM((d, d_ff), BF16), pltpu.VMEM((d_ff, d), BF16),
                        pltpu.VMEM((STAGE_SLOTS, STAGE_ROWS, max(d_ff, d)), F32),
                        pltpu.SemaphoreType.DMA((STAGE_SLOTS,))],
        compiler_params=_cparams(1),
        name="post_final" if final else "post",
    )(*args)


def _pack_gate(gf_w, gf_b, gb_w, gb_b):
    w = jnp.zeros((LANES, 2 * GLA_KEY_W), F32)
    w = w.at[:GATE_RANK, :GLA_KEY_W].set(gf_w)
    w = w.at[GATE_RANK:2 * GATE_RANK, GLA_KEY_W:].set(gb_w)
    return w.astype(BF16), jnp.concatenate([gf_b, gb_b]).reshape(1, -1).astype(F32)


def kernel(x, mem, ffn1_norm, ffn1_w_gate, ffn1_w_up, ffn1_w_down, mix_norm, w_in, gate_fwd_w, gate_fwd_b,
           gate_bwd_w, gate_bwd_b, gla_norm, conv_w, conv_b, w_out, xattn_norm, mem_norm, xattn_w_q,
           xattn_w_kv, xattn_w_o, ffn2_norm, ffn2_w_gate, ffn2_w_up, ffn2_w_down, final_norm):
    batch, seq, d = x.shape
    depth = w_in.shape[0]
    assert seq % GLA_BLOCK == 0 and GLA_BLOCK % CHUNK == 0 and seq % TOKEN_TILE == 0
    h = x.reshape(batch * seq, d)
    for l in range(depth):
        wgate, gbias = _pack_gate(gate_fwd_w[l], gate_fwd_b[l], gate_bwd_w[l], gate_bwd_b[l])
        h, q, k, v, r, bg, u, bcf, bcb = _pre(
            h, ffn1_norm[l], ffn1_w_gate[l], ffn1_w_up[l], ffn1_w_down[l], mix_norm[l], w_in[l], wgate, gbias)
        o_f, o_b = _gla(q, k, v, bcf, bcb, batch, seq)
        mk, mv = _kv(mem, mem_norm[l], xattn_w_kv[l])
        h = _post(h, o_f, o_b, r, bg, u, gla_norm[l], conv_w[l], conv_b[l], w_out[l],
                  xattn_norm[l], xattn_w_q[l], mk, mv, xattn_w_o[l],
                  ffn2_norm[l], ffn2_w_gate[l], ffn2_w_up[l], ffn2_w_down[l], seq,
                  final_norm=final_norm if l == depth - 1 else None)
    return h.reshape(batch, seq, d)
```

```python
import functools

import jax
import jax.numpy as jnp
from jax import lax
from jax.experimental import pallas as pl
from jax.experimental.pallas import tpu as pltpu

F32 = jnp.float32
BF16 = jnp.bfloat16

XATTN_HEADS = 4
GLA_HEADS = 4
GLA_DK = 64
GLA_DV = 128
GLA_KEY_W = GLA_HEADS * GLA_DK
GLA_VAL_W = GLA_HEADS * GLA_DV
GATE_RANK = 16
GATE_TAU = 16.0
CHUNK = 64
CONV_W = 512
EPS = 1e-6

LANES = 128
SUBLANES = 8
BF16_ROWS = 16
TOKEN_TILE = 512
GLA_BLOCK = 512
FFN_CHUNK = 256
STAGE_ROWS = 128
STAGE_SLOTS = 4
VMEM_LIMIT_BYTES = 58 * 1024 * 1024


def _cparams(n_axes):
    return pltpu.CompilerParams(
        dimension_semantics=("arbitrary",) * n_axes,
        vmem_limit_bytes=VMEM_LIMIT_BYTES,
    )


def _const_spec(shape):
    nd = len(shape)
    return pl.BlockSpec(shape, lambda *_: (0,) * nd, pipeline_mode=pl.Buffered(1))


_ANY_SPEC = pl.BlockSpec(memory_space=pl.ANY)


def _load_weights_bf16(jobs, stage_ref, sem_ref):
    n_slots, rows = stage_ref.shape[0], stage_ref.shape[1]
    ahead = n_slots - 1
    counts = [n_rows // rows for _, _, n_rows, _, _ in jobs]
    for (src, _, n_rows, dst, _), n in zip(jobs, counts):
        assert n_rows % rows == 0 and n >= ahead and src.shape[1] == dst.shape[1] <= stage_ref.shape[2]
    bases = [sum(counts[:j]) for j in range(len(jobs))]
    stream = [(j, c) for j, n in enumerate(counts) for c in range(n)]

    def copy(j, c, g):
        src, src_row0, _, _, _ = jobs[j]
        r0 = pl.multiple_of(src_row0 + c * rows, SUBLANES)
        return pltpu.make_async_copy(src.at[pl.ds(r0, rows), :],
                                     stage_ref.at[g % n_slots, :, pl.ds(0, src.shape[1])], sem_ref.at[g % n_slots])

    def finish(j, c, g):
        _, _, _, dst, dst_row0 = jobs[j]
        copy(j, c, g).wait()
        r0 = pl.multiple_of(dst_row0 + c * rows, BF16_ROWS)
        dst[pl.ds(r0, rows), :] = stage_ref[g % n_slots, :, 0:dst.shape[1]].astype(BF16)

    for g in range(ahead):
        copy(*stream[g], g).start()
    for j, n in enumerate(counts):
        def body(c, carry, j=j):
            copy(j, c + ahead, bases[j] + c + ahead).start()
            finish(j, c, bases[j] + c)
            return carry

        lax.fori_loop(0, n - ahead, body, 0)
        for c in range(n - ahead, n):
            g = bases[j] + c
            if g + ahead < len(stream):
                copy(*stream[g + ahead], g + ahead).start()
            finish(j, c, g)


def _rms(x, g):
    ms = jnp.mean(x * x, axis=-1, keepdims=True)
    return x * lax.rsqrt(ms + EPS) * g


def _dot(a, b):
    return jnp.dot(a, b, preferred_element_type=F32)


def _dot_nt(a, b):
    return lax.dot_general(a, b, (((1,), (1,)), ((), ())), preferred_element_type=F32)


def _swiglu_half_step(x, n_ref, wg_ref, wu_ref, wd_ref, a_ref):
    h = _rms(x, n_ref[...]).astype(BF16)
    for c in range(wg_ref.shape[1] // FFN_CHUNK):
        sl = slice(c * FFN_CHUNK, (c + 1) * FFN_CHUNK)
        g = _dot(h, wg_ref[:, sl])
        u = _dot(h, wu_ref[:, sl])
        a_ref[:, sl] = (g * jax.nn.sigmoid(g) * u).astype(BF16)
    return x + 0.5 * _dot(a_ref[...], wd_ref[...])


_P_Q, _P_K, _P_V, _P_R = 0, GLA_KEY_W, 2 * GLA_KEY_W, 2 * GLA_KEY_W + GLA_VAL_W
_P_BG = _P_R + GLA_VAL_W
_P_CG = _P_BG + CONV_W
_P_XV = _P_CG + CONV_W
_P_Z = _P_XV + CONV_W
_P_END = _P_Z + LANES


def _pre_body(x_ref, fn_ref, wg_hbm, wu_hbm, wd_hbm, mn_ref, wint_hbm,
              x1_ref, q_ref, k_ref, v_ref, r_ref, bg_ref, u_ref, z_ref,
              a_ref, wg_ref, wu_ref, wd_ref, wp_ref, stage_ref, sem_ref):
    @pl.when(pl.program_id(0) == 0)
    def _():
        d, d_ff = wg_hbm.shape
        z_w = 2 * GATE_RANK
        main = _P_BG
        conv = _P_Z - _P_BG
        _load_weights_bf16(
            [(wg_hbm, 0, d, wg_ref, 0), (wu_hbm, 0, d, wu_ref, 0), (wd_hbm, 0, d_ff, wd_ref, 0),
             (wint_hbm, 0, main, wp_ref, 0), (wint_hbm, main + z_w, conv, wp_ref, _P_BG)],
            stage_ref, sem_ref)
        z_copy = pltpu.make_async_copy(wint_hbm.at[pl.ds(main, z_w), :],
                                       stage_ref.at[0, pl.ds(0, z_w), pl.ds(0, d)], sem_ref.at[0])
        z_copy.start()
        z_copy.wait()
        wp_ref[_P_Z:_P_Z + z_w, :] = stage_ref[0, 0:z_w, 0:d].astype(BF16)
        wp_ref[_P_Z + z_w:_P_END, :] = jnp.zeros((LANES - z_w, d), BF16)

    x1 = _swiglu_half_step(x_ref[...], fn_ref, wg_ref, wu_ref, wd_ref, a_ref)
    x1_ref[...] = x1
    h = _rms(x1, mn_ref[...]).astype(BF16)
    p = _dot_nt(h, wp_ref[...])
    q_ref[...] = p[:, _P_Q:_P_K] * (GLA_DK ** -0.5)
    k_ref[...] = p[:, _P_K:_P_V]
    v_ref[...] = p[:, _P_V:_P_R].astype(BF16)
    r_ref[...] = p[:, _P_R:_P_BG].astype(BF16)
    bg_ref[...] = p[:, _P_BG:_P_CG].astype(BF16)
    u_ref[...] = (p[:, _P_CG:_P_XV] * p[:, _P_XV:_P_Z]).astype(BF16)
    z_ref[...] = p[:, _P_Z:_P_END].astype(BF16)


def _pre(x, ffn_norm, wg, wu, wd, mix_norm, w_in):
    t, d = x.shape
    d_ff = wg.shape[1]
    assert t % TOKEN_TILE == 0 and d_ff % FFN_CHUNK == 0 and TOKEN_TILE % CHUNK == 0
    tile = lambda w: pl.BlockSpec((TOKEN_TILE, w), lambda i: (i, 0))
    out_w = [(d, F32), (GLA_KEY_W, F32), (GLA_KEY_W, F32), (GLA_VAL_W, BF16), (GLA_VAL_W, BF16),
             (CONV_W, BF16), (CONV_W, BF16), (LANES, BF16)]
    assert w_in.shape[1] == _P_Z + 2 * GATE_RANK
    return pl.pallas_call(
        _pre_body,
        grid=(t // TOKEN_TILE,),
        in_specs=[tile(d), _const_spec((1, d)), _ANY_SPEC, _ANY_SPEC, _ANY_SPEC, _const_spec((1, d)), _ANY_SPEC],
        out_specs=[tile(w) for w, _ in out_w],
        out_shape=[jax.ShapeDtypeStruct((t, w), dt) for w, dt in out_w],
        scratch_shapes=[pltpu.VMEM((TOKEN_TILE, d_ff), BF16),
                        pltpu.VMEM((d, d_ff), BF16), pltpu.VMEM((d, d_ff), BF16), pltpu.VMEM((d_ff, d), BF16),
                        pltpu.VMEM((_P_END, d), BF16),
                        pltpu.VMEM((STAGE_SLOTS, STAGE_ROWS, max(d_ff, d)), F32),
                        pltpu.SemaphoreType.DMA((STAGE_SLOTS,))],
        compiler_params=_cparams(1),
        name="pre",
    )(x, ffn_norm.reshape(1, d), wg, wu, wd, mix_norm.reshape(1, d), w_in.T)


def _chunk_cumsum(x, reverse):
    n, w = x.shape
    groups = n // SUBLANES
    x3 = x.reshape(groups, SUBLANES, w)
    pos = lax.broadcasted_iota(jnp.int32, x3.shape, 1)
    step = 1
    while step < SUBLANES:
        if reverse:
            x3 = x3 + jnp.where(pos < SUBLANES - step, pltpu.roll(x3, SUBLANES - step, 1), 0.0)
        else:
            x3 = x3 + jnp.where(pos >= step, pltpu.roll(x3, step, 1), 0.0)
        step *= 2
    per = CHUNK // SUBLANES
    x4 = x3.reshape(n // CHUNK, per, SUBLANES, w)
    edge = x4[:, :, 0:1, :] if reverse else x4[:, :, SUBLANES - 1:SUBLANES, :]
    outs = [None] * per
    carry = None
    for j in (range(per - 1, -1, -1) if reverse else range(per)):
        outs[j] = x4[:, j] if carry is None else x4[:, j] + carry
        carry = edge[:, j] if carry is None else carry + edge[:, j]
    return jnp.stack(outs, axis=1).reshape(n, w)


def _gla_chunks(q_ref, k_ref, v_ref, z_ref, wgate_ref, gbias_ref, o_ref, s_ref, *, reverse):
    n_chunks = q_ref.shape[0] // CHUNK
    kw = GLA_KEY_W
    assert GLA_DK == CHUNK and 2 * CHUNK == LANES == GLA_DV and CHUNK & (CHUNK - 1) == 0
    shift = CHUNK.bit_length() - 1
    gz = _dot(z_ref[...], wgate_ref[...]) + gbias_ref[...]

    row_head = lax.broadcasted_iota(jnp.int32, (kw, kw), 0) >> shift
    lane_head = lax.broadcasted_iota(jnp.int32, (kw, kw), 1) >> shift
    same_head = row_head == lane_head
    pos_q = lax.broadcasted_iota(jnp.int32, (CHUNK, kw), 0)
    pos_k = lax.broadcasted_iota(jnp.int32, (CHUNK, kw), 1) & (CHUNK - 1)
    causal = (pos_k >= pos_q) if reverse else (pos_k <= pos_q)
    low_half = lax.broadcasted_iota(jnp.int32, (CHUNK, LANES), 1) < CHUNK
    low_feat = lax.broadcasted_iota(jnp.int32, (kw, LANES), 1) < CHUNK

    order = range(n_chunks - 1, -1, -1) if reverse else range(n_chunks)
    for c in order:
        rows = slice(c * CHUNK, (c + 1) * CHUNK)
        g = gz[rows, :]
        log_gate = jnp.minimum(g, 0.0) - jnp.log(1.0 + jnp.exp(-jnp.abs(g)))
        bc = _chunk_cumsum(log_gate * (1.0 / GATE_TAU), reverse)
        q = q_ref[rows, :]
        k = k_ref[rows, :]
        v = v_ref[rows, :]
        b_last = bc[0:1, :] if reverse else bc[CHUNK - 1:CHUNK, :]
        q_in = q * jnp.exp(bc)
        k_in = k * jnp.exp(-bc)
        k_st = k * jnp.exp(b_last - bc)
        k_blk = jnp.where(same_head, jnp.concatenate([k_in] * GLA_HEADS, axis=0), 0.0).astype(BF16)
        att = _dot_nt(q_in.astype(BF16), k_blk)
        att = jnp.where(causal, att, 0.0)
        decay_rows = jnp.broadcast_to(jnp.exp(b_last), (SUBLANES, kw))
        pad = jnp.zeros((LANES - CHUNK - SUBLANES, kw), F32)
        xt = jnp.concatenate([k_st, decay_rows, pad], axis=0).T
        kst_t = jnp.where(low_feat, xt, 0.0).astype(BF16)
        outs = []
        for pair in range(GLA_HEADS // 2):
            lanes = slice(pair * LANES, (pair + 1) * LANES)
            a_col, q_col = att[:, lanes], q_in[:, lanes]
            a_swapped = pltpu.roll(a_col, CHUNK, 1)
            q_swapped = pltpu.roll(q_col, CHUNK, 1)
            top = jnp.concatenate([jnp.where(low_half, a_col, q_swapped),
                                   jnp.where(low_half, a_swapped, q_col)], axis=1).astype(BF16)
            heads = (2 * pair, 2 * pair + 1)
            bottom = jnp.concatenate([kst_t[h * GLA_DK:(h + 1) * GLA_DK, :] for h in heads], axis=1)
            s_in = [s_ref[h] for h in heads]
            zero = jnp.zeros((LANES, GLA_DV), BF16)
            vs = [jnp.concatenate([v[:, h * GLA_DV:(h + 1) * GLA_DV], s.astype(BF16)], axis=0)
                  for h, s in zip(heads, s_in)]
            rhs = jnp.concatenate([jnp.concatenate([vs[0], zero], axis=1),
                                   jnp.concatenate([zero, vs[1]], axis=1)], axis=0)
            res = _dot(jnp.concatenate([top, bottom], axis=0), rhs)
            outs.append(res[:CHUNK, :])
            for i, (h, s) in enumerate(zip(heads, s_in)):
                decay = xt[h * GLA_DK:(h + 1) * GLA_DK, CHUNK:CHUNK + 1]
                s_ref[h] = decay * s + res[CHUNK:, i * GLA_DV:(i + 1) * GLA_DV]
        o_ref[rows, :] = jnp.concatenate(outs, axis=1).astype(o_ref.dtype)
        yield


def _gla_body(qf_ref, kf_ref, vf_ref, zf_ref, qb_ref, kb_ref, vb_ref, zb_ref, wgf_ref, wgb_ref, gbf_ref, gbb_ref,
              of_ref, ob_ref, sf_ref, sb_ref):
    @pl.when(pl.program_id(1) == 0)
    def _():
        sf_ref[...] = jnp.zeros_like(sf_ref)
        sb_ref[...] = jnp.zeros_like(sb_ref)

    fwd = _gla_chunks(qf_ref, kf_ref, vf_ref, zf_ref, wgf_ref, gbf_ref, of_ref, sf_ref, reverse=False)
    bwd = _gla_chunks(qb_ref, kb_ref, vb_ref, zb_ref, wgb_ref, gbb_ref, ob_ref, sb_ref, reverse=True)
    for _ in zip(fwd, bwd):
        pass


def _gla(q, k, v, z, wgate_f, wgate_b, gbias_f, gbias_b, batch, seq):
    nb = seq // GLA_BLOCK
    fwd = lambda w: pl.BlockSpec((GLA_BLOCK, w), lambda b, j: (b * nb + j, 0))
    bwd = lambda w: pl.BlockSpec((GLA_BLOCK, w), lambda b, j: (b * nb + nb - 1 - j, 0))
    t = batch * seq
    return pl.pallas_call(
        _gla_body,
        grid=(batch, nb),
        in_specs=[fwd(GLA_KEY_W), fwd(GLA_KEY_W), fwd(GLA_VAL_W), fwd(LANES),
                  bwd(GLA_KEY_W), bwd(GLA_KEY_W), bwd(GLA_VAL_W), bwd(LANES),
                  _const_spec(wgate_f.shape), _const_spec(wgate_b.shape),
                  _const_spec(gbias_f.shape), _const_spec(gbias_b.shape)],
        out_specs=[fwd(GLA_VAL_W), bwd(GLA_VAL_W)],
        out_shape=[jax.ShapeDtypeStruct((t, GLA_VAL_W), BF16)] * 2,
        scratch_shapes=[pltpu.VMEM((GLA_HEADS, GLA_DK, GLA_DV), F32)] * 2,
        compiler_params=_cparams(2),
        name="gla",
    )(q, k, v, z, q, k, v, z, wgate_f, wgate_b, gbias_f, gbias_b)


def _kv_body(m_ref, n_ref, w_ref, k_ref, v_ref):
    d = m_ref.shape[-1]
    h = _rms(m_ref[...], n_ref[...]).astype(BF16)
    kv = _dot(h, w_ref[...].astype(BF16))
    k_ref[...] = kv[:, :d].astype(BF16)
    v_ref[...] = kv[:, d:].astype(BF16)


def _kv(mem, norm, w_kv):
    b, m, d = mem.shape
    k, v = pl.pallas_call(
        _kv_body,
        grid=(1,),
        in_specs=[_const_spec((b * m, d)), _const_spec((1, d)), _const_spec(w_kv.shape)],
        out_specs=[_const_spec((b * m, d))] * 2,
        out_shape=[jax.ShapeDtypeStruct((b * m, d), BF16)] * 2,
        compiler_params=_cparams(1),
        name="kv",
    )(mem.reshape(b * m, d), norm.reshape(1, d), w_kv)
    return k.reshape(b, m, d), v.reshape(b, m, d)


def _token_mix_out(x, of_ref, ob_ref, r_ref, bg_ref, u_ref, up_ref, un_ref, gn_ref, cw_ref, cb_ref, wo_ref, seq):
    tm = x.shape[0]
    i = pl.program_id(0)
    o = of_ref[...].astype(F32) + ob_ref[...].astype(F32)
    heads = []
    for h in range(GLA_HEADS):
        oh = o[:, h * GLA_DV:(h + 1) * GLA_DV]
        heads.append(oh * lax.rsqrt(jnp.mean(oh * oh, axis=-1, keepdims=True) + EPS))
    r = r_ref[...].astype(F32)
    a_out = jnp.concatenate(heads, axis=1) * gn_ref[...] * (r * jax.nn.sigmoid(r))
    u = u_ref[...].astype(F32)
    at_seq_start = (i * tm) % seq == 0
    at_seq_end = ((i + 1) * tm) % seq == 0
    halo_prev = jnp.where(at_seq_start, 0.0, up_ref[BF16_ROWS - 1:BF16_ROWS, :].astype(F32))
    halo_next = jnp.where(at_seq_end, 0.0, un_ref[0:1, :].astype(F32))
    row = lax.broadcasted_iota(jnp.int32, u.shape, 0)
    u_prev = jnp.where(row == 0, halo_prev, pltpu.roll(u, 1, 0))
    u_next = jnp.where(row == tm - 1, halo_next, pltpu.roll(u, tm - 1, 0))
    conv = cw_ref[0:1, :] * u_prev + cw_ref[1:2, :] * u + cw_ref[2:3, :] * u_next + cb_ref[...]
    c_out = bg_ref[...].astype(F32) * conv
    mixed = jnp.concatenate([a_out, c_out], axis=1).astype(BF16)
    return x + _dot(mixed, wo_ref[...])


def _cross_attention(x, n_ref, wq_ref, k_ref, v_ref, wo_ref):
    d = x.shape[-1]
    hd = d // XATTN_HEADS
    h = _rms(x, n_ref[...]).astype(BF16)
    q = _dot(h, wq_ref[...])
    outs = []
    for a in range(XATTN_HEADS):
        sl = slice(a * hd, (a + 1) * hd)
        s = _dot_nt(q[:, sl].astype(BF16), k_ref[0, :, sl]) * (hd ** -0.5)
        p = jnp.exp(s - jnp.max(s, axis=-1, keepdims=True))
        denom = jnp.sum(p, axis=-1, keepdims=True)
        outs.append(_dot(p.astype(BF16), v_ref[0, :, sl]) / denom)
    o = jnp.concatenate(outs, axis=1).astype(BF16)
    return x + _dot(o, wo_ref[...])


def _post_body(*refs, seq, final):
    (x_ref, of_ref, ob_ref, r_ref, bg_ref, u_ref, up_ref, un_ref, gn_ref, cw_ref, cb_ref, wo_hbm,
     xn_ref, wq_hbm, k_ref, v_ref, wxo_hbm, fn_ref, wg_hbm, wu_hbm, wd_hbm) = refs[:21]
    refs = refs[21:]
    if final:
        last_ref, refs = refs[0], refs[1:]
    o_ref, a_ref, wo_ref, wq_ref, wxo_ref, wg_ref, wu_ref, wd_ref, stage_ref, sem_ref = refs

    @pl.when(pl.program_id(0) == 0)
    def _():
        _load_weights_bf16(
            [(src, 0, src.shape[0], dst, 0) for src, dst in
             ((wo_hbm, wo_ref), (wq_hbm, wq_ref), (wxo_hbm, wxo_ref), (wg_hbm, wg_ref), (wu_hbm, wu_ref),
              (wd_hbm, wd_ref))],
            stage_ref, sem_ref)

    x2 = _token_mix_out(x_ref[...], of_ref, ob_ref, r_ref, bg_ref, u_ref, up_ref, un_ref, gn_ref, cw_ref, cb_ref,
                        wo_ref, seq)
    x3 = _cross_attention(x2, xn_ref, wq_ref, k_ref, v_ref, wxo_ref)
    y = _swiglu_half_step(x3, fn_ref, wg_ref, wu_ref, wd_ref, a_ref)
    if final:
        y = _rms(y, last_ref[...])
    o_ref[...] = y


def _post(x, o_f, o_b, r, bg, u, gla_norm, conv_w, conv_b, w_out, xattn_norm, w_q, mk, mv, w_xo,
          ffn_norm, wg, wu, wd, seq, final_norm=None):
    t, d = x.shape
    tm = TOKEN_TILE
    d_ff = wg.shape[1]
    m = mk.shape[1]
    assert seq % tm == 0 and tm % BF16_ROWS == 0 and d_ff % FFN_CHUNK == 0
    final = final_norm is not None
    tile = lambda w: pl.BlockSpec((tm, w), lambda i: (i, 0))
    hb = tm // BF16_ROWS
    n_halo = t // BF16_ROWS
    halo_prev = pl.BlockSpec((BF16_ROWS, CONV_W), lambda i: (jnp.maximum(i * hb - 1, 0), 0))
    halo_next = pl.BlockSpec((BF16_ROWS, CONV_W), lambda i: (jnp.minimum((i + 1) * hb, n_halo - 1), 0))
    kv_blk = pl.BlockSpec((1, m, d), lambda i: (i // (seq // tm), 0, 0))
    in_specs = [tile(d), tile(GLA_VAL_W), tile(GLA_VAL_W), tile(GLA_VAL_W), tile(CONV_W), tile(CONV_W),
                halo_prev, halo_next, _const_spec((1, GLA_VAL_W)), _const_spec(conv_w.shape),
                _const_spec((1, CONV_W)), _ANY_SPEC,
                _const_spec((1, d)), _ANY_SPEC, kv_blk, kv_blk, _ANY_SPEC,
                _const_spec((1, d)), _ANY_SPEC, _ANY_SPEC, _ANY_SPEC]
    args = [x, o_f, o_b, r, bg, u, u, u, gla_norm.reshape(1, -1), conv_w, conv_b.reshape(1, -1), w_out,
            xattn_norm.reshape(1, d), w_q, mk, mv, w_xo, ffn_norm.reshape(1, d), wg, wu, wd]
    if final:
        in_specs.append(_const_spec((1, d)))
        args.append(final_norm.reshape(1, d))
    return pl.pallas_call(
        functools.partial(_post_body, seq=seq, final=final),
        grid=(t // tm,),
        in_specs=in_specs,
        out_specs=tile(d),
        out_shape=jax.ShapeDtypeStruct((t, d), F32),
        scratch_shapes=[pltpu.VMEM((tm, d_ff), BF16),
                        pltpu.VMEM(w_out.shape, BF16), pltpu.VMEM(w_q.shape, BF16), pltpu.VMEM(w_xo.shape, BF16),
                        pltpu.VMEM((d, d_ff), BF16), pltpu.VMEM((d, d_ff), BF16), pltpu.VMEM((d_ff, d), BF16),
                        pltpu.VMEM((STAGE_SLOTS, STAGE_ROWS, max(d_ff, d)), F32),
                        pltpu.SemaphoreType.DMA((STAGE_SLOTS,))],
        compiler_params=_cparams(1),
        name="post_final" if final else "post",
    )(*args)


def _pack_gate(gate_w, gate_b, z_row0):
    w = jnp.pad(gate_w, ((z_row0, LANES - z_row0 - GATE_RANK), (0, 0)))
    return w.astype(BF16), gate_b.reshape(1, -1).astype(F32)


def kernel(x, mem, ffn1_norm, ffn1_w_gate, ffn1_w_up, ffn1_w_down, mix_norm, w_in, gate_fwd_w, gate_fwd_b,
           gate_bwd_w, gate_bwd_b, gla_norm, conv_w, conv_b, w_out, xattn_norm, mem_norm, xattn_w_q,
           xattn_w_kv, xattn_w_o, ffn2_norm, ffn2_w_gate, ffn2_w_up, ffn2_w_down, final_norm):
    batch, seq, d = x.shape
    depth = w_in.shape[0]
    assert seq % GLA_BLOCK == 0 and GLA_BLOCK % CHUNK == 0 and seq % TOKEN_TILE == 0
    h = x.reshape(batch * seq, d)
    for l in range(depth):
        h, q, k, v, r, bg, u, z = _pre(
            h, ffn1_norm[l], ffn1_w_gate[l], ffn1_w_up[l], ffn1_w_down[l], mix_norm[l], w_in[l])
        wgate_f, gbias_f = _pack_gate(gate_fwd_w[l], gate_fwd_b[l], 0)
        wgate_b, gbias_b = _pack_gate(gate_bwd_w[l], gate_bwd_b[l], GATE_RANK)
        o_f, o_b = _gla(q, k, v, z, wgate_f, wgate_b, gbias_f, gbias_b, batch, seq)
        mk, mv = _kv(mem, mem_norm[l], xattn_w_kv[l])
        h = _post(h, o_f, o_b, r, bg, u, gla_norm[l], conv_w[l], conv_b[l], w_out[l],
                  xattn_norm[l], xattn_w_q[l], mk, mv, xattn_w_o[l],
                  ffn2_norm[l], ffn2_w_gate[l], ffn2_w_up[l], ffn2_w_down[l], seq,
                  final_norm=final_norm if l == depth - 1 else None)
    return h.reshape(batch, seq, d)
```

```python
import functools

import jax
import jax.numpy as jnp
from jax import lax
from jax.experimental import pallas as pl
from jax.experimental.pallas import tpu as pltpu

F32 = jnp.float32
BF16 = jnp.bfloat16

XATTN_HEADS = 4
GLA_HEADS = 4
GLA_DK = 64
GLA_DV = 128
GLA_KEY_W = GLA_HEADS * GLA_DK
GLA_VAL_W = GLA_HEADS * GLA_DV
GATE_RANK = 16
GATE_TAU = 16.0
CHUNK = 64
CONV_W = 512
EPS = 1e-6

LANES = 128
SUBLANES = 8
BF16_ROWS = 16
TOKEN_TILE = 512
TILE_SPLIT = 2
MIXER_PIECE_ROWS = 32
GLA_BLOCK = 512
FFN_CHUNK = 256
STAGE_ROWS = 128
STAGE_SLOTS = 4
VMEM_LIMIT_BYTES = 58 * 1024 * 1024


def _cparams(n_axes):
    return pltpu.CompilerParams(
        dimension_semantics=("arbitrary",) * n_axes,
        vmem_limit_bytes=VMEM_LIMIT_BYTES,
    )


def _const_spec(shape):
    nd = len(shape)
    return pl.BlockSpec(shape, lambda *_: (0,) * nd, pipeline_mode=pl.Buffered(1))


_ANY_SPEC = pl.BlockSpec(memory_space=pl.ANY)


def _load_weights_bf16(jobs, stage_ref, sem_ref):
    n_slots, rows = stage_ref.shape[0], stage_ref.shape[1]
    ahead = n_slots - 1
    counts = [n_rows // rows for _, _, n_rows, _, _ in jobs]
    for (src, _, n_rows, dst, _), n in zip(jobs, counts):
        assert n_rows % rows == 0 and n >= ahead and src.shape[1] == dst.shape[1] <= stage_ref.shape[2]
    bases = [sum(counts[:j]) for j in range(len(jobs))]
    stream = [(j, c) for j, n in enumerate(counts) for c in range(n)]

    def copy(j, c, g):
        src, src_row0, _, _, _ = jobs[j]
        r0 = pl.multiple_of(src_row0 + c * rows, SUBLANES)
        return pltpu.make_async_copy(src.at[pl.ds(r0, rows), :],
                                     stage_ref.at[g % n_slots, :, pl.ds(0, src.shape[1])], sem_ref.at[g % n_slots])

    def finish(j, c, g):
        _, _, _, dst, dst_row0 = jobs[j]
        copy(j, c, g).wait()
        r0 = pl.multiple_of(dst_row0 + c * rows, BF16_ROWS)
        dst[pl.ds(r0, rows), :] = stage_ref[g % n_slots, :, 0:dst.shape[1]].astype(BF16)

    for g in range(ahead):
        copy(*stream[g], g).start()
    for j, n in enumerate(counts):
        def body(c, carry, j=j):
            copy(j, c + ahead, bases[j] + c + ahead).start()
            finish(j, c, bases[j] + c)
            return carry

        lax.fori_loop(0, n - ahead, body, 0)
        for c in range(n - ahead, n):
            g = bases[j] + c
            if g + ahead < len(stream):
                copy(*stream[g + ahead], g + ahead).start()
            finish(j, c, g)


def _rms(x, g):
    ms = jnp.mean(x * x, axis=-1, keepdims=True)
    return x * lax.rsqrt(ms + EPS) * g


def _dot(a, b):
    return jnp.dot(a, b, preferred_element_type=F32)


def _dot_nt(a, b):
    return lax.dot_general(a, b, (((1,), (1,)), ((), ())), preferred_element_type=F32)


def _row_slices(tile_rows):
    assert tile_rows % (TILE_SPLIT * BF16_ROWS) == 0
    part = tile_rows // TILE_SPLIT
    return [slice(i * part, (i + 1) * part) for i in range(TILE_SPLIT)]


def _run_staggered(stage_gens, lead=1):
    active = list(stage_gens)
    for _ in range(lead):
        next(active[0])
    while active:
        for gen in list(active):
            try:
                next(gen)
            except StopIteration:
                active.remove(gen)


def _swiglu_stages(x, rows, n_ref, wg_ref, wu_ref, wd_ref, a_ref, out, filler=None):
    h = _rms(x, n_ref[...]).astype(BF16)
    yield
    for c in range(wg_ref.shape[1] // FFN_CHUNK):
        sl = slice(c * FFN_CHUNK, (c + 1) * FFN_CHUNK)
        g = _dot(h, wg_ref[:, sl])
        u = _dot(h, wu_ref[:, sl])
        a_ref[rows, sl] = (g * jax.nn.sigmoid(g) * u).astype(BF16)
        if filler is not None:
            next(filler, None)
    yield
    out["y"] = x + 0.5 * _dot(a_ref[rows, :], wd_ref[...])


_P_Q, _P_K, _P_V, _P_R = 0, GLA_KEY_W, 2 * GLA_KEY_W, 2 * GLA_KEY_W + GLA_VAL_W
_P_BG = _P_R + GLA_VAL_W
_P_CG = _P_BG + CONV_W
_P_XV = _P_CG + CONV_W
_P_Z = _P_XV + CONV_W
_P_END = _P_Z + LANES


def _pre_body(x_ref, fn_ref, wg_hbm, wu_hbm, wd_hbm, mn_ref, wint_hbm,
              x1_ref, q_ref, k_ref, v_ref, r_ref, bg_ref, u_ref, z_ref,
              a_ref, wg_ref, wu_ref, wd_ref, wp_ref, stage_ref, sem_ref):
    @pl.when(pl.program_id(0) == 0)
    def _():
        d, d_ff = wg_hbm.shape
        z_w = 2 * GATE_RANK
        main = _P_BG
        conv = _P_Z - _P_BG
        _load_weights_bf16(
            [(wg_hbm, 0, d, wg_ref, 0), (wu_hbm, 0, d, wu_ref, 0), (wd_hbm, 0, d_ff, wd_ref, 0),
             (wint_hbm, 0, main, wp_ref, 0), (wint_hbm, main + z_w, conv, wp_ref, _P_BG)],
            stage_ref, sem_ref)
        z_copy = pltpu.make_async_copy(wint_hbm.at[pl.ds(main, z_w), :],
                                       stage_ref.at[0, pl.ds(0, z_w), pl.ds(0, d)], sem_ref.at[0])
        z_copy.start()
        z_copy.wait()
        wp_ref[_P_Z:_P_Z + z_w, :] = stage_ref[0, 0:z_w, 0:d].astype(BF16)
        wp_ref[_P_Z + z_w:_P_END, :] = jnp.zeros((LANES - z_w, d), BF16)

    def stages(rows):
        ffn = {}
        yield from _swiglu_stages(x_ref[rows, :], rows, fn_ref, wg_ref, wu_ref, wd_ref, a_ref, ffn)
        x1 = ffn["y"]
        x1_ref[rows, :] = x1
        yield
        h = _rms(x1, mn_ref[...]).astype(BF16)
        yield
        p = _dot_nt(h, wp_ref[...])
        q_ref[rows, :] = p[:, _P_Q:_P_K] * (GLA_DK ** -0.5)
        k_ref[rows, :] = p[:, _P_K:_P_V]
        v_ref[rows, :] = p[:, _P_V:_P_R].astype(BF16)
        r_ref[rows, :] = p[:, _P_R:_P_BG].astype(BF16)
        bg_ref[rows, :] = p[:, _P_BG:_P_CG].astype(BF16)
        u_ref[rows, :] = (p[:, _P_CG:_P_XV] * p[:, _P_XV:_P_Z]).astype(BF16)
        z_ref[rows, :] = p[:, _P_Z:_P_END].astype(BF16)

    _run_staggered(stages(rows) for rows in _row_slices(x_ref.shape[0]))


def _pre(x, ffn_norm, wg, wu, wd, mix_norm, w_in):
    t, d = x.shape
    d_ff = wg.shape[1]
    assert t % TOKEN_TILE == 0 and d_ff % FFN_CHUNK == 0 and TOKEN_TILE % CHUNK == 0
    tile = lambda w: pl.BlockSpec((TOKEN_TILE, w), lambda i: (i, 0))
    out_w = [(d, F32), (GLA_KEY_W, F32), (GLA_KEY_W, F32), (GLA_VAL_W, BF16), (GLA_VAL_W, BF16),
             (CONV_W, BF16), (CONV_W, BF16), (LANES, BF16)]
    assert w_in.shape[1] == _P_Z + 2 * GATE_RANK
    return pl.pallas_call(
        _pre_body,
        grid=(t // TOKEN_TILE,),
        in_specs=[tile(d), _const_spec((1, d)), _ANY_SPEC, _ANY_SPEC, _ANY_SPEC, _const_spec((1, d)), _ANY_SPEC],
        out_specs=[tile(w) for w, _ in out_w],
        out_shape=[jax.ShapeDtypeStruct((t, w), dt) for w, dt in out_w],
        scratch_shapes=[pltpu.VMEM((TOKEN_TILE, d_ff), BF16),
                        pltpu.VMEM((d, d_ff), BF16), pltpu.VMEM((d, d_ff), BF16), pltpu.VMEM((d_ff, d), BF16),
                        pltpu.VMEM((_P_END, d), BF16),
                        pltpu.VMEM((STAGE_SLOTS, STAGE_ROWS, max(d_ff, d)), F32),
                        pltpu.SemaphoreType.DMA((STAGE_SLOTS,))],
        compiler_params=_cparams(1),
        name="pre",
    )(x, ffn_norm.reshape(1, d), wg, wu, wd, mix_norm.reshape(1, d), w_in.T)


def _chunk_cumsum(x, reverse):
    n, w = x.shape
    groups = n // SUBLANES
    x3 = x.reshape(groups, SUBLANES, w)
    pos = lax.broadcasted_iota(jnp.int32, x3.shape, 1)
    step = 1
    while step < SUBLANES:
        if reverse:
            x3 = x3 + jnp.where(pos < SUBLANES - step, pltpu.roll(x3, SUBLANES - step, 1), 0.0)
        else:
            x3 = x3 + jnp.where(pos >= step, pltpu.roll(x3, step, 1), 0.0)
        step *= 2
    per = CHUNK // SUBLANES
    x4 = x3.reshape(n // CHUNK, per, SUBLANES, w)
    edge = x4[:, :, 0:1, :] if reverse else x4[:, :, SUBLANES - 1:SUBLANES, :]
    outs = [None] * per
    carry = None
    for j in (range(per - 1, -1, -1) if reverse else range(per)):
        outs[j] = x4[:, j] if carry is None else x4[:, j] + carry
        carry = edge[:, j] if carry is None else carry + edge[:, j]
    return jnp.stack(outs, axis=1).reshape(n, w)


def _gla_chunks(q_ref, k_ref, v_ref, z_ref, wgate_ref, gbias_ref, o_ref, s_ref, *, reverse):
    n_chunks = q_ref.shape[0] // CHUNK
    kw = GLA_KEY_W
    assert GLA_DK == CHUNK and 2 * CHUNK == LANES == GLA_DV and CHUNK & (CHUNK - 1) == 0
    shift = CHUNK.bit_length() - 1
    gz = _dot(z_ref[...], wgate_ref[...]) + gbias_ref[...]

    row_head = lax.broadcasted_iota(jnp.int32, (kw, kw), 0) >> shift
    lane_head = lax.broadcasted_iota(jnp.int32, (kw, kw), 1) >> shift
    same_head = row_head == lane_head
    pos_q = lax.broadcasted_iota(jnp.int32, (CHUNK, kw), 0)
    pos_k = lax.broadcasted_iota(jnp.int32, (CHUNK, kw), 1) & (CHUNK - 1)
    causal = (pos_k >= pos_q) if reverse else (pos_k <= pos_q)
    low_half = lax.broadcasted_iota(jnp.int32, (CHUNK, LANES), 1) < CHUNK
    low_feat = lax.broadcasted_iota(jnp.int32, (kw, LANES), 1) < CHUNK

    order = range(n_chunks - 1, -1, -1) if reverse else range(n_chunks)
    for c in order:
        rows = slice(c * CHUNK, (c + 1) * CHUNK)
        g = gz[rows, :]
        log_gate = jnp.minimum(g, 0.0) - jnp.log(1.0 + jnp.exp(-jnp.abs(g)))
        bc = _chunk_cumsum(log_gate * (1.0 / GATE_TAU), reverse)
        q = q_ref[rows, :]
        k = k_ref[rows, :]
        v = v_ref[rows, :]
        b_last = bc[0:1, :] if reverse else bc[CHUNK - 1:CHUNK, :]
        q_in = q * jnp.exp(bc)
        k_in = k * jnp.exp(-bc)
        k_st = k * jnp.exp(b_last - bc)
        k_blk = jnp.where(same_head, jnp.concatenate([k_in] * GLA_HEADS, axis=0), 0.0).astype(BF16)
        att = _dot_nt(q_in.astype(BF16), k_blk)
        att = jnp.where(causal, att, 0.0)
        decay_rows = jnp.broadcast_to(jnp.exp(b_last), (SUBLANES, kw))
        pad = jnp.zeros((LANES - CHUNK - SUBLANES, kw), F32)
        xt = jnp.concatenate([k_st, decay_rows, pad], axis=0).T
        kst_t = jnp.where(low_feat, xt, 0.0).astype(BF16)
        outs = []
        for pair in range(GLA_HEADS // 2):
            lanes = slice(pair * LANES, (pair + 1) * LANES)
            a_col, q_col = att[:, lanes], q_in[:, lanes]
            a_swapped = pltpu.roll(a_col, CHUNK, 1)
            q_swapped = pltpu.roll(q_col, CHUNK, 1)
            top = jnp.concatenate([jnp.where(low_half, a_col, q_swapped),
                                   jnp.where(low_half, a_swapped, q_col)], axis=1).astype(BF16)
            heads = (2 * pair, 2 * pair + 1)
            bottom = jnp.concatenate([kst_t[h * GLA_DK:(h + 1) * GLA_DK, :] for h in heads], axis=1)
            s_in = [s_ref[h] for h in heads]
            zero = jnp.zeros((LANES, GLA_DV), BF16)
            vs = [jnp.concatenate([v[:, h * GLA_DV:(h + 1) * GLA_DV], s.astype(BF16)], axis=0)
                  for h, s in zip(heads, s_in)]
            rhs = jnp.concatenate([jnp.concatenate([vs[0], zero], axis=1),
                                   jnp.concatenate([zero, vs[1]], axis=1)], axis=0)
            res = _dot(jnp.concatenate([top, bottom], axis=0), rhs)
            outs.append(res[:CHUNK, :])
            for i, (h, s) in enumerate(zip(heads, s_in)):
                decay = xt[h * GLA_DK:(h + 1) * GLA_DK, CHUNK:CHUNK + 1]
                s_ref[h] = decay * s + res[CHUNK:, i * GLA_DV:(i + 1) * GLA_DV]
        o_ref[rows, :] = jnp.concatenate(outs, axis=1).astype(o_ref.dtype)
        yield


def _gla_body(qf_ref, kf_ref, vf_ref, zf_ref, qb_ref, kb_ref, vb_ref, zb_ref, wgf_ref, wgb_ref, gbf_ref, gbb_ref,
              of_ref, ob_ref, sf_ref, sb_ref):
    @pl.when(pl.program_id(1) == 0)
    def _():
        sf_ref[...] = jnp.zeros_like(sf_ref)
        sb_ref[...] = jnp.zeros_like(sb_ref)

    fwd = _gla_chunks(qf_ref, kf_ref, vf_ref, zf_ref, wgf_ref, gbf_ref, of_ref, sf_ref, reverse=False)
    bwd = _gla_chunks(qb_ref, kb_ref, vb_ref, zb_ref, wgb_ref, gbb_ref, ob_ref, sb_ref, reverse=True)
    for _ in zip(fwd, bwd):
        pass


def _gla(q, k, v, z, wgate_f, wgate_b, gbias_f, gbias_b, batch, seq):
    nb = seq // GLA_BLOCK
    fwd = lambda w: pl.BlockSpec((GLA_BLOCK, w), lambda b, j: (b * nb + j, 0))
    bwd = lambda w: pl.BlockSpec((GLA_BLOCK, w), lambda b, j: (b * nb + nb - 1 - j, 0))
    t = batch * seq
    return pl.pallas_call(
        _gla_body,
        grid=(batch, nb),
        in_specs=[fwd(GLA_KEY_W), fwd(GLA_KEY_W), fwd(GLA_VAL_W), fwd(LANES),
                  bwd(GLA_KEY_W), bwd(GLA_KEY_W), bwd(GLA_VAL_W), bwd(LANES),
                  _const_spec(wgate_f.shape), _const_spec(wgate_b.shape),
                  _const_spec(gbias_f.shape), _const_spec(gbias_b.shape)],
        out_specs=[fwd(GLA_VAL_W), bwd(GLA_VAL_W)],
        out_shape=[jax.ShapeDtypeStruct((t, GLA_VAL_W), BF16)] * 2,
        scratch_shapes=[pltpu.VMEM((GLA_HEADS, GLA_DK, GLA_DV), F32)] * 2,
        compiler_params=_cparams(2),
        name="gla",
    )(q, k, v, z, q, k, v, z, wgate_f, wgate_b, gbias_f, gbias_b)


def _kv_body(m_ref, n_ref, w_ref, k_ref, v_ref):
    d = m_ref.shape[-1]
    h = _rms(m_ref[...], n_ref[...]).astype(BF16)
    kv = _dot(h, w_ref[...].astype(BF16))
    k_ref[...] = kv[:, :d].astype(BF16)
    v_ref[...] = kv[:, d:].astype(BF16)


def _kv(mem, norm, w_kv):
    b, m, d = mem.shape
    k, v = pl.pallas_call(
        _kv_body,
        grid=(1,),
        in_specs=[_const_spec((b * m, d)), _const_spec((1, d)), _const_spec(w_kv.shape)],
        out_specs=[_const_spec((b * m, d))] * 2,
        out_shape=[jax.ShapeDtypeStruct((b * m, d), BF16)] * 2,
        compiler_params=_cparams(1),
        name="kv",
    )(mem.reshape(b * m, d), norm.reshape(1, d), w_kv)
    return k.reshape(b, m, d), v.reshape(b, m, d)


def _mixed_heads(rows, tile, of_ref, ob_ref, r_ref, bg_ref, u_ref, edge_before, edge_after, gn_ref, cw_ref, cb_ref,
                 seq):
    tm = u_ref.shape[0]
    n = rows.stop - rows.start
    o = of_ref[rows, :].astype(F32) + ob_ref[rows, :].astype(F32)
    heads = []
    for h in range(GLA_HEADS):
        oh = o[:, h * GLA_DV:(h + 1) * GLA_DV]
        heads.append(oh * lax.rsqrt(jnp.mean(oh * oh, axis=-1, keepdims=True) + EPS))
    r = r_ref[rows, :].astype(F32)
    a_out = jnp.concatenate(heads, axis=1) * gn_ref[...] * (r * jax.nn.sigmoid(r))
    u = u_ref[rows, :].astype(F32)
    if rows.start == 0:
        at_seq_start = (tile * tm) % seq == 0
        before = jnp.where(at_seq_start, 0.0, edge_before.astype(F32))
    else:
        before = u_ref[rows.start - 1:rows.start, :].astype(F32)
    if rows.stop == tm:
        at_seq_end = ((tile + 1) * tm) % seq == 0
        after = jnp.where(at_seq_end, 0.0, edge_after.astype(F32))
    else:
        after = u_ref[rows.stop:rows.stop + 1, :].astype(F32)
    row = lax.broadcasted_iota(jnp.int32, u.shape, 0)
    u_prev = jnp.where(row == 0, before, pltpu.roll(u, 1, 0))
    u_next = jnp.where(row == n - 1, after, pltpu.roll(u, n - 1, 0))
    conv = cw_ref[0:1, :] * u_prev + cw_ref[1:2, :] * u + cw_ref[2:3, :] * u_next + cb_ref[...]
    c_out = bg_ref[rows, :].astype(F32) * conv
    return jnp.concatenate([a_out, c_out], axis=1).astype(BF16)


def _cross_attention_stages(x, n_ref, wq_ref, k_ref, v_ref, wo_ref, out):
    d = x.shape[-1]
    hd = d // XATTN_HEADS
    h = _rms(x, n_ref[...]).astype(BF16)
    yield
    q = _dot(h, wq_ref[...])
    yield
    outs = []
    for a in range(XATTN_HEADS):
        sl = slice(a * hd, (a + 1) * hd)
        s = _dot_nt(q[:, sl].astype(BF16), k_ref[0, :, sl]) * (hd ** -0.5)
        p = jnp.exp(s - jnp.max(s, axis=-1, keepdims=True))
        denom = jnp.sum(p, axis=-1, keepdims=True)
        outs.append(_dot(p.astype(BF16), v_ref[0, :, sl]) / denom)
    o = jnp.concatenate(outs, axis=1).astype(BF16)
    yield
    out["y"] = x + _dot(o, wo_ref[...])


def _post_body(*refs, seq, final):
    first = refs[:5]
    (x_ref, of_ref, ob_ref, r_ref, bg_ref, u_ref, up_ref, un_ref, gn_ref, cw_ref, cb_ref, wo_hbm,
     xn_ref, wq_hbm, k_ref, v_ref, wxo_hbm, fn_ref, wg_hbm, wu_hbm, wd_hbm) = refs[5:26]
    refs = refs[26:]
    if final:
        last_ref, refs = refs[0], refs[1:]
    o_ref, a_ref, mixed_ref, wo_ref, wq_ref, wxo_ref, wg_ref, wu_ref, wd_ref, stage_ref, sem_ref = refs
    tm = x_ref.shape[0]
    i = pl.program_id(0)
    norm_conv = (gn_ref, cw_ref, cb_ref, seq)

    @pl.when(i == 0)
    def _():
        _load_weights_bf16(
            [(src, 0, src.shape[0], dst, 0) for src, dst in
             ((wo_hbm, wo_ref), (wq_hbm, wq_ref), (wxo_hbm, wxo_ref), (wg_hbm, wg_ref), (wu_hbm, wu_ref),
              (wd_hbm, wd_ref))],
            stage_ref, sem_ref)
        edge_after = u_ref[0:1, :]
        for rows in _row_slices(tm):
            mixed_ref[rows, :] = _mixed_heads(rows, 0, *first, edge_after, edge_after, *norm_conv)

    def next_mixed_pieces():
        nxt = jnp.minimum(i + 1, pl.num_programs(0) - 1)
        for r0 in range(0, tm, MIXER_PIECE_ROWS):
            rows = slice(r0, r0 + MIXER_PIECE_ROWS)
            mixed_ref[rows, :] = _mixed_heads(rows, nxt, of_ref, ob_ref, r_ref, bg_ref, u_ref,
                                              up_ref[BF16_ROWS - 1:BF16_ROWS, :], un_ref[0:1, :], *norm_conv)
            yield

    filler = next_mixed_pieces()

    def stages(rows):
        x2 = x_ref[rows, :] + _dot(mixed_ref[rows, :], wo_ref[...])
        yield
        box = {}
        yield from _cross_attention_stages(x2, xn_ref, wq_ref, k_ref, v_ref, wxo_ref, box)
        yield
        x3 = box["y"]
        yield from _swiglu_stages(x3, rows, fn_ref, wg_ref, wu_ref, wd_ref, a_ref, box, filler)
        y = box["y"]
        if final:
            yield
            y = _rms(y, last_ref[...])
        o_ref[rows, :] = y

    _run_staggered(stages(rows) for rows in _row_slices(tm))
    for _ in filler:
        pass


def _post(x, o_f, o_b, r, bg, u, gla_norm, conv_w, conv_b, w_out, xattn_norm, w_q, mk, mv, w_xo,
          ffn_norm, wg, wu, wd, seq, final_norm=None):
    t, d = x.shape
    tm = TOKEN_TILE
    d_ff = wg.shape[1]
    m = mk.shape[1]
    assert seq % tm == 0 and tm % BF16_ROWS == 0 and d_ff % FFN_CHUNK == 0
    final = final_norm is not None
    n_tiles = t // tm
    tile = lambda w: pl.BlockSpec((tm, w), lambda i: (i, 0))
    assert n_tiles >= 2
    nxt = lambda i: jnp.minimum(i + 1, n_tiles - 1)
    ahead = lambda w: pl.BlockSpec((tm, w), lambda i: (nxt(i), 0))
    first = lambda w: pl.BlockSpec((tm, w), lambda i: (0, 0), pipeline_mode=pl.Buffered(1))
    hb = tm // BF16_ROWS
    n_halo = t // BF16_ROWS
    halo_prev = pl.BlockSpec((BF16_ROWS, CONV_W), lambda i: (nxt(i) * hb - 1, 0))
    halo_next = pl.BlockSpec((BF16_ROWS, CONV_W), lambda i: (jnp.minimum((nxt(i) + 1) * hb, n_halo - 1), 0))
    kv_blk = pl.BlockSpec((1, m, d), lambda i: (i // (seq // tm), 0, 0))
    mixer_w = [GLA_VAL_W, GLA_VAL_W, GLA_VAL_W, CONV_W, CONV_W]
    in_specs = [first(w) for w in mixer_w] + [tile(d)] + [ahead(w) for w in mixer_w] + [
        halo_prev, halo_next, _const_spec((1, GLA_VAL_W)), _const_spec(conv_w.shape),
        _const_spec((1, CONV_W)), _ANY_SPEC,
        _const_spec((1, d)), _ANY_SPEC, kv_blk, kv_blk, _ANY_SPEC,
        _const_spec((1, d)), _ANY_SPEC, _ANY_SPEC, _ANY_SPEC]
    args = [o_f, o_b, r, bg, u, x, o_f, o_b, r, bg, u, u, u, gla_norm.reshape(1, -1), conv_w,
            conv_b.reshape(1, -1), w_out,
            xattn_norm.reshape(1, d), w_q, mk, mv, w_xo, ffn_norm.reshape(1, d), wg, wu, wd]
    if final:
        in_specs.append(_const_spec((1, d)))
        args.append(final_norm.reshape(1, d))
    return pl.pallas_call(
        functools.partial(_post_body, seq=seq, final=final),
        grid=(t // tm,),
        in_specs=in_specs,
        out_specs=tile(d),
        out_shape=jax.ShapeDtypeStruct((t, d), F32),
        scratch_shapes=[pltpu.VMEM((tm, d_ff), BF16), pltpu.VMEM((tm, GLA_VAL_W + CONV_W), BF16),
                        pltpu.VMEM(w_out.shape, BF16), pltpu.VMEM(w_q.shape, BF16), pltpu.VMEM(w_xo.shape, BF16),
                        pltpu.VMEM((d, d_ff), BF16), pltpu.VMEM((d, d_ff), BF16), pltpu.VMEM((d_ff, d), BF16),
                        pltpu.VMEM((STAGE_SLOTS, STAGE_ROWS, max(d_ff, d)), F32),
                        pltpu.SemaphoreType.DMA((STAGE_SLOTS,))],
        compiler_params=_cparams(1),
        name="post_final" if final else "post",
    )(*args)


def _pack_gate(gate_w, gate_b, z_row0):
    w = jnp.pad(gate_w, ((z_row0, LANES - z_row0 - GATE_RANK), (0, 0)))
    return w.astype(BF16), gate_b.reshape(1, -1).astype(F32)


def kernel(x, mem, ffn1_norm, ffn1_w_gate, ffn1_w_up, ffn1_w_down, mix_norm, w_in, gate_fwd_w, gate_fwd_b,
           gate_bwd_w, gate_bwd_b, gla_norm, conv_w, conv_b, w_out, xattn_norm, mem_norm, xattn_w_q,
           xattn_w_kv, xattn_w_o, ffn2_norm, ffn2_w_gate, ffn2_w_up, ffn2_w_down, final_norm):
    batch, seq, d = x.shape
    depth = w_in.shape[0]
    assert seq % GLA_BLOCK == 0 and GLA_BLOCK % CHUNK == 0 and seq % TOKEN_TILE == 0
    h = x.reshape(batch * seq, d)
    for l in range(depth):
        h, q, k, v, r, bg, u, z = _pre(
            h, ffn1_norm[l], ffn1_w_gate[l], ffn1_w_up[l], ffn1_w_down[l], mix_norm[l], w_in[l])
        wgate_f, gbias_f = _pack_gate(gate_fwd_w[l], gate_fwd_b[l], 0)
        wgate_b, gbias_b = _pack_gate(gate_bwd_w[l], gate_bwd_b[l], GATE_RANK)
        o_f, o_b = _gla(q, k, v, z, wgate_f, wgate_b, gbias_f, gbias_b, batch, seq)
        mk, mv = _kv(mem, mem_norm[l], xattn_w_kv[l])
        h = _post(h, o_f, o_b, r, bg, u, gla_norm[l], conv_w[l], conv_b[l], w_out[l],
                  xattn_norm[l], xattn_w_q[l], mk, mv, xattn_w_o[l],
                  ffn2_norm[l], ffn2_w_gate[l], ffn2_w_up[l], ffn2_w_down[l], seq,
                  final_norm=final_norm if l == depth - 1 else None)
    return h.reshape(batch, seq, d)
```

```python
import functools

import jax
import jax.numpy as jnp
from jax import lax
from jax.experimental import pallas as pl
from jax.experimental.pallas import tpu as pltpu

F32 = jnp.float32
BF16 = jnp.bfloat16

XATTN_HEADS = 4
GLA_HEADS = 4
GLA_DK = 64
GLA_DV = 128
GLA_KEY_W = GLA_HEADS * GLA_DK
GLA_VAL_W = GLA_HEADS * GLA_DV
GATE_RANK = 16
GATE_TAU = 16.0
CHUNK = 64
CONV_W = 512
EPS = 1e-6

LANES = 128
SUBLANES = 8
BF16_ROWS = 16
TOKEN_TILE = 512
TILE_SPLIT = 2
GLA_BLOCK = 512
FFN_CHUNK = 256
STAGE_ROWS = 128
STAGE_SLOTS = 4
VMEM_LIMIT_BYTES = 58 * 1024 * 1024


def _cparams(n_axes):
    return pltpu.CompilerParams(
        dimension_semantics=("arbitrary",) * n_axes,
        vmem_limit_bytes=VMEM_LIMIT_BYTES,
    )


def _const_spec(shape):
    nd = len(shape)
    return pl.BlockSpec(shape, lambda *_: (0,) * nd, pipeline_mode=pl.Buffered(1))


_ANY_SPEC = pl.BlockSpec(memory_space=pl.ANY)


def _load_weights_bf16(jobs, stage_ref, sem_ref):
    n_slots, rows = stage_ref.shape[0], stage_ref.shape[1]
    ahead = n_slots - 1
    counts = [n_rows // rows for _, _, n_rows, _, _ in jobs]
    for (src, _, n_rows, dst, _), n in zip(jobs, counts):
        assert n_rows % rows == 0 and n >= ahead and src.shape[1] == dst.shape[1] <= stage_ref.shape[2]
    bases = [sum(counts[:j]) for j in range(len(jobs))]
    stream = [(j, c) for j, n in enumerate(counts) for c in range(n)]

    def copy(j, c, g):
        src, src_row0, _, _, _ = jobs[j]
        r0 = pl.multiple_of(src_row0 + c * rows, SUBLANES)
        return pltpu.make_async_copy(src.at[pl.ds(r0, rows), :],
                                     stage_ref.at[g % n_slots, :, pl.ds(0, src.shape[1])], sem_ref.at[g % n_slots])

    def finish(j, c, g):
        _, _, _, dst, dst_row0 = jobs[j]
        copy(j, c, g).wait()
        r0 = pl.multiple_of(dst_row0 + c * rows, BF16_ROWS)
        dst[pl.ds(r0, rows), :] = stage_ref[g % n_slots, :, 0:dst.shape[1]].astype(BF16)

    for g in range(ahead):
        copy(*stream[g], g).start()
    for j, n in enumerate(counts):
        def body(c, carry, j=j):
            copy(j, c + ahead, bases[j] + c + ahead).start()
            finish(j, c, bases[j] + c)
            return carry

        lax.fori_loop(0, n - ahead, body, 0)
        for c in range(n - ahead, n):
            g = bases[j] + c
            if g + ahead < len(stream):
                copy(*stream[g + ahead], g + ahead).start()
            finish(j, c, g)


def _rms(x, g):
    ms = jnp.mean(x * x, axis=-1, keepdims=True)
    return x * lax.rsqrt(ms + EPS) * g


def _dot(a, b):
    return jnp.dot(a, b, preferred_element_type=F32)


def _dot_nt(a, b):
    return lax.dot_general(a, b, (((1,), (1,)), ((), ())), preferred_element_type=F32)


def _row_slices(tile_rows):
    assert tile_rows % (TILE_SPLIT * BF16_ROWS) == 0
    part = tile_rows // TILE_SPLIT
    return [slice(i * part, (i + 1) * part) for i in range(TILE_SPLIT)]


def _run_staggered(stage_gens, lead=1):
    active = list(stage_gens)
    for _ in range(lead):
        next(active[0])
    while active:
        for gen in list(active):
            try:
                next(gen)
            except StopIteration:
                active.remove(gen)


def _swiglu_stages(x, rows, n_ref, wg_ref, wu_ref, wd_ref, a_ref, out):
    h = _rms(x, n_ref[...]).astype(BF16)
    yield
    for c in range(wg_ref.shape[1] // FFN_CHUNK):
        sl = slice(c * FFN_CHUNK, (c + 1) * FFN_CHUNK)
        g = _dot(h, wg_ref[:, sl])
        u = _dot(h, wu_ref[:, sl])
        a_ref[rows, sl] = (g * jax.nn.sigmoid(g) * u).astype(BF16)
    yield
    out["y"] = x + 0.5 * _dot(a_ref[rows, :], wd_ref[...])


_P_Q, _P_K, _P_V, _P_R = 0, GLA_KEY_W, 2 * GLA_KEY_W, 2 * GLA_KEY_W + GLA_VAL_W
_P_BG = _P_R + GLA_VAL_W
_P_CG = _P_BG + CONV_W
_P_XV = _P_CG + CONV_W
_P_Z = _P_XV + CONV_W
_P_END = _P_Z + LANES


def _pre_body(x_ref, fn_ref, wg_hbm, wu_hbm, wd_hbm, mn_ref, wint_hbm,
              x1_ref, q_ref, k_ref, v_ref, r_ref, bg_ref, u_ref, z_ref,
              a_ref, wg_ref, wu_ref, wd_ref, wp_ref, stage_ref, sem_ref):
    @pl.when(pl.program_id(0) == 0)
    def _():
        d, d_ff = wg_hbm.shape
        z_w = 2 * GATE_RANK
        main = _P_BG
        conv = _P_Z - _P_BG
        _load_weights_bf16(
            [(wg_hbm, 0, d, wg_ref, 0), (wu_hbm, 0, d, wu_ref, 0), (wd_hbm, 0, d_ff, wd_ref, 0),
             (wint_hbm, 0, main, wp_ref, 0), (wint_hbm, main + z_w, conv, wp_ref, _P_BG)],
            stage_ref, sem_ref)
        z_copy = pltpu.make_async_copy(wint_hbm.at[pl.ds(main, z_w), :],
                                       stage_ref.at[0, pl.ds(0, z_w), pl.ds(0, d)], sem_ref.at[0])
        z_copy.start()
        z_copy.wait()
        wp_ref[_P_Z:_P_Z + z_w, :] = stage_ref[0, 0:z_w, 0:d].astype(BF16)
        wp_ref[_P_Z + z_w:_P_END, :] = jnp.zeros((LANES - z_w, d), BF16)

    def stages(rows):
        ffn = {}
        yield from _swiglu_stages(x_ref[rows, :], rows, fn_ref, wg_ref, wu_ref, wd_ref, a_ref, ffn)
        x1 = ffn["y"]
        x1_ref[rows, :] = x1
        yield
        h = _rms(x1, mn_ref[...]).astype(BF16)
        yield
        p = _dot_nt(h, wp_ref[...])
        q_ref[rows, :] = p[:, _P_Q:_P_K] * (GLA_DK ** -0.5)
        k_ref[rows, :] = p[:, _P_K:_P_V]
        v_ref[rows, :] = p[:, _P_V:_P_R].astype(BF16)
        r_ref[rows, :] = p[:, _P_R:_P_BG].astype(BF16)
        bg_ref[rows, :] = p[:, _P_BG:_P_CG].astype(BF16)
        u_ref[rows, :] = (p[:, _P_CG:_P_XV] * p[:, _P_XV:_P_Z]).astype(BF16)
        z_ref[rows, :] = p[:, _P_Z:_P_END].astype(BF16)

    _run_staggered(stages(rows) for rows in _row_slices(x_ref.shape[0]))


def _pre(x, ffn_norm, wg, wu, wd, mix_norm, w_in):
    t, d = x.shape
    d_ff = wg.shape[1]
    assert t % TOKEN_TILE == 0 and d_ff % FFN_CHUNK == 0 and TOKEN_TILE % CHUNK == 0
    tile = lambda w: pl.BlockSpec((TOKEN_TILE, w), lambda i: (i, 0))
    out_w = [(d, F32), (GLA_KEY_W, F32), (GLA_KEY_W, F32), (GLA_VAL_W, BF16), (GLA_VAL_W, BF16),
             (CONV_W, BF16), (CONV_W, BF16), (LANES, BF16)]
    assert w_in.shape[1] == _P_Z + 2 * GATE_RANK
    return pl.pallas_call(
        _pre_body,
        grid=(t // TOKEN_TILE,),
        in_specs=[tile(d), _const_spec((1, d)), _ANY_SPEC, _ANY_SPEC, _ANY_SPEC, _const_spec((1, d)), _ANY_SPEC],
        out_specs=[tile(w) for w, _ in out_w],
        out_shape=[jax.ShapeDtypeStruct((t, w), dt) for w, dt in out_w],
        scratch_shapes=[pltpu.VMEM((TOKEN_TILE, d_ff), BF16),
                        pltpu.VMEM((d, d_ff), BF16), pltpu.VMEM((d, d_ff), BF16), pltpu.VMEM((d_ff, d), BF16),
                        pltpu.VMEM((_P_END, d), BF16),
                        pltpu.VMEM((STAGE_SLOTS, STAGE_ROWS, max(d_ff, d)), F32),
                        pltpu.SemaphoreType.DMA((STAGE_SLOTS,))],
        compiler_params=_cparams(1),
        name="pre",
    )(x, ffn_norm.reshape(1, d), wg, wu, wd, mix_norm.reshape(1, d), w_in.T)


def _chunk_cumsum(x, reverse):
    n, w = x.shape
    groups = n // SUBLANES
    x3 = x.reshape(groups, SUBLANES, w)
    pos = lax.broadcasted_iota(jnp.int32, x3.shape, 1)
    step = 1
    while step < SUBLANES:
        if reverse:
            x3 = x3 + jnp.where(pos < SUBLANES - step, pltpu.roll(x3, SUBLANES - step, 1), 0.0)
        else:
            x3 = x3 + jnp.where(pos >= step, pltpu.roll(x3, step, 1), 0.0)
        step *= 2
    per = CHUNK // SUBLANES
    x4 = x3.reshape(n // CHUNK, per, SUBLANES, w)
    edge = x4[:, :, 0:1, :] if reverse else x4[:, :, SUBLANES - 1:SUBLANES, :]
    outs = [None] * per
    carry = None
    for j in (range(per - 1, -1, -1) if reverse else range(per)):
        outs[j] = x4[:, j] if carry is None else x4[:, j] + carry
        carry = edge[:, j] if carry is None else carry + edge[:, j]
    return jnp.stack(outs, axis=1).reshape(n, w)


def _gla_chunks(q_ref, k_ref, v_ref, z_ref, wgate_ref, gbias_ref, o_ref, s_ref, *, reverse):
    n_chunks = q_ref.shape[0] // CHUNK
    kw = GLA_KEY_W
    assert GLA_DK == CHUNK and 2 * CHUNK == LANES == GLA_DV and CHUNK & (CHUNK - 1) == 0
    shift = CHUNK.bit_length() - 1
    gz = _dot(z_ref[...], wgate_ref[...]) + gbias_ref[...]

    row_head = lax.broadcasted_iota(jnp.int32, (kw, kw), 0) >> shift
    lane_head = lax.broadcasted_iota(jnp.int32, (kw, kw), 1) >> shift
    same_head = row_head == lane_head
    pos_q = lax.broadcasted_iota(jnp.int32, (CHUNK, kw), 0)
    pos_k = lax.broadcasted_iota(jnp.int32, (CHUNK, kw), 1) & (CHUNK - 1)
    causal = (pos_k >= pos_q) if reverse else (pos_k <= pos_q)
    low_half = lax.broadcasted_iota(jnp.int32, (CHUNK, LANES), 1) < CHUNK
    low_feat = lax.broadcasted_iota(jnp.int32, (kw, LANES), 1) < CHUNK

    order = range(n_chunks - 1, -1, -1) if reverse else range(n_chunks)
    for c in order:
        rows = slice(c * CHUNK, (c + 1) * CHUNK)
        g = gz[rows, :]
        log_gate = jnp.minimum(g, 0.0) - jnp.log(1.0 + jnp.exp(-jnp.abs(g)))
        bc = _chunk_cumsum(log_gate * (1.0 / GATE_TAU), reverse)
        q = q_ref[rows, :]
        k = k_ref[rows, :]
        v = v_ref[rows, :]
        b_last = bc[0:1, :] if reverse else bc[CHUNK - 1:CHUNK, :]
        q_in = q * jnp.exp(bc)
        k_in = k * jnp.exp(-bc)
        k_st = k * jnp.exp(b_last - bc)
        k_blk = jnp.where(same_head, jnp.concatenate([k_in] * GLA_HEADS, axis=0), 0.0).astype(BF16)
        att = _dot_nt(q_in.astype(BF16), k_blk)
        att = jnp.where(causal, att, 0.0)
        decay_rows = jnp.broadcast_to(jnp.exp(b_last), (SUBLANES, kw))
        pad = jnp.zeros((LANES - CHUNK - SUBLANES, kw), F32)
        xt = jnp.concatenate([k_st, decay_rows, pad], axis=0).T
        kst_t = jnp.where(low_feat, xt, 0.0).astype(BF16)
        outs = []
        for pair in range(GLA_HEADS // 2):
            lanes = slice(pair * LANES, (pair + 1) * LANES)
            a_col, q_col = att[:, lanes], q_in[:, lanes]
            a_swapped = pltpu.roll(a_col, CHUNK, 1)
            q_swapped = pltpu.roll(q_col, CHUNK, 1)
            top = jnp.concatenate([jnp.where(low_half, a_col, q_swapped),
                                   jnp.where(low_half, a_swapped, q_col)], axis=1).astype(BF16)
            heads = (2 * pair, 2 * pair + 1)
            bottom = jnp.concatenate([kst_t[h * GLA_DK:(h + 1) * GLA_DK, :] for h in heads], axis=1)
            s_in = [s_ref[h] for h in heads]
            zero = jnp.zeros((LANES, GLA_DV), BF16)
            vs = [jnp.concatenate([v[:, h * GLA_DV:(h + 1) * GLA_DV], s.astype(BF16)], axis=0)
                  for h, s in zip(heads, s_in)]
            rhs = jnp.concatenate([jnp.concatenate([vs[0], zero], axis=1),
                                   jnp.concatenate([zero, vs[1]], axis=1)], axis=0)
            res = _dot(jnp.concatenate([top, bottom], axis=0), rhs)
            outs.append(res[:CHUNK, :])
            for i, (h, s) in enumerate(zip(heads, s_in)):
                decay = xt[h * GLA_DK:(h + 1) * GLA_DK, CHUNK:CHUNK + 1]
                s_ref[h] = decay * s + res[CHUNK:, i * GLA_DV:(i + 1) * GLA_DV]
        o_ref[rows, :] = jnp.concatenate(outs, axis=1).astype(o_ref.dtype)
        yield


def _gla_body(qf_ref, kf_ref, vf_ref, zf_ref, qb_ref, kb_ref, vb_ref, zb_ref, wgf_ref, wgb_ref, gbf_ref, gbb_ref,
              of_ref, ob_ref, sf_ref, sb_ref):
    @pl.when(pl.program_id(1) == 0)
    def _():
        sf_ref[...] = jnp.zeros_like(sf_ref)
        sb_ref[...] = jnp.zeros_like(sb_ref)

    fwd = _gla_chunks(qf_ref, kf_ref, vf_ref, zf_ref, wgf_ref, gbf_ref, of_ref, sf_ref, reverse=False)
    bwd = _gla_chunks(qb_ref, kb_ref, vb_ref, zb_ref, wgb_ref, gbb_ref, ob_ref, sb_ref, reverse=True)
    for _ in zip(fwd, bwd):
        pass


def _gla(q, k, v, z, wgate_f, wgate_b, gbias_f, gbias_b, batch, seq):
    nb = seq // GLA_BLOCK
    fwd = lambda w: pl.BlockSpec((GLA_BLOCK, w), lambda b, j: (b * nb + j, 0))
    bwd = lambda w: pl.BlockSpec((GLA_BLOCK, w), lambda b, j: (b * nb + nb - 1 - j, 0))
    t = batch * seq
    return pl.pallas_call(
        _gla_body,
        grid=(batch, nb),
        in_specs=[fwd(GLA_KEY_W), fwd(GLA_KEY_W), fwd(GLA_VAL_W), fwd(LANES),
                  bwd(GLA_KEY_W), bwd(GLA_KEY_W), bwd(GLA_VAL_W), bwd(LANES),
                  _const_spec(wgate_f.shape), _const_spec(wgate_b.shape),
                  _const_spec(gbias_f.shape), _const_spec(gbias_b.shape)],
        out_specs=[fwd(GLA_VAL_W), bwd(GLA_VAL_W)],
        out_shape=[jax.ShapeDtypeStruct((t, GLA_VAL_W), BF16)] * 2,
        scratch_shapes=[pltpu.VMEM((GLA_HEADS, GLA_DK, GLA_DV), F32)] * 2,
        compiler_params=_cparams(2),
        name="gla",
    )(q, k, v, z, q, k, v, z, wgate_f, wgate_b, gbias_f, gbias_b)


def _kv_body(m_ref, n_ref, w_ref, k_ref, v_ref):
    d = m_ref.shape[-1]
    h = _rms(m_ref[...], n_ref[...]).astype(BF16)
    kv = _dot(h, w_ref[...].astype(BF16))
    k_ref[...] = kv[:, :d].astype(BF16)
    v_ref[...] = kv[:, d:].astype(BF16)


def _kv(mem, norm, w_kv):
    b, m, d = mem.shape
    k, v = pl.pallas_call(
        _kv_body,
        grid=(1,),
        in_specs=[_const_spec((b * m, d)), _const_spec((1, d)), _const_spec(w_kv.shape)],
        out_specs=[_const_spec((b * m, d))] * 2,
        out_shape=[jax.ShapeDtypeStruct((b * m, d), BF16)] * 2,
        compiler_params=_cparams(1),
        name="kv",
    )(mem.reshape(b * m, d), norm.reshape(1, d), w_kv)
    return k.reshape(b, m, d), v.reshape(b, m, d)


def _mixed_heads(rows, tile, of_ref, ob_ref, r_ref, bg_ref, u_ref, edge_before, edge_after, gn_ref, cw_ref, cb_ref,
                 seq):
    tm = u_ref.shape[0]
    n = rows.stop - rows.start
    o = of_ref[rows, :].astype(F32) + ob_ref[rows, :].astype(F32)
    heads = []
    for h in range(GLA_HEADS):
        oh = o[:, h * GLA_DV:(h + 1) * GLA_DV]
        heads.append(oh * lax.rsqrt(jnp.mean(oh * oh, axis=-1, keepdims=True) + EPS))
    r = r_ref[rows, :].astype(F32)
    a_out = jnp.concatenate(heads, axis=1) * gn_ref[...] * (r * jax.nn.sigmoid(r))
    u = u_ref[rows, :].astype(F32)
    if rows.start == 0:
        at_seq_start = (tile * tm) % seq == 0
        before = jnp.where(at_seq_start, 0.0, edge_before.astype(F32))
    else:
        before = u_ref[rows.start - 1:rows.start, :].astype(F32)
    if rows.stop == tm:
        at_seq_end = ((tile + 1) * tm) % seq == 0
        after = jnp.where(at_seq_end, 0.0, edge_after.astype(F32))
    else:
        after = u_ref[rows.stop:rows.stop + 1, :].astype(F32)
    row = lax.broadcasted_iota(jnp.int32, u.shape, 0)
    u_prev = jnp.where(row == 0, before, pltpu.roll(u, 1, 0))
    u_next = jnp.where(row == n - 1, after, pltpu.roll(u, n - 1, 0))
    conv = cw_ref[0:1, :] * u_prev + cw_ref[1:2, :] * u + cw_ref[2:3, :] * u_next + cb_ref[...]
    c_out = bg_ref[rows, :].astype(F32) * conv
    return jnp.concatenate([a_out, c_out], axis=1).astype(BF16)


def _cross_attention_stages(x, n_ref, wq_ref, k_ref, v_ref, wo_ref, out):
    d = x.shape[-1]
    hd = d // XATTN_HEADS
    h = _rms(x, n_ref[...]).astype(BF16)
    yield
    q = _dot(h, wq_ref[...])
    yield
    outs = []
    for a in range(XATTN_HEADS):
        sl = slice(a * hd, (a + 1) * hd)
        s = _dot_nt(q[:, sl].astype(BF16), k_ref[0, :, sl]) * (hd ** -0.5)
        p = jnp.exp(s - jnp.max(s, axis=-1, keepdims=True))
        denom = jnp.sum(p, axis=-1, keepdims=True)
        outs.append(_dot(p.astype(BF16), v_ref[0, :, sl]) / denom)
    o = jnp.concatenate(outs, axis=1).astype(BF16)
    yield
    out["y"] = x + _dot(o, wo_ref[...])


def _post_body(*refs, seq, final):
    (x_ref, of_ref, ob_ref, r_ref, bg_ref, u_ref, up_ref, un_ref, gn_ref, cw_ref, cb_ref, wo_hbm,
     xn_ref, wq_hbm, k_ref, v_ref, wxo_hbm, fn_ref, wg_hbm, wu_hbm, wd_hbm) = refs[:21]
    refs = refs[21:]
    if final:
        last_ref, refs = refs[0], refs[1:]
    o_ref, a_ref, wo_ref, wq_ref, wxo_ref, wg_ref, wu_ref, wd_ref, stage_ref, sem_ref = refs
    tm = x_ref.shape[0]
    i = pl.program_id(0)

    @pl.when(i == 0)
    def _():
        _load_weights_bf16(
            [(src, 0, src.shape[0], dst, 0) for src, dst in
             ((wo_hbm, wo_ref), (wq_hbm, wq_ref), (wxo_hbm, wxo_ref), (wg_hbm, wg_ref), (wu_hbm, wu_ref),
              (wd_hbm, wd_ref))],
            stage_ref, sem_ref)

    def stages(rows):
        mixed = _mixed_heads(rows, i, of_ref, ob_ref, r_ref, bg_ref, u_ref, up_ref[BF16_ROWS - 1:BF16_ROWS, :],
                             un_ref[0:1, :], gn_ref, cw_ref, cb_ref, seq)
        yield
        x2 = x_ref[rows, :] + _dot(mixed, wo_ref[...])
        yield
        box = {}
        yield from _cross_attention_stages(x2, xn_ref, wq_ref, k_ref, v_ref, wxo_ref, box)
        yield
        x3 = box["y"]
        yield from _swiglu_stages(x3, rows, fn_ref, wg_ref, wu_ref, wd_ref, a_ref, box)
        y = box["y"]
        if final:
            yield
            y = _rms(y, last_ref[...])
        o_ref[rows, :] = y

    _run_staggered(stages(rows) for rows in _row_slices(tm))


def _post(x, o_f, o_b, r, bg, u, gla_norm, conv_w, conv_b, w_out, xattn_norm, w_q, mk, mv, w_xo,
          ffn_norm, wg, wu, wd, seq, final_norm=None):
    t, d = x.shape
    tm = TOKEN_TILE
    d_ff = wg.shape[1]
    m = mk.shape[1]
    assert seq % tm == 0 and tm % BF16_ROWS == 0 and d_ff % FFN_CHUNK == 0
    final = final_norm is not None
    tile = lambda w: pl.BlockSpec((tm, w), lambda i: (i, 0))
    hb = tm // BF16_ROWS
    n_halo = t // BF16_ROWS
    halo_prev = pl.BlockSpec((BF16_ROWS, CONV_W), lambda i: (jnp.maximum(i * hb - 1, 0), 0))
    halo_next = pl.BlockSpec((BF16_ROWS, CONV_W), lambda i: (jnp.minimum((i + 1) * hb, n_halo - 1), 0))
    kv_blk = pl.BlockSpec((1, m, d), lambda i: (i // (seq // tm), 0, 0))
    in_specs = [tile(d), tile(GLA_VAL_W), tile(GLA_VAL_W), tile(GLA_VAL_W), tile(CONV_W), tile(CONV_W),
                halo_prev, halo_next, _const_spec((1, GLA_VAL_W)), _const_spec(conv_w.shape),
                _const_spec((1, CONV_W)), _ANY_SPEC,
                _const_spec((1, d)), _ANY_SPEC, kv_blk, kv_blk, _ANY_SPEC,
                _const_spec((1, d)), _ANY_SPEC, _ANY_SPEC, _ANY_SPEC]
    args = [x, o_f, o_b, r, bg, u, u, u, gla_norm.reshape(1, -1), conv_w, conv_b.reshape(1, -1), w_out,
            xattn_norm.reshape(1, d), w_q, mk, mv, w_xo, ffn_norm.reshape(1, d), wg, wu, wd]
    if final:
        in_specs.append(_const_spec((1, d)))
        args.append(final_norm.reshape(1, d))
    return pl.pallas_call(
        functools.partial(_post_body, seq=seq, final=final),
        grid=(t // tm,),
        in_specs=in_specs,
        out_specs=tile(d),
        out_shape=jax.ShapeDtypeStruct((t, d), F32),
        scratch_shapes=[pltpu.VMEM((tm, d_ff), BF16),
                        pltpu.VMEM(w_out.shape, BF16), pltpu.VMEM(w_q.shape, BF16), pltpu.VMEM(w_xo.shape, BF16),
                        pltpu.VMEM((d, d_ff), BF16), pltpu.VMEM((d, d_ff), BF16), pltpu.VMEM((d_ff, d), BF16),
                        pltpu.VMEM((STAGE_SLOTS, STAGE_ROWS, max(d_ff, d)), F32),
                        pltpu.SemaphoreType.DMA((STAGE_SLOTS,))],
        compiler_params=_cparams(1),
        name="post_final" if final else "post",
    )(*args)


def _pack_gate(gate_w, gate_b, z_row0):
    w = jnp.pad(gate_w, ((z_row0, LANES - z_row0 - GATE_RANK), (0, 0)))
    return w.astype(BF16), gate_b.reshape(1, -1).astype(F32)


def kernel(x, mem, ffn1_norm, ffn1_w_gate, ffn1_w_up, ffn1_w_down, mix_norm, w_in, gate_fwd_w, gate_fwd_b,
           gate_bwd_w, gate_bwd_b, gla_norm, conv_w, conv_b, w_out, xattn_norm, mem_norm, xattn_w_q,
           xattn_w_kv, xattn_w_o, ffn2_norm, ffn2_w_gate, ffn2_w_up, ffn2_w_down, final_norm):
    batch, seq, d = x.shape
    depth = w_in.shape[0]
    assert seq % GLA_BLOCK == 0 and GLA_BLOCK % CHUNK == 0 and seq % TOKEN_TILE == 0
    h = x.reshape(batch * seq, d)
    for l in range(depth):
        h, q, k, v, r, bg, u, z = _pre(
            h, ffn1_norm[l], ffn1_w_gate[l], ffn1_w_up[l], ffn1_w_down[l], mix_norm[l], w_in[l])
        wgate_f, gbias_f = _pack_gate(gate_fwd_w[l], gate_fwd_b[l], 0)
        wgate_b, gbias_b = _pack_gate(gate_bwd_w[l], gate_bwd_b[l], GATE_RANK)
        o_f, o_b = _gla(q, k, v, z, wgate_f, wgate_b, gbias_f, gbias_b, batch, seq)
        mk, mv = _kv(mem, mem_norm[l], xattn_w_kv[l])
        h = _post(h, o_f, o_b, r, bg, u, gla_norm[l], conv_w[l], conv_b[l], w_out[l],
                  xattn_norm[l], xattn_w_q[l], mk, mv, xattn_w_o[l],
                  ffn2_norm[l], ffn2_w_gate[l], ffn2_w_up[l], ffn2_w_down[l], seq,
                  final_norm=final_norm if l == depth - 1 else None)
    return h.reshape(batch, seq, d)
```

```python
import functools

import jax
import jax.numpy as jnp
from jax import lax
from jax.experimental import pallas as pl
from jax.experimental.pallas import tpu as pltpu

F32 = jnp.float32
BF16 = jnp.bfloat16

XATTN_HEADS = 4
GLA_HEADS = 4
GLA_DK = 64
GLA_DV = 128
GLA_KEY_W = GLA_HEADS * GLA_DK
GLA_VAL_W = GLA_HEADS * GLA_DV
GATE_RANK = 16
GATE_TAU = 16.0
CHUNK = 64
CONV_W = 512
EPS = 1e-6
LOG2_E = 1.4426950408889634

LANES = 128
SUBLANES = 8
BF16_ROWS = 16
TOKEN_TILE = 512
TILE_SPLIT = 2
GLA_BLOCK = 512
GLA_PIPELINE_DEPTH = 2
FFN_CHUNK = 256
STAGE_ROWS = 128
STAGE_SLOTS = 4
VMEM_LIMIT_BYTES = 58 * 1024 * 1024


def _cparams(n_axes):
    return pltpu.CompilerParams(
        dimension_semantics=("arbitrary",) * n_axes,
        vmem_limit_bytes=VMEM_LIMIT_BYTES,
    )


def _const_spec(shape):
    nd = len(shape)
    return pl.BlockSpec(shape, lambda *_: (0,) * nd, pipeline_mode=pl.Buffered(1))


_ANY_SPEC = pl.BlockSpec(memory_space=pl.ANY)


def _load_weights_bf16(jobs, stage_ref, sem_ref):
    n_slots, rows = stage_ref.shape[0], stage_ref.shape[1]
    ahead = n_slots - 1
    counts = [n_rows // rows for _, _, n_rows, _, _ in jobs]
    for (src, _, n_rows, dst, _), n in zip(jobs, counts):
        assert n_rows % rows == 0 and n >= ahead and src.shape[1] == dst.shape[1] <= stage_ref.shape[2]
    bases = [sum(counts[:j]) for j in range(len(jobs))]
    stream = [(j, c) for j, n in enumerate(counts) for c in range(n)]

    def copy(j, c, g):
        src, src_row0, _, _, _ = jobs[j]
        r0 = pl.multiple_of(src_row0 + c * rows, SUBLANES)
        return pltpu.make_async_copy(src.at[pl.ds(r0, rows), :],
                                     stage_ref.at[g % n_slots, :, pl.ds(0, src.shape[1])], sem_ref.at[g % n_slots])

    def finish(j, c, g):
        _, _, _, dst, dst_row0 = jobs[j]
        copy(j, c, g).wait()
        r0 = pl.multiple_of(dst_row0 + c * rows, BF16_ROWS)
        dst[pl.ds(r0, rows), :] = stage_ref[g % n_slots, :, 0:dst.shape[1]].astype(BF16)

    for g in range(ahead):
        copy(*stream[g], g).start()
    for j, n in enumerate(counts):
        def body(c, carry, j=j):
            copy(j, c + ahead, bases[j] + c + ahead).start()
            finish(j, c, bases[j] + c)
            return carry

        lax.fori_loop(0, n - ahead, body, 0)
        for c in range(n - ahead, n):
            g = bases[j] + c
            if g + ahead < len(stream):
                copy(*stream[g + ahead], g + ahead).start()
            finish(j, c, g)


def _rms(x, g):
    ms = jnp.mean(x * x, axis=-1, keepdims=True)
    return x * lax.rsqrt(ms + EPS) * g


def _dot(a, b):
    return jnp.dot(a, b, preferred_element_type=F32)


def _dot_nt(a, b):
    return lax.dot_general(a, b, (((1,), (1,)), ((), ())), preferred_element_type=F32)


def _row_slices(tile_rows):
    assert tile_rows % (TILE_SPLIT * BF16_ROWS) == 0
    part = tile_rows // TILE_SPLIT
    return [slice(i * part, (i + 1) * part) for i in range(TILE_SPLIT)]


def _run_staggered(stage_gens, lead=1):
    active = list(stage_gens)
    for _ in range(lead):
        next(active[0])
    while active:
        for gen in list(active):
            try:
                next(gen)
            except StopIteration:
                active.remove(gen)


def _swiglu_stages(x, rows, n_ref, wg_ref, wu_ref, wd_ref, a_ref, out):
    h = _rms(x, n_ref[...]).astype(BF16)
    yield
    for c in range(wg_ref.shape[1] // FFN_CHUNK):
        sl = slice(c * FFN_CHUNK, (c + 1) * FFN_CHUNK)
        g = _dot(h, wg_ref[:, sl])
        u = _dot(h, wu_ref[:, sl])
        a_ref[rows, sl] = (g * jax.nn.sigmoid(g) * u).astype(BF16)
    yield
    out["y"] = x + 0.5 * _dot(a_ref[rows, :], wd_ref[...])


_P_Q, _P_K, _P_V, _P_R = 0, GLA_KEY_W, 2 * GLA_KEY_W, 2 * GLA_KEY_W + GLA_VAL_W
_P_BG = _P_R + GLA_VAL_W
_P_CG = _P_BG + CONV_W
_P_XV = _P_CG + CONV_W
_P_Z = _P_XV + CONV_W
_P_END = _P_Z + LANES


def _pre_body(x_ref, fn_ref, wg_hbm, wu_hbm, wd_hbm, mn_ref, wint_hbm,
              x1_ref, q_ref, k_ref, v_ref, r_ref, bg_ref, u_ref, z_ref,
              a_ref, wg_ref, wu_ref, wd_ref, wp_ref, stage_ref, sem_ref):
    @pl.when(pl.program_id(0) == 0)
    def _():
        d, d_ff = wg_hbm.shape
        z_w = 2 * GATE_RANK
        main = _P_BG
        conv = _P_Z - _P_BG
        _load_weights_bf16(
            [(wg_hbm, 0, d, wg_ref, 0), (wu_hbm, 0, d, wu_ref, 0), (wd_hbm, 0, d_ff, wd_ref, 0),
             (wint_hbm, 0, main, wp_ref, 0), (wint_hbm, main + z_w, conv, wp_ref, _P_BG)],
            stage_ref, sem_ref)
        z_copy = pltpu.make_async_copy(wint_hbm.at[pl.ds(main, z_w), :],
                                       stage_ref.at[0, pl.ds(0, z_w), pl.ds(0, d)], sem_ref.at[0])
        z_copy.start()
        z_copy.wait()
        wp_ref[_P_Z:_P_Z + z_w, :] = stage_ref[0, 0:z_w, 0:d].astype(BF16)
        wp_ref[_P_Z + z_w:_P_END, :] = jnp.zeros((LANES - z_w, d), BF16)

    def stages(rows):
        ffn = {}
        yield from _swiglu_stages(x_ref[rows, :], rows, fn_ref, wg_ref, wu_ref, wd_ref, a_ref, ffn)
        x1 = ffn["y"]
        x1_ref[rows, :] = x1
        yield
        h = _rms(x1, mn_ref[...]).astype(BF16)
        yield
        p = _dot_nt(h, wp_ref[...])
        q_ref[rows, :] = p[:, _P_Q:_P_K] * (GLA_DK ** -0.5)
        k_ref[rows, :] = p[:, _P_K:_P_V]
        v_ref[rows, :] = p[:, _P_V:_P_R].astype(BF16)
        r_ref[rows, :] = p[:, _P_R:_P_BG].astype(BF16)
        bg_ref[rows, :] = p[:, _P_BG:_P_CG].astype(BF16)
        u_ref[rows, :] = (p[:, _P_CG:_P_XV] * p[:, _P_XV:_P_Z]).astype(BF16)
        z_ref[rows, :] = p[:, _P_Z:_P_END].astype(BF16)

    _run_staggered(stages(rows) for rows in _row_slices(x_ref.shape[0]))


def _pre(x, ffn_norm, wg, wu, wd, mix_norm, w_in):
    t, d = x.shape
    d_ff = wg.shape[1]
    assert t % TOKEN_TILE == 0 and d_ff % FFN_CHUNK == 0 and TOKEN_TILE % CHUNK == 0
    tile = lambda w: pl.BlockSpec((TOKEN_TILE, w), lambda i: (i, 0))
    out_w = [(d, F32), (GLA_KEY_W, F32), (GLA_KEY_W, F32), (GLA_VAL_W, BF16), (GLA_VAL_W, BF16),
             (CONV_W, BF16), (CONV_W, BF16), (LANES, BF16)]
    assert w_in.shape[1] == _P_Z + 2 * GATE_RANK
    return pl.pallas_call(
        _pre_body,
        grid=(t // TOKEN_TILE,),
        in_specs=[tile(d), _const_spec((1, d)), _ANY_SPEC, _ANY_SPEC, _ANY_SPEC, _const_spec((1, d)), _ANY_SPEC],
        out_specs=[tile(w) for w, _ in out_w],
        out_shape=[jax.ShapeDtypeStruct((t, w), dt) for w, dt in out_w],
        scratch_shapes=[pltpu.VMEM((TOKEN_TILE, d_ff), BF16),
                        pltpu.VMEM((d, d_ff), BF16), pltpu.VMEM((d, d_ff), BF16), pltpu.VMEM((d_ff, d), BF16),
                        pltpu.VMEM((_P_END, d), BF16),
                        pltpu.VMEM((STAGE_SLOTS, STAGE_ROWS, max(d_ff, d)), F32),
                        pltpu.SemaphoreType.DMA((STAGE_SLOTS,))],
        compiler_params=_cparams(1),
        name="pre",
    )(x, ffn_norm.reshape(1, d), wg, wu, wd, mix_norm.reshape(1, d), w_in.T)


def _chunk_cumsum(x, reverse):
    n, w = x.shape
    groups = n // SUBLANES
    x3 = x.reshape(groups, SUBLANES, w)
    pos = lax.broadcasted_iota(jnp.int32, x3.shape, 1)
    step = 1
    while step < SUBLANES:
        if reverse:
            x3 = x3 + jnp.where(pos < SUBLANES - step, pltpu.roll(x3, SUBLANES - step, 1), 0.0)
        else:
            x3 = x3 + jnp.where(pos >= step, pltpu.roll(x3, step, 1), 0.0)
        step *= 2
    per = CHUNK // SUBLANES
    x4 = x3.reshape(n // CHUNK, per, SUBLANES, w)
    edge = x4[:, :, 0:1, :] if reverse else x4[:, :, SUBLANES - 1:SUBLANES, :]
    outs = [None] * per
    carry = None
    for j in (range(per - 1, -1, -1) if reverse else range(per)):
        outs[j] = x4[:, j] if carry is None else x4[:, j] + carry
        carry = edge[:, j] if carry is None else carry + edge[:, j]
    return jnp.stack(outs, axis=1).reshape(n, w)


def _gla_chunks(q_ref, k_ref, v_ref, z_ref, wgate_ref, gbias_ref, o_ref, s_ref, *, reverse):
    n_chunks = q_ref.shape[0] // CHUNK
    kw = GLA_KEY_W
    assert GLA_DK == CHUNK and 2 * CHUNK == LANES == GLA_DV and CHUNK & (CHUNK - 1) == 0
    shift = CHUNK.bit_length() - 1
    gz = _dot(z_ref[...], wgate_ref[...]) + gbias_ref[...]

    row_head = lax.broadcasted_iota(jnp.int32, (kw, kw), 0) >> shift
    lane_head = lax.broadcasted_iota(jnp.int32, (kw, kw), 1) >> shift
    same_head = row_head.astype(F32).astype(BF16) == lane_head.astype(F32).astype(BF16)
    pos_q = lax.broadcasted_iota(jnp.int32, (CHUNK, kw), 0)
    pos_k = lax.broadcasted_iota(jnp.int32, (CHUNK, kw), 1) & (CHUNK - 1)
    causal = (pos_k >= pos_q) if reverse else (pos_k <= pos_q)
    low_half = lax.broadcasted_iota(jnp.int32, (CHUNK, LANES), 1) < CHUNK
    low_feat = lax.broadcasted_iota(jnp.int32, (kw, LANES), 1) < CHUNK

    def chunk(c):
        rows = slice(c * CHUNK, (c + 1) * CHUNK)
        g = gz[rows, :]
        log2_gate = (jnp.minimum(g, 0.0) - jnp.log(1.0 + jnp.exp2(jnp.abs(g) * -LOG2_E))) * (LOG2_E / GATE_TAU)
        bc = _chunk_cumsum(log2_gate, reverse)
        q = q_ref[rows, :]
        k = k_ref[rows, :]
        v = v_ref[rows, :]
        b_last = bc[0:1, :] if reverse else bc[CHUNK - 1:CHUNK, :]
        q_in = q * jnp.exp2(bc)
        k_in = (k * jnp.exp2(-bc)).astype(BF16)
        k_st = k * jnp.exp2(b_last - bc)
        k_blk = jnp.where(same_head, jnp.concatenate([k_in] * GLA_HEADS, axis=0), jnp.zeros((), BF16))
        att = _dot_nt(q_in.astype(BF16), k_blk)
        att = jnp.where(causal, att, 0.0)
        decay_rows = jnp.broadcast_to(jnp.exp2(b_last), (SUBLANES, kw))
        pad = jnp.zeros((LANES - CHUNK - SUBLANES, kw), F32)
        xt = jnp.concatenate([k_st, decay_rows, pad], axis=0).T
        kst_t = jnp.where(low_feat, xt, 0.0).astype(BF16)
        yield
        outs = []
        for pair in range(GLA_HEADS // 2):
            lanes = slice(pair * LANES, (pair + 1) * LANES)
            a_col, q_col = att[:, lanes], q_in[:, lanes]
            a_swapped = pltpu.roll(a_col, CHUNK, 1)
            q_swapped = pltpu.roll(q_col, CHUNK, 1)
            top = jnp.concatenate([jnp.where(low_half, a_col, q_swapped),
                                   jnp.where(low_half, a_swapped, q_col)], axis=1).astype(BF16)
            heads = (2 * pair, 2 * pair + 1)
            bottom = jnp.concatenate([kst_t[h * GLA_DK:(h + 1) * GLA_DK, :] for h in heads], axis=1)
            s_in = [s_ref[h] for h in heads]
            zero = jnp.zeros((LANES, GLA_DV), BF16)
            vs = [jnp.concatenate([v[:, h * GLA_DV:(h + 1) * GLA_DV], s.astype(BF16)], axis=0)
                  for h, s in zip(heads, s_in)]
            rhs = jnp.concatenate([jnp.concatenate([vs[0], zero], axis=1),
                                   jnp.concatenate([zero, vs[1]], axis=1)], axis=0)
            res = _dot(jnp.concatenate([top, bottom], axis=0), rhs)
            outs.append(res[:CHUNK, :])
            for i, (h, s) in enumerate(zip(heads, s_in)):
                decay = xt[h * GLA_DK:(h + 1) * GLA_DK, CHUNK:CHUNK + 1]
                s_ref[h] = decay * s + res[CHUNK:, i * GLA_DV:(i + 1) * GLA_DV]
        o_ref[rows, :] = jnp.concatenate(outs, axis=1).astype(o_ref.dtype)

    return [chunk(c) for c in (range(n_chunks - 1, -1, -1) if reverse else range(n_chunks))]


def _run_pipelined(two_stage_gens, depth):
    started = []
    for gen in two_stage_gens:
        next(gen)
        started.append(gen)
        if len(started) > depth:
            for _ in started.pop(0):
                pass
    for gen in started:
        for _ in gen:
            pass


def _gla_body(qf_ref, kf_ref, vf_ref, zf_ref, qb_ref, kb_ref, vb_ref, zb_ref, wgf_ref, wgb_ref, gbf_ref, gbb_ref,
              of_ref, ob_ref, sf_ref, sb_ref):
    @pl.when(pl.program_id(1) == 0)
    def _():
        sf_ref[...] = jnp.zeros_like(sf_ref)
        sb_ref[...] = jnp.zeros_like(sb_ref)

    fwd = _gla_chunks(qf_ref, kf_ref, vf_ref, zf_ref, wgf_ref, gbf_ref, of_ref, sf_ref, reverse=False)
    bwd = _gla_chunks(qb_ref, kb_ref, vb_ref, zb_ref, wgb_ref, gbb_ref, ob_ref, sb_ref, reverse=True)
    _run_pipelined([gen for pair in zip(fwd, bwd) for gen in pair], GLA_PIPELINE_DEPTH)


def _gla(q, k, v, z, wgate_f, wgate_b, gbias_f, gbias_b, batch, seq):
    nb = seq // GLA_BLOCK
    fwd = lambda w: pl.BlockSpec((GLA_BLOCK, w), lambda b, j: (b * nb + j, 0))
    bwd = lambda w: pl.BlockSpec((GLA_BLOCK, w), lambda b, j: (b * nb + nb - 1 - j, 0))
    t = batch * seq
    return pl.pallas_call(
        _gla_body,
        grid=(batch, nb),
        in_specs=[fwd(GLA_KEY_W), fwd(GLA_KEY_W), fwd(GLA_VAL_W), fwd(LANES),
                  bwd(GLA_KEY_W), bwd(GLA_KEY_W), bwd(GLA_VAL_W), bwd(LANES),
                  _const_spec(wgate_f.shape), _const_spec(wgate_b.shape),
                  _const_spec(gbias_f.shape), _const_spec(gbias_b.shape)],
        out_specs=[fwd(GLA_VAL_W), bwd(GLA_VAL_W)],
        out_shape=[jax.ShapeDtypeStruct((t, GLA_VAL_W), BF16)] * 2,
        scratch_shapes=[pltpu.VMEM((GLA_HEADS, GLA_DK, GLA_DV), F32)] * 2,
        compiler_params=_cparams(2),
        name="gla",
    )(q, k, v, z, q, k, v, z, wgate_f, wgate_b, gbias_f, gbias_b)


def _kv_body(m_ref, n_ref, w_ref, k_ref, v_ref):
    d = m_ref.shape[-1]
    h = _rms(m_ref[...], n_ref[...]).astype(BF16)
    kv = _dot(h, w_ref[...].astype(BF16))
    k_ref[...] = kv[:, :d].astype(BF16)
    v_ref[...] = kv[:, d:].astype(BF16)


def _kv(mem, norm, w_kv):
    b, m, d = mem.shape
    k, v = pl.pallas_call(
        _kv_body,
        grid=(1,),
        in_specs=[_const_spec((b * m, d)), _const_spec((1, d)), _const_spec(w_kv.shape)],
        out_specs=[_const_spec((b * m, d))] * 2,
        out_shape=[jax.ShapeDtypeStruct((b * m, d), BF16)] * 2,
        compiler_params=_cparams(1),
        name="kv",
    )(mem.reshape(b * m, d), norm.reshape(1, d), w_kv)
    return k.reshape(b, m, d), v.reshape(b, m, d)


def _mixed_heads(rows, tile, of_ref, ob_ref, r_ref, bg_ref, u_ref, edge_before, edge_after, gn_ref, cw_ref, cb_ref,
                 seq):
    tm = u_ref.shape[0]
    n = rows.stop - rows.start
    o = of_ref[rows, :].astype(F32) + ob_ref[rows, :].astype(F32)
    heads = []
    for h in range(GLA_HEADS):
        oh = o[:, h * GLA_DV:(h + 1) * GLA_DV]
        heads.append(oh * lax.rsqrt(jnp.mean(oh * oh, axis=-1, keepdims=True) + EPS))
    r = r_ref[rows, :].astype(F32)
    a_out = jnp.concatenate(heads, axis=1) * gn_ref[...] * (r * jax.nn.sigmoid(r))
    u = u_ref[rows, :].astype(F32)
    if rows.start == 0:
        at_seq_start = (tile * tm) % seq == 0
        before = jnp.where(at_seq_start, 0.0, edge_before.astype(F32))
    else:
        before = u_ref[rows.start - 1:rows.start, :].astype(F32)
    if rows.stop == tm:
        at_seq_end = ((tile + 1) * tm) % seq == 0
        after = jnp.where(at_seq_end, 0.0, edge_after.astype(F32))
    else:
        after = u_ref[rows.stop:rows.stop + 1, :].astype(F32)
    row = lax.broadcasted_iota(jnp.int32, u.shape, 0)
    u_prev = jnp.where(row == 0, before, pltpu.roll(u, 1, 0))
    u_next = jnp.where(row == n - 1, after, pltpu.roll(u, n - 1, 0))
    conv = cw_ref[0:1, :] * u_prev + cw_ref[1:2, :] * u + cw_ref[2:3, :] * u_next + cb_ref[...]
    c_out = bg_ref[rows, :].astype(F32) * conv
    return jnp.concatenate([a_out, c_out], axis=1).astype(BF16)


def _cross_attention_stages(x, n_ref, wq_ref, k_ref, v_ref, wo_ref, out):
    d = x.shape[-1]
    hd = d // XATTN_HEADS
    h = _rms(x, n_ref[...]).astype(BF16)
    yield
    q = _dot(h, wq_ref[...])
    yield
    outs = []
    for a in range(XATTN_HEADS):
        sl = slice(a * hd, (a + 1) * hd)
        s = _dot_nt(q[:, sl].astype(BF16), k_ref[0, :, sl]) * (hd ** -0.5)
        p = jnp.exp(s - jnp.max(s, axis=-1, keepdims=True))
        denom = jnp.sum(p, axis=-1, keepdims=True)
        outs.append(_dot(p.astype(BF16), v_ref[0, :, sl]) / denom)
    o = jnp.concatenate(outs, axis=1).astype(BF16)
    yield
    out["y"] = x + _dot(o, wo_ref[...])


def _post_body(*refs, seq, final):
    (x_ref, of_ref, ob_ref, r_ref, bg_ref, u_ref, up_ref, un_ref, gn_ref, cw_ref, cb_ref, wo_hbm,
     xn_ref, wq_hbm, k_ref, v_ref, wxo_hbm, fn_ref, wg_hbm, wu_hbm, wd_hbm) = refs[:21]
    refs = refs[21:]
    if final:
        last_ref, refs = refs[0], refs[1:]
    o_ref, a_ref, wo_ref, wq_ref, wxo_ref, wg_ref, wu_ref, wd_ref, stage_ref, sem_ref = refs
    tm = x_ref.shape[0]
    i = pl.program_id(0)

    @pl.when(i == 0)
    def _():
        _load_weights_bf16(
            [(src, 0, src.shape[0], dst, 0) for src, dst in
             ((wo_hbm, wo_ref), (wq_hbm, wq_ref), (wxo_hbm, wxo_ref), (wg_hbm, wg_ref), (wu_hbm, wu_ref),
              (wd_hbm, wd_ref))],
            stage_ref, sem_ref)

    def stages(rows):
        mixed = _mixed_heads(rows, i, of_ref, ob_ref, r_ref, bg_ref, u_ref, up_ref[BF16_ROWS - 1:BF16_ROWS, :],
                             un_ref[0:1, :], gn_ref, cw_ref, cb_ref, seq)
        yield
        x2 = x_ref[rows, :] + _dot(mixed, wo_ref[...])
        yield
        box = {}
        yield from _cross_attention_stages(x2, xn_ref, wq_ref, k_ref, v_ref, wxo_ref, box)
        yield
        x3 = box["y"]
        yield from _swiglu_stages(x3, rows, fn_ref, wg_ref, wu_ref, wd_ref, a_ref, box)
        y = box["y"]
        if final:
            yield
            y = _rms(y, last_ref[...])
        o_ref[rows, :] = y

    _run_staggered(stages(rows) for rows in _row_slices(tm))


def _post(x, o_f, o_b, r, bg, u, gla_norm, conv_w, conv_b, w_out, xattn_norm, w_q, mk, mv, w_xo,
          ffn_norm, wg, wu, wd, seq, final_norm=None):
    t, d = x.shape
    tm = TOKEN_TILE
    d_ff = wg.shape[1]
    m = mk.shape[1]
    assert seq % tm == 0 and tm % BF16_ROWS == 0 and d_ff % FFN_CHUNK == 0
    final = final_norm is not None
    tile = lambda w: pl.BlockSpec((tm, w), lambda i: (i, 0))
    hb = tm // BF16_ROWS
    n_halo = t // BF16_ROWS
    halo_prev = pl.BlockSpec((BF16_ROWS, CONV_W), lambda i: (jnp.maximum(i * hb - 1, 0), 0))
    halo_next = pl.BlockSpec((BF16_ROWS, CONV_W), lambda i: (jnp.minimum((i + 1) * hb, n_halo - 1), 0))
    kv_blk = pl.BlockSpec((1, m, d), lambda i: (i // (seq // tm), 0, 0))
    in_specs = [tile(d), tile(GLA_VAL_W), tile(GLA_VAL_W), tile(GLA_VAL_W), tile(CONV_W), tile(CONV_W),
                halo_prev, halo_next, _const_spec((1, GLA_VAL_W)), _const_spec(conv_w.shape),
                _const_spec((1, CONV_W)), _ANY_SPEC,
                _const_spec((1, d)), _ANY_SPEC, kv_blk, kv_blk, _ANY_SPEC,
                _const_spec((1, d)), _ANY_SPEC, _ANY_SPEC, _ANY_SPEC]
    args = [x, o_f, o_b, r, bg, u, u, u, gla_norm.reshape(1, -1), conv_w, conv_b.reshape(1, -1), w_out,
            xattn_norm.reshape(1, d), w_q, mk, mv, w_xo, ffn_norm.reshape(1, d), wg, wu, wd]
    if final:
        in_specs.append(_const_spec((1, d)))
        args.append(final_norm.reshape(1, d))
    return pl.pallas_call(
        functools.partial(_post_body, seq=seq, final=final),
        grid=(t // tm,),
        in_specs=in_specs,
        out_specs=tile(d),
        out_shape=jax.ShapeDtypeStruct((t, d), F32),
        scratch_shapes=[pltpu.VMEM((tm, d_ff), BF16),
                        pltpu.VMEM(w_out.shape, BF16), pltpu.VMEM(w_q.shape, BF16), pltpu.VMEM(w_xo.shape, BF16),
                        pltpu.VMEM((d, d_ff), BF16), pltpu.VMEM((d, d_ff), BF16), pltpu.VMEM((d_ff, d), BF16),
                        pltpu.VMEM((STAGE_SLOTS, STAGE_ROWS, max(d_ff, d)), F32),
                        pltpu.SemaphoreType.DMA((STAGE_SLOTS,))],
        compiler_params=_cparams(1),
        name="post_final" if final else "post",
    )(*args)


def _pack_gate(gate_w, gate_b, z_row0):
    w = jnp.pad(gate_w, ((z_row0, LANES - z_row0 - GATE_RANK), (0, 0)))
    return w.astype(BF16), gate_b.reshape(1, -1).astype(F32)


def kernel(x, mem, ffn1_norm, ffn1_w_gate, ffn1_w_up, ffn1_w_down, mix_norm, w_in, gate_fwd_w, gate_fwd_b,
           gate_bwd_w, gate_bwd_b, gla_norm, conv_w, conv_b, w_out, xattn_norm, mem_norm, xattn_w_q,
           xattn_w_kv, xattn_w_o, ffn2_norm, ffn2_w_gate, ffn2_w_up, ffn2_w_down, final_norm):
    batch, seq, d = x.shape
    depth = w_in.shape[0]
    assert seq % GLA_BLOCK == 0 and GLA_BLOCK % CHUNK == 0 and seq % TOKEN_TILE == 0
    h = x.reshape(batch * seq, d)
    for l in range(depth):
        h, q, k, v, r, bg, u, z = _pre(
            h, ffn1_norm[l], ffn1_w_gate[l], ffn1_w_up[l], ffn1_w_down[l], mix_norm[l], w_in[l])
        wgate_f, gbias_f = _pack_gate(gate_fwd_w[l], gate_fwd_b[l], 0)
        wgate_b, gbias_b = _pack_gate(gate_bwd_w[l], gate_bwd_b[l], GATE_RANK)
        o_f, o_b = _gla(q, k, v, z, wgate_f, wgate_b, gbias_f, gbias_b, batch, seq)
        mk, mv = _kv(mem, mem_norm[l], xattn_w_kv[l])
        h = _post(h, o_f, o_b, r, bg, u, gla_norm[l], conv_w[l], conv_b[l], w_out[l],
                  xattn_norm[l], xattn_w_q[l], mk, mv, xattn_w_o[l],
                  ffn2_norm[l], ffn2_w_gate[l], ffn2_w_up[l], ffn2_w_down[l], seq,
                  final_norm=final_norm if l == depth - 1 else None)
    return h.reshape(batch, seq, d)
```

```python
import functools

import jax
import jax.numpy as jnp
from jax import lax
from jax.experimental import pallas as pl
from jax.experimental.pallas import tpu as pltpu

F32 = jnp.float32
BF16 = jnp.bfloat16

XATTN_HEADS = 4
GLA_HEADS = 4
GLA_DK = 64
GLA_DV = 128
GLA_KEY_W = GLA_HEADS * GLA_DK
GLA_VAL_W = GLA_HEADS * GLA_DV
GATE_RANK = 16
GATE_TAU = 16.0
CHUNK = 64
CONV_W = 512
EPS = 1e-6
LOG2_E = 1.4426950408889634

LANES = 128
SUBLANES = 8
BF16_ROWS = 16
TOKEN_TILE = 512
TILE_SPLIT = 2
GLA_BLOCK = 512
GLA_PIPELINE_DEPTH = 2
FFN_CHUNK = 256
STAGE_ROWS = 128
STAGE_SLOTS = 4
VMEM_LIMIT_BYTES = 58 * 1024 * 1024


def _cparams(n_axes):
    return pltpu.CompilerParams(
        dimension_semantics=("arbitrary",) * n_axes,
        vmem_limit_bytes=VMEM_LIMIT_BYTES,
    )


def _const_spec(shape):
    nd = len(shape)
    return pl.BlockSpec(shape, lambda *_: (0,) * nd, pipeline_mode=pl.Buffered(1))


_ANY_SPEC = pl.BlockSpec(memory_space=pl.ANY)


def _load_weights_bf16(jobs, stage_ref, sem_ref):
    n_slots, rows = stage_ref.shape[0], stage_ref.shape[1]
    ahead = n_slots - 1
    counts = [n_rows // rows for _, _, n_rows, _, _ in jobs]
    for (src, _, n_rows, dst, _), n in zip(jobs, counts):
        assert n_rows % rows == 0 and n >= ahead and src.shape[1] == dst.shape[1] <= stage_ref.shape[2]
    bases = [sum(counts[:j]) for j in range(len(jobs))]
    stream = [(j, c) for j, n in enumerate(counts) for c in range(n)]

    def copy(j, c, g):
        src, src_row0, _, _, _ = jobs[j]
        r0 = pl.multiple_of(src_row0 + c * rows, SUBLANES)
        return pltpu.make_async_copy(src.at[pl.ds(r0, rows), :],
                                     stage_ref.at[g % n_slots, :, pl.ds(0, src.shape[1])], sem_ref.at[g % n_slots])

    def finish(j, c, g):
        _, _, _, dst, dst_row0 = jobs[j]
        copy(j, c, g).wait()
        r0 = pl.multiple_of(dst_row0 + c * rows, BF16_ROWS)
        dst[pl.ds(r0, rows), :] = stage_ref[g % n_slots, :, 0:dst.shape[1]].astype(BF16)

    for g in range(ahead):
        copy(*stream[g], g).start()
    for j, n in enumerate(counts):
        def body(c, carry, j=j):
            copy(j, c + ahead, bases[j] + c + ahead).start()
            finish(j, c, bases[j] + c)
            return carry

        lax.fori_loop(0, n - ahead, body, 0)
        for c in range(n - ahead, n):
            g = bases[j] + c
            if g + ahead < len(stream):
                copy(*stream[g + ahead], g + ahead).start()
            finish(j, c, g)


def _convert_weights_across_steps(jobs, step, n_steps, in_sem, out_sem):
    def chunks(job):
        return job[0].shape[0] // job[2].shape[1]

    def copy_in(j, c, slot):
        src, _, stage, _ = jobs[j]
        r = stage.shape[1]
        return pltpu.make_async_copy(src.at[pl.ds(pl.multiple_of(c * r, r), r), :], stage.at[slot],
                                     in_sem.at[j, slot])

    def copy_out(j, c, slot):
        _, dst, stage, bstage = jobs[j]
        r = stage.shape[1]
        return pltpu.make_async_copy(bstage.at[slot], dst.at[pl.ds(pl.multiple_of(c * r, r), r), :],
                                     out_sem.at[j, slot])

    for job in jobs:
        assert job[0].shape[0] % job[2].shape[1] == 0 and 2 <= chunks(job) <= n_steps
    slot = step % 2
    for n in sorted({chunks(job) for job in jobs}):
        group = [j for j, job in enumerate(jobs) if chunks(job) == n]

        @pl.when(step == 0)
        def _():
            for j in group:
                copy_in(j, 0, 0).start()

        @pl.when(step + 1 < n)
        def _():
            for j in group:
                copy_in(j, step + 1, 1 - slot).start()

        @pl.when(jnp.logical_and(step >= 2, step - 2 < n))
        def _():
            for j in group:
                copy_out(j, step - 2, slot).wait()

        @pl.when(step < n)
        def _():
            for j in group:
                _, _, stage, bstage = jobs[j]
                copy_in(j, step, slot).wait()
                bstage[slot] = stage[slot].astype(BF16)
                copy_out(j, step, slot).start()

        late = [c for c in (n_steps - 2, n_steps - 1) if c < n]
        if late:
            @pl.when(step == n_steps - 1)
            def _():
                for j in group:
                    for c in late:
                        copy_out(j, c, c % 2).wait()


def _rms(x, g):
    ms = jnp.mean(x * x, axis=-1, keepdims=True)
    return x * lax.rsqrt(ms + EPS) * g


def _dot(a, b):
    return jnp.dot(a, b, preferred_element_type=F32)


def _dot_nt(a, b):
    return lax.dot_general(a, b, (((1,), (1,)), ((), ())), preferred_element_type=F32)


def _row_slices(tile_rows):
    assert tile_rows % (TILE_SPLIT * BF16_ROWS) == 0
    part = tile_rows // TILE_SPLIT
    return [slice(i * part, (i + 1) * part) for i in range(TILE_SPLIT)]


def _run_staggered(stage_gens, lead=1):
    active = list(stage_gens)
    for _ in range(lead):
        next(active[0])
    while active:
        for gen in list(active):
            try:
                next(gen)
            except StopIteration:
                active.remove(gen)


def _swiglu_stages(x, rows, n_ref, wg_ref, wu_ref, wd_ref, a_ref, out):
    h = _rms(x, n_ref[...]).astype(BF16)
    yield
    for c in range(wg_ref.shape[1] // FFN_CHUNK):
        sl = slice(c * FFN_CHUNK, (c + 1) * FFN_CHUNK)
        g = _dot(h, wg_ref[:, sl])
        u = _dot(h, wu_ref[:, sl])
        a_ref[rows, sl] = (g * jax.nn.sigmoid(g) * u).astype(BF16)
    yield
    out["y"] = x + 0.5 * _dot(a_ref[rows, :], wd_ref[...])


_P_Q, _P_K, _P_V, _P_R = 0, GLA_KEY_W, 2 * GLA_KEY_W, 2 * GLA_KEY_W + GLA_VAL_W
_P_BG = _P_R + GLA_VAL_W
_P_CG = _P_BG + CONV_W
_P_XV = _P_CG + CONV_W
_P_Z = _P_XV + CONV_W
_P_END = _P_Z + LANES


def _pre_body(*refs, n_later, n_steps):
    (x_ref, fn_ref, wg_hbm, wu_hbm, wd_hbm, mn_ref, wint_hbm), refs = refs[:7], refs[7:]
    later_f32, refs = refs[:n_later], refs[n_later:]
    (x1_ref, q_ref, k_ref, v_ref, r_ref, bg_ref, u_ref, z_ref), refs = refs[:8], refs[8:]
    later_bf16, refs = refs[:n_later], refs[n_later:]
    (a_ref, wg_ref, wu_ref, wd_ref, wp_ref, stage_ref, sem_ref), refs = refs[:7], refs[7:]
    later_stage, later_bstage = refs[:n_later], refs[n_later:2 * n_later]
    later_in_sem, later_out_sem = refs[2 * n_later:]

    _convert_weights_across_steps(list(zip(later_f32, later_bf16, later_stage, later_bstage)),
                                  pl.program_id(0), n_steps, later_in_sem, later_out_sem)

    @pl.when(pl.program_id(0) == 0)
    def _():
        d, d_ff = wg_hbm.shape
        z_w = 2 * GATE_RANK
        main = _P_BG
        conv = _P_Z - _P_BG
        _load_weights_bf16(
            [(wg_hbm, 0, d, wg_ref, 0), (wu_hbm, 0, d, wu_ref, 0), (wd_hbm, 0, d_ff, wd_ref, 0),
             (wint_hbm, 0, main, wp_ref, 0), (wint_hbm, main + z_w, conv, wp_ref, _P_BG)],
            stage_ref, sem_ref)
        z_copy = pltpu.make_async_copy(wint_hbm.at[pl.ds(main, z_w), :],
                                       stage_ref.at[0, pl.ds(0, z_w), pl.ds(0, d)], sem_ref.at[0])
        z_copy.start()
        z_copy.wait()
        wp_ref[_P_Z:_P_Z + z_w, :] = stage_ref[0, 0:z_w, 0:d].astype(BF16)
        wp_ref[_P_Z + z_w:_P_END, :] = jnp.zeros((LANES - z_w, d), BF16)

    def stages(rows):
        ffn = {}
        yield from _swiglu_stages(x_ref[rows, :], rows, fn_ref, wg_ref, wu_ref, wd_ref, a_ref, ffn)
        x1 = ffn["y"]
        x1_ref[rows, :] = x1
        yield
        h = _rms(x1, mn_ref[...]).astype(BF16)
        yield
        p = _dot_nt(h, wp_ref[...])
        q_ref[rows, :] = p[:, _P_Q:_P_K] * (GLA_DK ** -0.5)
        k_ref[rows, :] = p[:, _P_K:_P_V]
        v_ref[rows, :] = p[:, _P_V:_P_R].astype(BF16)
        r_ref[rows, :] = p[:, _P_R:_P_BG].astype(BF16)
        bg_ref[rows, :] = p[:, _P_BG:_P_CG].astype(BF16)
        u_ref[rows, :] = (p[:, _P_CG:_P_XV] * p[:, _P_XV:_P_Z]).astype(BF16)
        z_ref[rows, :] = p[:, _P_Z:_P_END].astype(BF16)

    _run_staggered(stages(rows) for rows in _row_slices(x_ref.shape[0]))


def _rows_per_step(n_rows, n_steps):
    r = -(-n_rows // n_steps)
    r += -r % BF16_ROWS
    while n_rows % r:
        r += BF16_ROWS
    return r


def _pre(x, ffn_norm, wg, wu, wd, mix_norm, w_in, later_weights):
    t, d = x.shape
    d_ff = wg.shape[1]
    assert t % TOKEN_TILE == 0 and d_ff % FFN_CHUNK == 0 and TOKEN_TILE % CHUNK == 0
    n_steps = t // TOKEN_TILE
    tile = lambda w: pl.BlockSpec((TOKEN_TILE, w), lambda i: (i, 0))
    out_w = [(d, F32), (GLA_KEY_W, F32), (GLA_KEY_W, F32), (GLA_VAL_W, BF16), (GLA_VAL_W, BF16),
             (CONV_W, BF16), (CONV_W, BF16), (LANES, BF16)]
    assert w_in.shape[1] == _P_Z + 2 * GATE_RANK
    n_later = len(later_weights)
    later_stage = [(2, _rows_per_step(w.shape[0], n_steps), w.shape[1]) for w in later_weights]
    outs = pl.pallas_call(
        functools.partial(_pre_body, n_later=n_later, n_steps=n_steps),
        grid=(n_steps,),
        in_specs=[tile(d), _const_spec((1, d)), _ANY_SPEC, _ANY_SPEC, _ANY_SPEC, _const_spec((1, d)), _ANY_SPEC]
        + [_ANY_SPEC] * n_later,
        out_specs=[tile(w) for w, _ in out_w] + [_ANY_SPEC] * n_later,
        out_shape=[jax.ShapeDtypeStruct((t, w), dt) for w, dt in out_w]
        + [jax.ShapeDtypeStruct(w.shape, BF16) for w in later_weights],
        scratch_shapes=[pltpu.VMEM((TOKEN_TILE, d_ff), BF16),
                        pltpu.VMEM((d, d_ff), BF16), pltpu.VMEM((d, d_ff), BF16), pltpu.VMEM((d_ff, d), BF16),
                        pltpu.VMEM((_P_END, d), BF16),
                        pltpu.VMEM((STAGE_SLOTS, STAGE_ROWS, max(d_ff, d)), F32),
                        pltpu.SemaphoreType.DMA((STAGE_SLOTS,))]
        + [pltpu.VMEM(s, F32) for s in later_stage] + [pltpu.VMEM(s, BF16) for s in later_stage]
        + [pltpu.SemaphoreType.DMA((n_later, 2)), pltpu.SemaphoreType.DMA((n_later, 2))],
        compiler_params=_cparams(1),
        name="pre",
    )(x, ffn_norm.reshape(1, d), wg, wu, wd, mix_norm.reshape(1, d), w_in.T, *later_weights)
    return outs[:len(out_w)], outs[len(out_w):]


def _chunk_cumsum(x, reverse):
    n, w = x.shape
    groups = n // SUBLANES
    x3 = x.reshape(groups, SUBLANES, w)
    pos = lax.broadcasted_iota(jnp.int32, x3.shape, 1)
    step = 1
    while step < SUBLANES:
        if reverse:
            x3 = x3 + jnp.where(pos < SUBLANES - step, pltpu.roll(x3, SUBLANES - step, 1), 0.0)
        else:
            x3 = x3 + jnp.where(pos >= step, pltpu.roll(x3, step, 1), 0.0)
        step *= 2
    per = CHUNK // SUBLANES
    x4 = x3.reshape(n // CHUNK, per, SUBLANES, w)
    edge = x4[:, :, 0:1, :] if reverse else x4[:, :, SUBLANES - 1:SUBLANES, :]
    outs = [None] * per
    carry = None
    for j in (range(per - 1, -1, -1) if reverse else range(per)):
        outs[j] = x4[:, j] if carry is None else x4[:, j] + carry
        carry = edge[:, j] if carry is None else carry + edge[:, j]
    return jnp.stack(outs, axis=1).reshape(n, w)


def _gla_chunks(q_ref, k_ref, v_ref, z_ref, wgate_ref, gbias_ref, o_ref, s_ref, *, reverse):
    n_chunks = q_ref.shape[0] // CHUNK
    kw = GLA_KEY_W
    assert GLA_DK == CHUNK and 2 * CHUNK == LANES == GLA_DV and CHUNK & (CHUNK - 1) == 0
    shift = CHUNK.bit_length() - 1
    gz = _dot(z_ref[...], wgate_ref[...]) + gbias_ref[...]

    row_head = lax.broadcasted_iota(jnp.int32, (kw, kw), 0) >> shift
    lane_head = lax.broadcasted_iota(jnp.int32, (kw, kw), 1) >> shift
    same_head = row_head.astype(F32).astype(BF16) == lane_head.astype(F32).astype(BF16)
    pos_q = lax.broadcasted_iota(jnp.int32, (CHUNK, kw), 0)
    pos_k = lax.broadcasted_iota(jnp.int32, (CHUNK, kw), 1) & (CHUNK - 1)
    causal = (pos_k >= pos_q) if reverse else (pos_k <= pos_q)
    low_half = lax.broadcasted_iota(jnp.int32, (CHUNK, LANES), 1) < CHUNK
    low_feat = lax.broadcasted_iota(jnp.int32, (kw, LANES), 1) < CHUNK

    def chunk(c):
        rows = slice(c * CHUNK, (c + 1) * CHUNK)
        g = gz[rows, :]
        log2_gate = (jnp.minimum(g, 0.0) - jnp.log(1.0 + jnp.exp2(jnp.abs(g) * -LOG2_E))) * (LOG2_E / GATE_TAU)
        bc = _chunk_cumsum(log2_gate, reverse)
        q = q_ref[rows, :]
        k = k_ref[rows, :]
        v = v_ref[rows, :]
        b_last = bc[0:1, :] if reverse else bc[CHUNK - 1:CHUNK, :]
        q_in = q * jnp.exp2(bc)
        k_in = (k * jnp.exp2(-bc)).astype(BF16)
        k_st = k * jnp.exp2(b_last - bc)
        k_blk = jnp.where(same_head, jnp.concatenate([k_in] * GLA_HEADS, axis=0), jnp.zeros((), BF16))
        att = _dot_nt(q_in.astype(BF16), k_blk)
        att = jnp.where(causal, att, 0.0)
        decay_rows = jnp.broadcast_to(jnp.exp2(b_last), (SUBLANES, kw))
        pad = jnp.zeros((LANES - CHUNK - SUBLANES, kw), F32)
        xt = jnp.concatenate([k_st, decay_rows, pad], axis=0).T
        kst_t = jnp.where(low_feat, xt, 0.0).astype(BF16)
        yield
        outs = []
        for pair in range(GLA_HEADS // 2):
            lanes = slice(pair * LANES, (pair + 1) * LANES)
            a_col, q_col = att[:, lanes], q_in[:, lanes]
            a_swapped = pltpu.roll(a_col, CHUNK, 1)
            q_swapped = pltpu.roll(q_col, CHUNK, 1)
            top = jnp.concatenate([jnp.where(low_half, a_col, q_swapped),
                                   jnp.where(low_half, a_swapped, q_col)], axis=1).astype(BF16)
            heads = (2 * pair, 2 * pair + 1)
            bottom = jnp.concatenate([kst_t[h * GLA_DK:(h + 1) * GLA_DK, :] for h in heads], axis=1)
            s_in = [s_ref[h] for h in heads]
            zero = jnp.zeros((LANES, GLA_DV), BF16)
            vs = [jnp.concatenate([v[:, h * GLA_DV:(h + 1) * GLA_DV], s.astype(BF16)], axis=0)
                  for h, s in zip(heads, s_in)]
            rhs = jnp.concatenate([jnp.concatenate([vs[0], zero], axis=1),
                                   jnp.concatenate([zero, vs[1]], axis=1)], axis=0)
            res = _dot(jnp.concatenate([top, bottom], axis=0), rhs)
            outs.append(res[:CHUNK, :])
            for i, (h, s) in enumerate(zip(heads, s_in)):
                decay = xt[h * GLA_DK:(h + 1) * GLA_DK, CHUNK:CHUNK + 1]
                s_ref[h] = decay * s + res[CHUNK:, i * GLA_DV:(i + 1) * GLA_DV]
        o_ref[rows, :] = jnp.concatenate(outs, axis=1).astype(o_ref.dtype)

    return [chunk(c) for c in (range(n_chunks - 1, -1, -1) if reverse else range(n_chunks))]


def _run_pipelined(two_stage_gens, depth):
    started = []
    for gen in two_stage_gens:
        next(gen)
        started.append(gen)
        if len(started) > depth:
            for _ in started.pop(0):
                pass
    for gen in started:
        for _ in gen:
            pass


def _gla_body(qf_ref, kf_ref, vf_ref, zf_ref, qb_ref, kb_ref, vb_ref, zb_ref, wgf_ref, wgb_ref, gbf_ref, gbb_ref,
              of_ref, ob_ref, sf_ref, sb_ref):
    @pl.when(pl.program_id(1) == 0)
    def _():
        sf_ref[...] = jnp.zeros_like(sf_ref)
        sb_ref[...] = jnp.zeros_like(sb_ref)

    fwd = _gla_chunks(qf_ref, kf_ref, vf_ref, zf_ref, wgf_ref, gbf_ref, of_ref, sf_ref, reverse=False)
    bwd = _gla_chunks(qb_ref, kb_ref, vb_ref, zb_ref, wgb_ref, gbb_ref, ob_ref, sb_ref, reverse=True)
    _run_pipelined([gen for pair in zip(fwd, bwd) for gen in pair], GLA_PIPELINE_DEPTH)


def _gla(q, k, v, z, wgate_f, wgate_b, gbias_f, gbias_b, batch, seq):
    nb = seq // GLA_BLOCK
    fwd = lambda w: pl.BlockSpec((GLA_BLOCK, w), lambda b, j: (b * nb + j, 0))
    bwd = lambda w: pl.BlockSpec((GLA_BLOCK, w), lambda b, j: (b * nb + nb - 1 - j, 0))
    t = batch * seq
    return pl.pallas_call(
        _gla_body,
        grid=(batch, nb),
        in_specs=[fwd(GLA_KEY_W), fwd(GLA_KEY_W), fwd(GLA_VAL_W), fwd(LANES),
                  bwd(GLA_KEY_W), bwd(GLA_KEY_W), bwd(GLA_VAL_W), bwd(LANES),
                  _const_spec(wgate_f.shape), _const_spec(wgate_b.shape),
                  _const_spec(gbias_f.shape), _const_spec(gbias_b.shape)],
        out_specs=[fwd(GLA_VAL_W), bwd(GLA_VAL_W)],
        out_shape=[jax.ShapeDtypeStruct((t, GLA_VAL_W), BF16)] * 2,
        scratch_shapes=[pltpu.VMEM((GLA_HEADS, GLA_DK, GLA_DV), F32)] * 2,
        compiler_params=_cparams(2),
        name="gla",
    )(q, k, v, z, q, k, v, z, wgate_f, wgate_b, gbias_f, gbias_b)


def _kv_body(m_ref, n_ref, w_ref, k_ref, v_ref):
    d = m_ref.shape[-1]
    h = _rms(m_ref[...], n_ref[...]).astype(BF16)
    kv = _dot(h, w_ref[...].astype(BF16))
    k_ref[...] = kv[:, :d].astype(BF16)
    v_ref[...] = kv[:, d:].astype(BF16)


def _kv(mem, norm, w_kv):
    b, m, d = mem.shape
    k, v = pl.pallas_call(
        _kv_body,
        grid=(1,),
        in_specs=[_const_spec((b * m, d)), _const_spec((1, d)), _const_spec(w_kv.shape)],
        out_specs=[_const_spec((b * m, d))] * 2,
        out_shape=[jax.ShapeDtypeStruct((b * m, d), BF16)] * 2,
        compiler_params=_cparams(1),
        name="kv",
    )(mem.reshape(b * m, d), norm.reshape(1, d), w_kv)
    return k.reshape(b, m, d), v.reshape(b, m, d)


def _mixed_heads(rows, tile, of_ref, ob_ref, r_ref, bg_ref, u_ref, edge_before, edge_after, gn_ref, cw_ref, cb_ref,
                 seq):
    tm = u_ref.shape[0]
    n = rows.stop - rows.start
    o = of_ref[rows, :].astype(F32) + ob_ref[rows, :].astype(F32)
    heads = []
    for h in range(GLA_HEADS):
        oh = o[:, h * GLA_DV:(h + 1) * GLA_DV]
        heads.append(oh * lax.rsqrt(jnp.mean(oh * oh, axis=-1, keepdims=True) + EPS))
    r = r_ref[rows, :].astype(F32)
    a_out = jnp.concatenate(heads, axis=1) * gn_ref[...] * (r * jax.nn.sigmoid(r))
    u = u_ref[rows, :].astype(F32)
    if rows.start == 0:
        at_seq_start = (tile * tm) % seq == 0
        before = jnp.where(at_seq_start, 0.0, edge_before.astype(F32))
    else:
        before = u_ref[rows.start - 1:rows.start, :].astype(F32)
    if rows.stop == tm:
        at_seq_end = ((tile + 1) * tm) % seq == 0
        after = jnp.where(at_seq_end, 0.0, edge_after.astype(F32))
    else:
        after = u_ref[rows.stop:rows.stop + 1, :].astype(F32)
    row = lax.broadcasted_iota(jnp.int32, u.shape, 0)
    u_prev = jnp.where(row == 0, before, pltpu.roll(u, 1, 0))
    u_next = jnp.where(row == n - 1, after, pltpu.roll(u, n - 1, 0))
    conv = cw_ref[0:1, :] * u_prev + cw_ref[1:2, :] * u + cw_ref[2:3, :] * u_next + cb_ref[...]
    c_out = bg_ref[rows, :].astype(F32) * conv
    return jnp.concatenate([a_out, c_out], axis=1).astype(BF16)


def _cross_attention_stages(x, n_ref, wq_ref, k_ref, v_ref, wo_ref, out):
    d = x.shape[-1]
    hd = d // XATTN_HEADS
    h = _rms(x, n_ref[...]).astype(BF16)
    yield
    q = _dot(h, wq_ref[...])
    yield
    outs = []
    for a in range(XATTN_HEADS):
        sl = slice(a * hd, (a + 1) * hd)
        s = _dot_nt(q[:, sl].astype(BF16), k_ref[0, :, sl]) * (hd ** -0.5)
        p = jnp.exp(s - jnp.max(s, axis=-1, keepdims=True))
        denom = jnp.sum(p, axis=-1, keepdims=True)
        outs.append(_dot(p.astype(BF16), v_ref[0, :, sl]) / denom)
    o = jnp.concatenate(outs, axis=1).astype(BF16)
    yield
    out["y"] = x + _dot(o, wo_ref[...])


def _post_body(*refs, seq, final):
    (x_ref, of_ref, ob_ref, r_ref, bg_ref, u_ref, up_ref, un_ref, gn_ref, cw_ref, cb_ref, wo_ref,
     xn_ref, wq_ref, k_ref, v_ref, wxo_ref, fn_ref, wg_ref, wu_ref, wd_ref) = refs[:21]
    refs = refs[21:]
    if final:
        last_ref, refs = refs[0], refs[1:]
    o_ref, a_ref = refs
    tm = x_ref.shape[0]
    i = pl.program_id(0)

    def stages(rows):
        mixed = _mixed_heads(rows, i, of_ref, ob_ref, r_ref, bg_ref, u_ref, up_ref[BF16_ROWS - 1:BF16_ROWS, :],
                             un_ref[0:1, :], gn_ref, cw_ref, cb_ref, seq)
        yield
        x2 = x_ref[rows, :] + _dot(mixed, wo_ref[...])
        yield
        box = {}
        yield from _cross_attention_stages(x2, xn_ref, wq_ref, k_ref, v_ref, wxo_ref, box)
        yield
        x3 = box["y"]
        yield from _swiglu_stages(x3, rows, fn_ref, wg_ref, wu_ref, wd_ref, a_ref, box)
        y = box["y"]
        if final:
            yield
            y = _rms(y, last_ref[...])
        o_ref[rows, :] = y

    _run_staggered(stages(rows) for rows in _row_slices(tm))


def _post(x, o_f, o_b, r, bg, u, gla_norm, conv_w, conv_b, w_out, xattn_norm, w_q, mk, mv, w_xo,
          ffn_norm, wg, wu, wd, seq, final_norm=None):
    t, d = x.shape
    tm = TOKEN_TILE
    d_ff = wg.shape[1]
    m = mk.shape[1]
    assert seq % tm == 0 and tm % BF16_ROWS == 0 and d_ff % FFN_CHUNK == 0
    final = final_norm is not None
    tile = lambda w: pl.BlockSpec((tm, w), lambda i: (i, 0))
    hb = tm // BF16_ROWS
    n_halo = t // BF16_ROWS
    halo_prev = pl.BlockSpec((BF16_ROWS, CONV_W), lambda i: (jnp.maximum(i * hb - 1, 0), 0))
    halo_next = pl.BlockSpec((BF16_ROWS, CONV_W), lambda i: (jnp.minimum((i + 1) * hb, n_halo - 1), 0))
    kv_blk = pl.BlockSpec((1, m, d), lambda i: (i // (seq // tm), 0, 0))
    for w in (w_out, w_q, w_xo, wg, wu, wd):
        assert w.dtype == BF16
    in_specs = [tile(d), tile(GLA_VAL_W), tile(GLA_VAL_W), tile(GLA_VAL_W), tile(CONV_W), tile(CONV_W),
                halo_prev, halo_next, _const_spec((1, GLA_VAL_W)), _const_spec(conv_w.shape),
                _const_spec((1, CONV_W)), _const_spec(w_out.shape),
                _const_spec((1, d)), _const_spec(w_q.shape), kv_blk, kv_blk, _const_spec(w_xo.shape),
                _const_spec((1, d)), _const_spec(wg.shape), _const_spec(wu.shape), _const_spec(wd.shape)]
    args = [x, o_f, o_b, r, bg, u, u, u, gla_norm.reshape(1, -1), conv_w, conv_b.reshape(1, -1), w_out,
            xattn_norm.reshape(1, d), w_q, mk, mv, w_xo, ffn_norm.reshape(1, d), wg, wu, wd]
    if final:
        in_specs.append(_const_spec((1, d)))
        args.append(final_norm.reshape(1, d))
    return pl.pallas_call(
        functools.partial(_post_body, seq=seq, final=final),
        grid=(t // tm,),
        in_specs=in_specs,
        out_specs=tile(d),
        out_shape=jax.ShapeDtypeStruct((t, d), F32),
        scratch_shapes=[pltpu.VMEM((tm, d_ff), BF16)],
        compiler_params=_cparams(1),
        name="post_final" if final else "post",
    )(*args)


def _pack_gate(gate_w, gate_b, z_row0):
    w = jnp.pad(gate_w, ((z_row0, LANES - z_row0 - GATE_RANK), (0, 0)))
    return w.astype(BF16), gate_b.reshape(1, -1).astype(F32)


def kernel(x, mem, ffn1_norm, ffn1_w_gate, ffn1_w_up, ffn1_w_down, mix_norm, w_in, gate_fwd_w, gate_fwd_b,
           gate_bwd_w, gate_bwd_b, gla_norm, conv_w, conv_b, w_out, xattn_norm, mem_norm, xattn_w_q,
           xattn_w_kv, xattn_w_o, ffn2_norm, ffn2_w_gate, ffn2_w_up, ffn2_w_down, final_norm):
    batch, seq, d = x.shape
    depth = w_in.shape[0]
    assert seq % GLA_BLOCK == 0 and GLA_BLOCK % CHUNK == 0 and seq % TOKEN_TILE == 0
    h = x.reshape(batch * seq, d)
    for l in range(depth):
        post_weights = [w_out[l], xattn_w_q[l], xattn_w_o[l], ffn2_w_gate[l], ffn2_w_up[l], ffn2_w_down[l]]
        (h, q, k, v, r, bg, u, z), (w_o, w_q, w_xo, wg2, wu2, wd2) = _pre(
            h, ffn1_norm[l], ffn1_w_gate[l], ffn1_w_up[l], ffn1_w_down[l], mix_norm[l], w_in[l], post_weights)
        wgate_f, gbias_f = _pack_gate(gate_fwd_w[l], gate_fwd_b[l], 0)
        wgate_b, gbias_b = _pack_gate(gate_bwd_w[l], gate_bwd_b[l], GATE_RANK)
        o_f, o_b = _gla(q, k, v, z, wgate_f, wgate_b, gbias_f, gbias_b, batch, seq)
        mk, mv = _kv(mem, mem_norm[l], xattn_w_kv[l])
        h = _post(h, o_f, o_b, r, bg, u, gla_norm[l], conv_w[l], conv_b[l], w_o,
                  xattn_norm[l], w_q, mk, mv, w_xo, ffn2_norm[l], wg2, wu2, wd2, seq,
                  final_norm=final_norm if l == depth - 1 else None)
    return h.reshape(batch, seq, d)
```

```python
import functools

import jax
import jax.numpy as jnp
from jax import lax
from jax.experimental import pallas as pl
from jax.experimental.pallas import tpu as pltpu

F32 = jnp.float32
BF16 = jnp.bfloat16

XATTN_HEADS = 4
GLA_HEADS = 4
GLA_DK = 64
GLA_DV = 128
GLA_KEY_W = GLA_HEADS * GLA_DK
GLA_VAL_W = GLA_HEADS * GLA_DV
GATE_RANK = 16
GATE_TAU = 16.0
CHUNK = 64
CONV_W = 512
EPS = 1e-6
LOG2_E = 1.4426950408889634

LANES = 128
SUBLANES = 8
BF16_ROWS = 16
TOKEN_TILE = 512
TILE_SPLIT = 2
GLA_BLOCK = 1024
GLA_PIPELINE_DEPTH = 2
FFN_CHUNK = 256
STAGE_ROWS = 128
STAGE_SLOTS = 6
VMEM_LIMIT_BYTES = 58 * 1024 * 1024


def _cparams(n_axes):
    return pltpu.CompilerParams(
        dimension_semantics=("arbitrary",) * n_axes,
        vmem_limit_bytes=VMEM_LIMIT_BYTES,
    )


def _const_spec(shape):
    nd = len(shape)
    return pl.BlockSpec(shape, lambda *_: (0,) * nd, pipeline_mode=pl.Buffered(1))


_ANY_SPEC = pl.BlockSpec(memory_space=pl.ANY)


def _load_weights_bf16(jobs, stage_ref, sem_ref):
    n_slots, rows = stage_ref.shape[0], stage_ref.shape[1]
    ahead = n_slots - 1
    counts = [n_rows // rows for _, _, n_rows, _, _ in jobs]
    for (src, _, n_rows, dst, _), n in zip(jobs, counts):
        assert n_rows % rows == 0 and n >= ahead and src.shape[1] == dst.shape[1] <= stage_ref.shape[2]
    bases = [sum(counts[:j]) for j in range(len(jobs))]
    stream = [(j, c) for j, n in enumerate(counts) for c in range(n)]

    def copy(j, c, g):
        src, src_row0, _, _, _ = jobs[j]
        r0 = pl.multiple_of(src_row0 + c * rows, SUBLANES)
        return pltpu.make_async_copy(src.at[pl.ds(r0, rows), :],
                                     stage_ref.at[g % n_slots, :, pl.ds(0, src.shape[1])], sem_ref.at[g % n_slots])

    def finish(j, c, g):
        _, _, _, dst, dst_row0 = jobs[j]
        copy(j, c, g).wait()
        r0 = pl.multiple_of(dst_row0 + c * rows, BF16_ROWS)
        dst[pl.ds(r0, rows), :] = stage_ref[g % n_slots, :, 0:dst.shape[1]].astype(BF16)

    for g in range(ahead):
        copy(*stream[g], g).start()
    for j, n in enumerate(counts):
        def body(c, carry, j=j):
            copy(j, c + ahead, bases[j] + c + ahead).start()
            finish(j, c, bases[j] + c)
            return carry

        lax.fori_loop(0, n - ahead, body, 0)
        for c in range(n - ahead, n):
            g = bases[j] + c
            if g + ahead < len(stream):
                copy(*stream[g + ahead], g + ahead).start()
            finish(j, c, g)


def _convert_weights_across_steps(jobs, step, n_steps, in_sem, out_sem):
    def chunks(job):
        return job[0].shape[0] // job[2].shape[1]

    def copy_in(j, c, slot):
        src, _, stage, _ = jobs[j]
        r = stage.shape[1]
        return pltpu.make_async_copy(src.at[pl.ds(pl.multiple_of(c * r, r), r), :], stage.at[slot],
                                     in_sem.at[j, slot])

    def copy_out(j, c, slot):
        _, dst, stage, bstage = jobs[j]
        r = stage.shape[1]
        return pltpu.make_async_copy(bstage.at[slot], dst.at[pl.ds(pl.multiple_of(c * r, r), r), :],
                                     out_sem.at[j, slot])

    for job in jobs:
        assert job[0].shape[0] % job[2].shape[1] == 0 and 2 <= chunks(job) <= n_steps
    slot = step % 2
    for n in sorted({chunks(job) for job in jobs}):
        group = [j for j, job in enumerate(jobs) if chunks(job) == n]

        @pl.when(step == 0)
        def _():
            for j in group:
                copy_in(j, 0, 0).start()

        @pl.when(step + 1 < n)
        def _():
            for j in group:
                copy_in(j, step + 1, 1 - slot).start()

        @pl.when(jnp.logical_and(step >= 2, step - 2 < n))
        def _():
            for j in group:
                copy_out(j, step - 2, slot).wait()

        @pl.when(step < n)
        def _():
            for j in group:
                _, _, stage, bstage = jobs[j]
                copy_in(j, step, slot).wait()
                bstage[slot] = stage[slot].astype(BF16)
                copy_out(j, step, slot).start()

        late = [c for c in (n_steps - 2, n_steps - 1) if c < n]
        if late:
            @pl.when(step == n_steps - 1)
            def _():
                for j in group:
                    for c in late:
                        copy_out(j, c, c % 2).wait()


def _rms(x, g):
    ms = jnp.mean(x * x, axis=-1, keepdims=True)
    return x * lax.rsqrt(ms + EPS) * g


def _dot(a, b):
    return jnp.dot(a, b, preferred_element_type=F32)


def _dot_nt(a, b):
    return lax.dot_general(a, b, (((1,), (1,)), ((), ())), preferred_element_type=F32)


def _row_slices(tile_rows):
    assert tile_rows % (TILE_SPLIT * BF16_ROWS) == 0
    part = tile_rows // TILE_SPLIT
    return [slice(i * part, (i + 1) * part) for i in range(TILE_SPLIT)]


def _run_staggered(stage_gens, lead=1):
    active = list(stage_gens)
    for _ in range(lead):
        next(active[0])
    while active:
        for gen in list(active):
            try:
                next(gen)
            except StopIteration:
                active.remove(gen)


def _swiglu_stages(x, rows, n_ref, wg_ref, wu_ref, wd_ref, a_ref, out):
    h = _rms(x, n_ref[...]).astype(BF16)
    yield
    for c in range(wg_ref.shape[1] // FFN_CHUNK):
        sl = slice(c * FFN_CHUNK, (c + 1) * FFN_CHUNK)
        g = _dot(h, wg_ref[:, sl])
        u = _dot(h, wu_ref[:, sl])
        a_ref[rows, sl] = (g * jax.nn.sigmoid(g) * u).astype(BF16)
    yield
    out["y"] = x + 0.5 * _dot(a_ref[rows, :], wd_ref[...])


_P_Q, _P_K, _P_V, _P_R = 0, GLA_KEY_W, 2 * GLA_KEY_W, 2 * GLA_KEY_W + GLA_VAL_W
_P_BG = _P_R + GLA_VAL_W
_P_CG = _P_BG + CONV_W
_P_XV = _P_CG + CONV_W
_P_Z = _P_XV + CONV_W
_P_END = _P_Z + LANES


def _pre_body(*refs, n_later, n_steps):
    (x_ref, fn_ref, wg_hbm, wu_hbm, wd_hbm, mn_ref, wint_hbm), refs = refs[:7], refs[7:]
    later_f32, refs = refs[:n_later], refs[n_later:]
    (x1_ref, q_ref, k_ref, v_ref, r_ref, bg_ref, u_ref, z_ref), refs = refs[:8], refs[8:]
    later_bf16, refs = refs[:n_later], refs[n_later:]
    (a_ref, wg_ref, wu_ref, wd_ref, wp_ref, stage_ref, sem_ref), refs = refs[:7], refs[7:]
    later_stage, later_bstage = refs[:n_later], refs[n_later:2 * n_later]
    later_in_sem, later_out_sem = refs[2 * n_later:]

    _convert_weights_across_steps(list(zip(later_f32, later_bf16, later_stage, later_bstage)),
                                  pl.program_id(0), n_steps, later_in_sem, later_out_sem)

    @pl.when(pl.program_id(0) == 0)
    def _():
        d, d_ff = wg_hbm.shape
        z_w = 2 * GATE_RANK
        main = _P_BG
        conv = _P_Z - _P_BG
        _load_weights_bf16(
            [(wg_hbm, 0, d, wg_ref, 0), (wu_hbm, 0, d, wu_ref, 0), (wd_hbm, 0, d_ff, wd_ref, 0),
             (wint_hbm, 0, main, wp_ref, 0), (wint_hbm, main + z_w, conv, wp_ref, _P_BG)],
            stage_ref, sem_ref)
        z_copy = pltpu.make_async_copy(wint_hbm.at[pl.ds(main, z_w), :],
                                       stage_ref.at[0, pl.ds(0, z_w), pl.ds(0, d)], sem_ref.at[0])
        z_copy.start()
        z_copy.wait()
        wp_ref[_P_Z:_P_Z + z_w, :] = stage_ref[0, 0:z_w, 0:d].astype(BF16)
        wp_ref[_P_Z + z_w:_P_END, :] = jnp.zeros((LANES - z_w, d), BF16)

    def stages(rows):
        ffn = {}
        yield from _swiglu_stages(x_ref[rows, :], rows, fn_ref, wg_ref, wu_ref, wd_ref, a_ref, ffn)
        x1 = ffn["y"]
        x1_ref[rows, :] = x1
        yield
        h = _rms(x1, mn_ref[...]).astype(BF16)
        yield
        p = _dot_nt(h, wp_ref[...])
        q_ref[rows, :] = p[:, _P_Q:_P_K] * (GLA_DK ** -0.5)
        k_ref[rows, :] = p[:, _P_K:_P_V]
        v_ref[rows, :] = p[:, _P_V:_P_R].astype(BF16)
        r_ref[rows, :] = p[:, _P_R:_P_BG].astype(BF16)
        bg_ref[rows, :] = p[:, _P_BG:_P_CG].astype(BF16)
        u_ref[rows, :] = (p[:, _P_CG:_P_XV] * p[:, _P_XV:_P_Z]).astype(BF16)
        z_ref[rows, :] = p[:, _P_Z:_P_END].astype(BF16)

    _run_staggered(stages(rows) for rows in _row_slices(x_ref.shape[0]))


def _rows_per_step(n_rows, n_steps):
    r = -(-n_rows // n_steps)
    r += -r % BF16_ROWS
    while n_rows % r:
        r += BF16_ROWS
    return r


def _pre(x, ffn_norm, wg, wu, wd, mix_norm, w_in, later_weights):
    t, d = x.shape
    d_ff = wg.shape[1]
    assert t % TOKEN_TILE == 0 and d_ff % FFN_CHUNK == 0 and TOKEN_TILE % CHUNK == 0
    n_steps = t // TOKEN_TILE
    tile = lambda w: pl.BlockSpec((TOKEN_TILE, w), lambda i: (i, 0))
    out_w = [(d, F32), (GLA_KEY_W, F32), (GLA_KEY_W, F32), (GLA_VAL_W, BF16), (GLA_VAL_W, BF16),
             (CONV_W, BF16), (CONV_W, BF16), (LANES, BF16)]
    assert w_in.shape[1] == _P_Z + 2 * GATE_RANK
    n_later = len(later_weights)
    later_stage = [(2, _rows_per_step(w.shape[0], n_steps), w.shape[1]) for w in later_weights]
    outs = pl.pallas_call(
        functools.partial(_pre_body, n_later=n_later, n_steps=n_steps),
        grid=(n_steps,),
        in_specs=[tile(d), _const_spec((1, d)), _ANY_SPEC, _ANY_SPEC, _ANY_SPEC, _const_spec((1, d)), _ANY_SPEC]
        + [_ANY_SPEC] * n_later,
        out_specs=[tile(w) for w, _ in out_w] + [_ANY_SPEC] * n_later,
        out_shape=[jax.ShapeDtypeStruct((t, w), dt) for w, dt in out_w]
        + [jax.ShapeDtypeStruct(w.shape, BF16) for w in later_weights],
        scratch_shapes=[pltpu.VMEM((TOKEN_TILE, d_ff), BF16),
                        pltpu.VMEM((d, d_ff), BF16), pltpu.VMEM((d, d_ff), BF16), pltpu.VMEM((d_ff, d), BF16),
                        pltpu.VMEM((_P_END, d), BF16),
                        pltpu.VMEM((STAGE_SLOTS, STAGE_ROWS, max(d_ff, d)), F32),
                        pltpu.SemaphoreType.DMA((STAGE_SLOTS,))]
        + [pltpu.VMEM(s, F32) for s in later_stage] + [pltpu.VMEM(s, BF16) for s in later_stage]
        + [pltpu.SemaphoreType.DMA((n_later, 2)), pltpu.SemaphoreType.DMA((n_later, 2))],
        compiler_params=_cparams(1),
        name="pre",
    )(x, ffn_norm.reshape(1, d), wg, wu, wd, mix_norm.reshape(1, d), w_in.T, *later_weights)
    return outs[:len(out_w)], outs[len(out_w):]


def _chunk_cumsum(x, reverse):
    n, w = x.shape
    groups = n // SUBLANES
    x3 = x.reshape(groups, SUBLANES, w)
    pos = lax.broadcasted_iota(jnp.int32, x3.shape, 1)
    step = 1
    while step < SUBLANES:
        if reverse:
            x3 = x3 + jnp.where(pos < SUBLANES - step, pltpu.roll(x3, SUBLANES - step, 1), 0.0)
        else:
            x3 = x3 + jnp.where(pos >= step, pltpu.roll(x3, step, 1), 0.0)
        step *= 2
    per = CHUNK // SUBLANES
    x4 = x3.reshape(n // CHUNK, per, SUBLANES, w)
    edge = x4[:, :, 0:1, :] if reverse else x4[:, :, SUBLANES - 1:SUBLANES, :]
    outs = [None] * per
    carry = None
    for j in (range(per - 1, -1, -1) if reverse else range(per)):
        outs[j] = x4[:, j] if carry is None else x4[:, j] + carry
        carry = edge[:, j] if carry is None else carry + edge[:, j]
    return jnp.stack(outs, axis=1).reshape(n, w)


def _gla_chunks(q_ref, k_ref, v_ref, z_ref, wgate_ref, gbias_ref, o_ref, s_ref, *, reverse):
    n_chunks = q_ref.shape[0] // CHUNK
    kw = GLA_KEY_W
    assert GLA_DK == CHUNK and 2 * CHUNK == LANES == GLA_DV and CHUNK & (CHUNK - 1) == 0
    shift = CHUNK.bit_length() - 1
    gz = _dot(z_ref[...], wgate_ref[...]) + gbias_ref[...]

    row_head = lax.broadcasted_iota(jnp.int32, (kw, kw), 0) >> shift
    lane_head = lax.broadcasted_iota(jnp.int32, (kw, kw), 1) >> shift
    same_head = row_head.astype(F32).astype(BF16) == lane_head.astype(F32).astype(BF16)
    pos_q = lax.broadcasted_iota(jnp.int32, (CHUNK, kw), 0)
    pos_k = lax.broadcasted_iota(jnp.int32, (CHUNK, kw), 1) & (CHUNK - 1)
    causal = (pos_k >= pos_q) if reverse else (pos_k <= pos_q)
    low_half = lax.broadcasted_iota(jnp.int32, (CHUNK, LANES), 1) < CHUNK
    low_feat = lax.broadcasted_iota(jnp.int32, (kw, LANES), 1) < CHUNK

    def chunk(c):
        rows = slice(c * CHUNK, (c + 1) * CHUNK)
        g = gz[rows, :]
        log2_gate = (jnp.minimum(g, 0.0) - jnp.log(1.0 + jnp.exp2(jnp.abs(g) * -LOG2_E))) * (LOG2_E / GATE_TAU)
        bc = _chunk_cumsum(log2_gate, reverse)
        q = q_ref[rows, :]
        k = k_ref[rows, :]
        v = v_ref[rows, :]
        b_last = bc[0:1, :] if reverse else bc[CHUNK - 1:CHUNK, :]
        q_in = q * jnp.exp2(bc)
        k_in = (k * jnp.exp2(-bc)).astype(BF16)
        k_st = k * jnp.exp2(b_last - bc)
        k_blk = jnp.where(same_head, jnp.concatenate([k_in] * GLA_HEADS, axis=0), jnp.zeros((), BF16))
        att = _dot_nt(q_in.astype(BF16), k_blk)
        att = jnp.where(causal, att, 0.0)
        decay_rows = jnp.broadcast_to(jnp.exp2(b_last), (SUBLANES, kw))
        pad = jnp.zeros((LANES - CHUNK - SUBLANES, kw), F32)
        xt = jnp.concatenate([k_st, decay_rows, pad], axis=0).T
        kst_t = jnp.where(low_feat, xt, 0.0).astype(BF16)
        yield
        outs = []
        for pair in range(GLA_HEADS // 2):
            lanes = slice(pair * LANES, (pair + 1) * LANES)
            a_col, q_col = att[:, lanes], q_in[:, lanes]
            a_swapped = pltpu.roll(a_col, CHUNK, 1)
            q_swapped = pltpu.roll(q_col, CHUNK, 1)
            top = jnp.concatenate([jnp.where(low_half, a_col, q_swapped),
                                   jnp.where(low_half, a_swapped, q_col)], axis=1).astype(BF16)
            heads = (2 * pair, 2 * pair + 1)
            bottom = jnp.concatenate([kst_t[h * GLA_DK:(h + 1) * GLA_DK, :] for h in heads], axis=1)
            s_in = [s_ref[h] for h in heads]
            zero = jnp.zeros((LANES, GLA_DV), BF16)
            vs = [jnp.concatenate([v[:, h * GLA_DV:(h + 1) * GLA_DV], s.astype(BF16)], axis=0)
                  for h, s in zip(heads, s_in)]
            rhs = jnp.concatenate([jnp.concatenate([vs[0], zero], axis=1),
                                   jnp.concatenate([zero, vs[1]], axis=1)], axis=0)
            res = _dot(jnp.concatenate([top, bottom], axis=0), rhs)
            outs.append(res[:CHUNK, :])
            for i, (h, s) in enumerate(zip(heads, s_in)):
                decay = xt[h * GLA_DK:(h + 1) * GLA_DK, CHUNK:CHUNK + 1]
                s_ref[h] = decay * s + res[CHUNK:, i * GLA_DV:(i + 1) * GLA_DV]
        o_ref[rows, :] = jnp.concatenate(outs, axis=1).astype(o_ref.dtype)

    return [chunk(c) for c in (range(n_chunks - 1, -1, -1) if reverse else range(n_chunks))]


def _run_pipelined(two_stage_gens, depth):
    started = []
    for gen in two_stage_gens:
        next(gen)
        started.append(gen)
        if len(started) > depth:
            for _ in started.pop(0):
                pass
    for gen in started:
        for _ in gen:
            pass


def _gla_body(qf_ref, kf_ref, vf_ref, zf_ref, qb_ref, kb_ref, vb_ref, zb_ref, wgf_ref, wgb_ref, gbf_ref, gbb_ref,
              of_ref, ob_ref, sf_ref, sb_ref):
    @pl.when(pl.program_id(1) == 0)
    def _():
        sf_ref[...] = jnp.zeros_like(sf_ref)
        sb_ref[...] = jnp.zeros_like(sb_ref)

    fwd = _gla_chunks(qf_ref, kf_ref, vf_ref, zf_ref, wgf_ref, gbf_ref, of_ref, sf_ref, reverse=False)
    bwd = _gla_chunks(qb_ref, kb_ref, vb_ref, zb_ref, wgb_ref, gbb_ref, ob_ref, sb_ref, reverse=True)
    _run_pipelined([gen for pair in zip(fwd, bwd) for gen in pair], GLA_PIPELINE_DEPTH)


def _gla(q, k, v, z, wgate_f, wgate_b, gbias_f, gbias_b, batch, seq):
    nb = seq // GLA_BLOCK
    fwd = lambda w: pl.BlockSpec((GLA_BLOCK, w), lambda b, j: (b * nb + j, 0))
    bwd = lambda w: pl.BlockSpec((GLA_BLOCK, w), lambda b, j: (b * nb + nb - 1 - j, 0))
    t = batch * seq
    return pl.pallas_call(
        _gla_body,
        grid=(batch, nb),
        in_specs=[fwd(GLA_KEY_W), fwd(GLA_KEY_W), fwd(GLA_VAL_W), fwd(LANES),
                  bwd(GLA_KEY_W), bwd(GLA_KEY_W), bwd(GLA_VAL_W), bwd(LANES),
                  _const_spec(wgate_f.shape), _const_spec(wgate_b.shape),
                  _const_spec(gbias_f.shape), _const_spec(gbias_b.shape)],
        out_specs=[fwd(GLA_VAL_W), bwd(GLA_VAL_W)],
        out_shape=[jax.ShapeDtypeStruct((t, GLA_VAL_W), BF16)] * 2,
        scratch_shapes=[pltpu.VMEM((GLA_HEADS, GLA_DK, GLA_DV), F32)] * 2,
        compiler_params=_cparams(2),
        name="gla",
    )(q, k, v, z, q, k, v, z, wgate_f, wgate_b, gbias_f, gbias_b)


def _kv_body(m_ref, n_ref, w_ref, k_ref, v_ref):
    d = m_ref.shape[-1]
    h = _rms(m_ref[...], n_ref[...]).astype(BF16)
    kv = _dot(h, w_ref[...].astype(BF16))
    k_ref[...] = kv[:, :d].astype(BF16)
    v_ref[...] = kv[:, d:].astype(BF16)


def _kv(mem, norm, w_kv):
    b, m, d = mem.shape
    k, v = pl.pallas_call(
        _kv_body,
        grid=(1,),
        in_specs=[_const_spec((b * m, d)), _const_spec((1, d)), _const_spec(w_kv.shape)],
        out_specs=[_const_spec((b * m, d))] * 2,
        out_shape=[jax.ShapeDtypeStruct((b * m, d), BF16)] * 2,
        compiler_params=_cparams(1),
        name="kv",
    )(mem.reshape(b * m, d), norm.reshape(1, d), w_kv)
    return k.reshape(b, m, d), v.reshape(b, m, d)


def _mixed_heads(rows, tile, of_ref, ob_ref, r_ref, bg_ref, u_ref, edge_before, edge_after, gn_ref, cw_ref, cb_ref,
                 seq):
    tm = u_ref.shape[0]
    n = rows.stop - rows.start
    o = of_ref[rows, :].astype(F32) + ob_ref[rows, :].astype(F32)
    heads = []
    for h in range(GLA_HEADS):
        oh = o[:, h * GLA_DV:(h + 1) * GLA_DV]
        heads.append(oh * lax.rsqrt(jnp.mean(oh * oh, axis=-1, keepdims=True) + EPS))
    r = r_ref[rows, :].astype(F32)
    a_out = jnp.concatenate(heads, axis=1) * gn_ref[...] * (r * jax.nn.sigmoid(r))
    u = u_ref[rows, :].astype(F32)
    if rows.start == 0:
        at_seq_start = (tile * tm) % seq == 0
        before = jnp.where(at_seq_start, 0.0, edge_before.astype(F32))
    else:
        before = u_ref[rows.start - 1:rows.start, :].astype(F32)
    if rows.stop == tm:
        at_seq_end = ((tile + 1) * tm) % seq == 0
        after = jnp.where(at_seq_end, 0.0, edge_after.astype(F32))
    else:
        after = u_ref[rows.stop:rows.stop + 1, :].astype(F32)
    row = lax.broadcasted_iota(jnp.int32, u.shape, 0)
    u_prev = jnp.where(row == 0, before, pltpu.roll(u, 1, 0))
    u_next = jnp.where(row == n - 1, after, pltpu.roll(u, n - 1, 0))
    conv = cw_ref[0:1, :] * u_prev + cw_ref[1:2, :] * u + cw_ref[2:3, :] * u_next + cb_ref[...]
    c_out = bg_ref[rows, :].astype(F32) * conv
    return jnp.concatenate([a_out, c_out], axis=1).astype(BF16)


def _cross_attention_stages(x, n_ref, wq_ref, k_ref, v_ref, wo_ref, out):
    d = x.shape[-1]
    hd = d // XATTN_HEADS
    h = _rms(x, n_ref[...]).astype(BF16)
    yield
    q = _dot(h, wq_ref[...])
    yield
    outs = []
    for a in range(XATTN_HEADS):
        sl = slice(a * hd, (a + 1) * hd)
        s = _dot_nt(q[:, sl].astype(BF16), k_ref[0, :, sl]) * (hd ** -0.5)
        p = jnp.exp(s - jnp.max(s, axis=-1, keepdims=True))
        denom = jnp.sum(p, axis=-1, keepdims=True)
        outs.append(_dot(p.astype(BF16), v_ref[0, :, sl]) / denom)
    o = jnp.concatenate(outs, axis=1).astype(BF16)
    yield
    out["y"] = x + _dot(o, wo_ref[...])


def _post_body(*refs, seq, final):
    (x_ref, of_ref, ob_ref, r_ref, bg_ref, u_ref, up_ref, un_ref, gn_ref, cw_ref, cb_ref, wo_ref,
     xn_ref, wq_ref, k_ref, v_ref, wxo_ref, fn_ref, wg_ref, wu_ref, wd_ref) = refs[:21]
    refs = refs[21:]
    if final:
        last_ref, refs = refs[0], refs[1:]
    o_ref, a_ref = refs
    tm = x_ref.shape[0]
    i = pl.program_id(0)

    def stages(rows):
        mixed = _mixed_heads(rows, i, of_ref, ob_ref, r_ref, bg_ref, u_ref, up_ref[BF16_ROWS - 1:BF16_ROWS, :],
                             un_ref[0:1, :], gn_ref, cw_ref, cb_ref, seq)
        yield
        x2 = x_ref[rows, :] + _dot(mixed, wo_ref[...])
        yield
        box = {}
        yield from _cross_attention_stages(x2, xn_ref, wq_ref, k_ref, v_ref, wxo_ref, box)
        yield
        x3 = box["y"]
        yield from _swiglu_stages(x3, rows, fn_ref, wg_ref, wu_ref, wd_ref, a_ref, box)
        y = box["y"]
        if final:
            yield
            y = _rms(y, last_ref[...])
        o_ref[rows, :] = y

    _run_staggered(stages(rows) for rows in _row_slices(tm))


def _post(x, o_f, o_b, r, bg, u, gla_norm, conv_w, conv_b, w_out, xattn_norm, w_q, mk, mv, w_xo,
          ffn_norm, wg, wu, wd, seq, final_norm=None):
    t, d = x.shape
    tm = TOKEN_TILE
    d_ff = wg.shape[1]
    m = mk.shape[1]
    assert seq % tm == 0 and tm % BF16_ROWS == 0 and d_ff % FFN_CHUNK == 0
    final = final_norm is not None
    tile = lambda w: pl.BlockSpec((tm, w), lambda i: (i, 0))
    hb = tm // BF16_ROWS
    n_halo = t // BF16_ROWS
    halo_prev = pl.BlockSpec((BF16_ROWS, CONV_W), lambda i: (jnp.maximum(i * hb - 1, 0), 0))
    halo_next = pl.BlockSpec((BF16_ROWS, CONV_W), lambda i: (jnp.minimum((i + 1) * hb, n_halo - 1), 0))
    kv_blk = pl.BlockSpec((1, m, d), lambda i: (i // (seq // tm), 0, 0))
    for w in (w_out, w_q, w_xo, wg, wu, wd):
        assert w.dtype == BF16
    in_specs = [tile(d), tile(GLA_VAL_W), tile(GLA_VAL_W), tile(GLA_VAL_W), tile(CONV_W), tile(CONV_W),
                halo_prev, halo_next, _const_spec((1, GLA_VAL_W)), _const_spec(conv_w.shape),
                _const_spec((1, CONV_W)), _const_spec(w_out.shape),
                _const_spec((1, d)), _const_spec(w_q.shape), kv_blk, kv_blk, _const_spec(w_xo.shape),
                _const_spec((1, d)), _const_spec(wg.shape), _const_spec(wu.shape), _const_spec(wd.shape)]
    args = [x, o_f, o_b, r, bg, u, u, u, gla_norm.reshape(1, -1), conv_w, conv_b.reshape(1, -1), w_out,
            xattn_norm.reshape(1, d), w_q, mk, mv, w_xo, ffn_norm.reshape(1, d), wg, wu, wd]
    if final:
        in_specs.append(_const_spec((1, d)))
        args.append(final_norm.reshape(1, d))
    return pl.pallas_call(
        functools.partial(_post_body, seq=seq, final=final),
        grid=(t // tm,),
        in_specs=in_specs,
        out_specs=tile(d),
        out_shape=jax.ShapeDtypeStruct((t, d), F32),
        scratch_shapes=[pltpu.VMEM((tm, d_ff), BF16)],
        compiler_params=_cparams(1),
        name="post_final" if final else "post",
    )(*args)


def _pack_gate(gate_w, gate_b, z_row0):
    w = jnp.pad(gate_w, ((z_row0, LANES - z_row0 - GATE_RANK), (0, 0)))
    return w.astype(BF16), gate_b.reshape(1, -1).astype(F32)


def kernel(x, mem, ffn1_norm, ffn1_w_gate, ffn1_w_up, ffn1_w_down, mix_norm, w_in, gate_fwd_w, gate_fwd_b,
           gate_bwd_w, gate_bwd_b, gla_norm, conv_w, conv_b, w_out, xattn_norm, mem_norm, xattn_w_q,
           xattn_w_kv, xattn_w_o, ffn2_norm, ffn2_w_gate, ffn2_w_up, ffn2_w_down, final_norm):
    batch, seq, d = x.shape
    depth = w_in.shape[0]
    assert seq % GLA_BLOCK == 0 and GLA_BLOCK % CHUNK == 0 and seq % TOKEN_TILE == 0
    h = x.reshape(batch * seq, d)
    for l in range(depth):
        post_weights = [w_out[l], xattn_w_q[l], xattn_w_o[l], ffn2_w_gate[l], ffn2_w_up[l], ffn2_w_down[l]]
        (h, q, k, v, r, bg, u, z), (w_o, w_q, w_xo, wg2, wu2, wd2) = _pre(
            h, ffn1_norm[l], ffn1_w_gate[l], ffn1_w_up[l], ffn1_w_down[l], mix_norm[l], w_in[l], post_weights)
        wgate_f, gbias_f = _pack_gate(gate_fwd_w[l], gate_fwd_b[l], 0)
        wgate_b, gbias_b = _pack_gate(gate_bwd_w[l], gate_bwd_b[l], GATE_RANK)
        o_f, o_b = _gla(q, k, v, z, wgate_f, wgate_b, gbias_f, gbias_b, batch, seq)
        mk, mv = _kv(mem, mem_norm[l], xattn_w_kv[l])
        h = _post(h, o_f, o_b, r, bg, u, gla_norm[l], conv_w[l], conv_b[l], w_o,
                  xattn_norm[l], w_q, mk, mv, w_xo, ffn2_norm[l], wg2, wu2, wd2, seq,
                  final_norm=final_norm if l == depth - 1 else None)
    return h.reshape(batch, seq, d)
```

```python
import functools

import jax
import jax.numpy as jnp
from jax import lax
from jax.experimental import pallas as pl
from jax.experimental.pallas import tpu as pltpu

F32 = jnp.float32
BF16 = jnp.bfloat16

XATTN_HEADS = 4
GLA_HEADS = 4
GLA_DK = 64
GLA_DV = 128
GLA_KEY_W = GLA_HEADS * GLA_DK
GLA_VAL_W = GLA_HEADS * GLA_DV
GATE_RANK = 16
GATE_TAU = 16.0
CHUNK = 64
CONV_W = 512
EPS = 1e-6
LOG2_E = 1.4426950408889634

LANES = 128
SUBLANES = 8
BF16_ROWS = 16
TOKEN_TILE = 512
TILE_SPLIT = 2
GLA_BLOCK = 1024
GLA_PIPELINE_DEPTH = 2
FFN_CHUNK = 256
STAGE_ROWS = 128
STAGE_SLOTS = 6
VMEM_LIMIT_BYTES = 58 * 1024 * 1024


def _cparams(n_axes):
    return pltpu.CompilerParams(
        dimension_semantics=("arbitrary",) * n_axes,
        vmem_limit_bytes=VMEM_LIMIT_BYTES,
    )


def _const_spec(shape):
    nd = len(shape)
    return pl.BlockSpec(shape, lambda *_: (0,) * nd, pipeline_mode=pl.Buffered(1))


_ANY_SPEC = pl.BlockSpec(memory_space=pl.ANY)


def _load_weights_bf16(jobs, stage_ref, sem_ref):
    n_slots, rows = stage_ref.shape[0], stage_ref.shape[1]
    ahead = n_slots - 1
    counts = [n_rows // rows for _, _, n_rows, _, _ in jobs]
    for (src, _, n_rows, dst, _), n in zip(jobs, counts):
        assert n_rows % rows == 0 and n >= ahead and src.shape[1] == dst.shape[1] <= stage_ref.shape[2]
    bases = [sum(counts[:j]) for j in range(len(jobs))]
    stream = [(j, c) for j, n in enumerate(counts) for c in range(n)]

    def copy(j, c, g):
        src, src_row0, _, _, _ = jobs[j]
        r0 = pl.multiple_of(src_row0 + c * rows, SUBLANES)
        return pltpu.make_async_copy(src.at[pl.ds(r0, rows), :],
                                     stage_ref.at[g % n_slots, :, pl.ds(0, src.shape[1])], sem_ref.at[g % n_slots])

    def finish(j, c, g):
        _, _, _, dst, dst_row0 = jobs[j]
        copy(j, c, g).wait()
        r0 = pl.multiple_of(dst_row0 + c * rows, BF16_ROWS)
        dst[pl.ds(r0, rows), :] = stage_ref[g % n_slots, :, 0:dst.shape[1]].astype(BF16)

    for g in range(ahead):
        copy(*stream[g], g).start()
    for j, n in enumerate(counts):
        def body(c, carry, j=j):
            copy(j, c + ahead, bases[j] + c + ahead).start()
            finish(j, c, bases[j] + c)
            return carry

        lax.fori_loop(0, n - ahead, body, 0)
        for c in range(n - ahead, n):
            g = bases[j] + c
            if g + ahead < len(stream):
                copy(*stream[g + ahead], g + ahead).start()
            finish(j, c, g)


def _convert_weights_across_steps(jobs, step, n_steps, in_sem, out_sem):
    def chunks(job):
        return job[0].shape[0] // job[2].shape[1]

    def copy_in(j, c, slot):
        src, _, stage, _ = jobs[j]
        r = stage.shape[1]
        return pltpu.make_async_copy(src.at[pl.ds(pl.multiple_of(c * r, r), r), :], stage.at[slot],
                                     in_sem.at[j, slot])

    def copy_out(j, c, slot):
        _, dst, stage, bstage = jobs[j]
        r = stage.shape[1]
        return pltpu.make_async_copy(bstage.at[slot], dst.at[pl.ds(pl.multiple_of(c * r, r), r), :],
                                     out_sem.at[j, slot])

    for job in jobs:
        assert job[0].shape[0] % job[2].shape[1] == 0 and 2 <= chunks(job) <= n_steps
    slot = step % 2
    for n in sorted({chunks(job) for job in jobs}):
        group = [j for j, job in enumerate(jobs) if chunks(job) == n]

        @pl.when(step == 0)
        def _():
            for j in group:
                copy_in(j, 0, 0).start()

        @pl.when(step + 1 < n)
        def _():
            for j in group:
                copy_in(j, step + 1, 1 - slot).start()

        @pl.when(jnp.logical_and(step >= 2, step - 2 < n))
        def _():
            for j in group:
                copy_out(j, step - 2, slot).wait()

        @pl.when(step < n)
        def _():
            for j in group:
                _, _, stage, bstage = jobs[j]
                copy_in(j, step, slot).wait()
                bstage[slot] = stage[slot].astype(BF16)
                copy_out(j, step, slot).start()

        late = [c for c in (n_steps - 2, n_steps - 1) if c < n]
        if late:
            @pl.when(step == n_steps - 1)
            def _():
                for j in group:
                    for c in late:
                        copy_out(j, c, c % 2).wait()


def _rms(x, g):
    ms = jnp.mean(x * x, axis=-1, keepdims=True)
    return x * lax.rsqrt(ms + EPS) * g


def _dot(a, b):
    return jnp.dot(a, b, preferred_element_type=F32)


def _dot_nt(a, b):
    return lax.dot_general(a, b, (((1,), (1,)), ((), ())), preferred_element_type=F32)


def _row_slices(tile_rows):
    assert tile_rows % (TILE_SPLIT * BF16_ROWS) == 0
    part = tile_rows // TILE_SPLIT
    return [slice(i * part, (i + 1) * part) for i in range(TILE_SPLIT)]


def _run_staggered(stage_gens):
    active = list(stage_gens)
    next(active[0])
    while active:
        for gen in list(active):
            try:
                next(gen)
            except StopIteration:
                active.remove(gen)


def _swiglu_stages(x, rows, n_ref, wg_ref, wu_ref, wd_ref, a_ref, out):
    h = _rms(x, n_ref[...]).astype(BF16)
    yield
    for c in range(wg_ref.shape[1] // FFN_CHUNK):
        sl = slice(c * FFN_CHUNK, (c + 1) * FFN_CHUNK)
        g = _dot(h, wg_ref[:, sl])
        u = _dot(h, wu_ref[:, sl])
        a_ref[rows, sl] = (g * jax.nn.sigmoid(g) * u).astype(BF16)
    yield
    out["y"] = x + 0.5 * _dot(a_ref[rows, :], wd_ref[...])


_P_Q, _P_K, _P_V, _P_R = 0, GLA_KEY_W, 2 * GLA_KEY_W, 2 * GLA_KEY_W + GLA_VAL_W
_P_BG = _P_R + GLA_VAL_W
_P_CG = _P_BG + CONV_W
_P_XV = _P_CG + CONV_W
_P_Z = _P_XV + CONV_W
_P_END = _P_Z + LANES


def _pre_body(*refs, n_later, n_steps):
    (x_ref, fn_ref, wg_hbm, wu_hbm, wd_hbm, mn_ref, wint_hbm), refs = refs[:7], refs[7:]
    later_f32, refs = refs[:n_later], refs[n_later:]
    (x1_ref, q_ref, k_ref, v_ref, r_ref, bg_ref, u_ref, z_ref), refs = refs[:8], refs[8:]
    later_bf16, refs = refs[:n_later], refs[n_later:]
    (a_ref, wg_ref, wu_ref, wd_ref, wp_ref, stage_ref, sem_ref), refs = refs[:7], refs[7:]
    later_stage, later_bstage = refs[:n_later], refs[n_later:2 * n_later]
    later_in_sem, later_out_sem = refs[2 * n_later:]

    _convert_weights_across_steps(list(zip(later_f32, later_bf16, later_stage, later_bstage)),
                                  pl.program_id(0), n_steps, later_in_sem, later_out_sem)

    @pl.when(pl.program_id(0) == 0)
    def _():
        d, d_ff = wg_hbm.shape
        z_w = 2 * GATE_RANK
        main = _P_BG
        conv = _P_Z - _P_BG
        _load_weights_bf16(
            [(wg_hbm, 0, d, wg_ref, 0), (wu_hbm, 0, d, wu_ref, 0), (wd_hbm, 0, d_ff, wd_ref, 0),
             (wint_hbm, 0, main, wp_ref, 0), (wint_hbm, main + z_w, conv, wp_ref, _P_BG)],
            stage_ref, sem_ref)
        z_copy = pltpu.make_async_copy(wint_hbm.at[pl.ds(main, z_w), :],
                                       stage_ref.at[0, pl.ds(0, z_w), pl.ds(0, d)], sem_ref.at[0])
        z_copy.start()
        z_copy.wait()
        wp_ref[_P_Z:_P_Z + z_w, :] = stage_ref[0, 0:z_w, 0:d].astype(BF16)
        wp_ref[_P_Z + z_w:_P_END, :] = jnp.zeros((LANES - z_w, d), BF16)

    def stages(rows):
        ffn = {}
        yield from _swiglu_stages(x_ref[rows, :], rows, fn_ref, wg_ref, wu_ref, wd_ref, a_ref, ffn)
        x1 = ffn["y"]
        x1_ref[rows, :] = x1
        yield
        h = _rms(x1, mn_ref[...]).astype(BF16)
        yield
        p = _dot_nt(h, wp_ref[...])
        q_ref[rows, :] = p[:, _P_Q:_P_K] * (GLA_DK ** -0.5)
        k_ref[rows, :] = p[:, _P_K:_P_V]
        v_ref[rows, :] = p[:, _P_V:_P_R].astype(BF16)
        r_ref[rows, :] = p[:, _P_R:_P_BG].astype(BF16)
        bg_ref[rows, :] = p[:, _P_BG:_P_CG].astype(BF16)
        u_ref[rows, :] = (p[:, _P_CG:_P_XV] * p[:, _P_XV:_P_Z]).astype(BF16)
        z_ref[rows, :] = p[:, _P_Z:_P_END].astype(BF16)

    _run_staggered(stages(rows) for rows in _row_slices(x_ref.shape[0]))


def _rows_per_step(n_rows, n_steps):
    r = -(-n_rows // n_steps)
    r += -r % BF16_ROWS
    while n_rows % r:
        r += BF16_ROWS
    return r


def _pre(x, ffn_norm, wg, wu, wd, mix_norm, w_in, later_weights):
    t, d = x.shape
    d_ff = wg.shape[1]
    assert t % TOKEN_TILE == 0 and d_ff % FFN_CHUNK == 0 and TOKEN_TILE % CHUNK == 0
    n_steps = t // TOKEN_TILE
    tile = lambda w: pl.BlockSpec((TOKEN_TILE, w), lambda i: (i, 0))
    out_w = [(d, F32), (GLA_KEY_W, F32), (GLA_KEY_W, F32), (GLA_VAL_W, BF16), (GLA_VAL_W, BF16),
             (CONV_W, BF16), (CONV_W, BF16), (LANES, BF16)]
    assert w_in.shape[1] == _P_Z + 2 * GATE_RANK
    n_later = len(later_weights)
    later_stage = [(2, _rows_per_step(w.shape[0], n_steps), w.shape[1]) for w in later_weights]
    outs = pl.pallas_call(
        functools.partial(_pre_body, n_later=n_later, n_steps=n_steps),
        grid=(n_steps,),
        in_specs=[tile(d), _const_spec((1, d)), _ANY_SPEC, _ANY_SPEC, _ANY_SPEC, _const_spec((1, d)), _ANY_SPEC]
        + [_ANY_SPEC] * n_later,
        out_specs=[tile(w) for w, _ in out_w] + [_ANY_SPEC] * n_later,
        out_shape=[jax.ShapeDtypeStruct((t, w), dt) for w, dt in out_w]
        + [jax.ShapeDtypeStruct(w.shape, BF16) for w in later_weights],
        scratch_shapes=[pltpu.VMEM((TOKEN_TILE, d_ff), BF16),
                        pltpu.VMEM((d, d_ff), BF16), pltpu.VMEM((d, d_ff), BF16), pltpu.VMEM((d_ff, d), BF16),
                        pltpu.VMEM((_P_END, d), BF16),
                        pltpu.VMEM((STAGE_SLOTS, STAGE_ROWS, max(d_ff, d)), F32),
                        pltpu.SemaphoreType.DMA((STAGE_SLOTS,))]
        + [pltpu.VMEM(s, F32) for s in later_stage] + [pltpu.VMEM(s, BF16) for s in later_stage]
        + [pltpu.SemaphoreType.DMA((n_later, 2)), pltpu.SemaphoreType.DMA((n_later, 2))],
        compiler_params=_cparams(1),
        name="pre",
    )(x, ffn_norm.reshape(1, d), wg, wu, wd, mix_norm.reshape(1, d), w_in.T, *later_weights)
    return outs[:len(out_w)], outs[len(out_w):]


def _chunk_cumsum(x, reverse):
    n, w = x.shape
    groups = n // SUBLANES
    x3 = x.reshape(groups, SUBLANES, w)
    pos = lax.broadcasted_iota(jnp.int32, x3.shape, 1)
    step = 1
    while step < SUBLANES:
        if reverse:
            x3 = x3 + jnp.where(pos < SUBLANES - step, pltpu.roll(x3, SUBLANES - step, 1), 0.0)
        else:
            x3 = x3 + jnp.where(pos >= step, pltpu.roll(x3, step, 1), 0.0)
        step *= 2
    per = CHUNK // SUBLANES
    x4 = x3.reshape(n // CHUNK, per, SUBLANES, w)
    edge = x4[:, :, 0:1, :] if reverse else x4[:, :, SUBLANES - 1:SUBLANES, :]
    outs = [None] * per
    carry = None
    for j in (range(per - 1, -1, -1) if reverse else range(per)):
        outs[j] = x4[:, j] if carry is None else x4[:, j] + carry
        carry = edge[:, j] if carry is None else carry + edge[:, j]
    return jnp.stack(outs, axis=1).reshape(n, w)


def _gla_chunks(q_ref, k_ref, v_ref, z_ref, wgate_ref, gbias_ref, o_ref, s_ref, *, reverse):
    n_chunks = q_ref.shape[0] // CHUNK
    kw = GLA_KEY_W
    assert GLA_DK == CHUNK and 2 * CHUNK == LANES == GLA_DV and CHUNK & (CHUNK - 1) == 0
    shift = CHUNK.bit_length() - 1
    gz = _dot(z_ref[...], wgate_ref[...]) + gbias_ref[...]

    row_head = lax.broadcasted_iota(jnp.int32, (kw, kw), 0) >> shift
    lane_head = lax.broadcasted_iota(jnp.int32, (kw, kw), 1) >> shift
    same_head = row_head.astype(F32).astype(BF16) == lane_head.astype(F32).astype(BF16)
    pos_q = lax.broadcasted_iota(jnp.int32, (CHUNK, kw), 0)
    pos_k = lax.broadcasted_iota(jnp.int32, (CHUNK, kw), 1) & (CHUNK - 1)
    causal = (pos_k >= pos_q) if reverse else (pos_k <= pos_q)
    low_half = lax.broadcasted_iota(jnp.int32, (CHUNK, LANES), 1) < CHUNK
    low_feat = lax.broadcasted_iota(jnp.int32, (kw, LANES), 1) < CHUNK

    def chunk(c):
        rows = slice(c * CHUNK, (c + 1) * CHUNK)
        g = gz[rows, :]
        log2_gate = (jnp.minimum(g, 0.0) - jnp.log(1.0 + jnp.exp2(jnp.abs(g) * -LOG2_E))) * (LOG2_E / GATE_TAU)
        bc = _chunk_cumsum(log2_gate, reverse)
        q = q_ref[rows, :]
        k = k_ref[rows, :]
        v = v_ref[rows, :]
        b_last = bc[0:1, :] if reverse else bc[CHUNK - 1:CHUNK, :]
        q_in = q * jnp.exp2(bc)
        k_in = (k * jnp.exp2(-bc)).astype(BF16)
        k_st = k * jnp.exp2(b_last - bc)
        k_blk = jnp.where(same_head, jnp.concatenate([k_in] * GLA_HEADS, axis=0), jnp.zeros((), BF16))
        att = _dot_nt(q_in.astype(BF16), k_blk)
        att = jnp.where(causal, att, 0.0)
        decay_rows = jnp.broadcast_to(jnp.exp2(b_last), (SUBLANES, kw))
        pad = jnp.zeros((LANES - CHUNK - SUBLANES, kw), F32)
        xt = jnp.concatenate([k_st, decay_rows, pad], axis=0).T
        kst_t = jnp.where(low_feat, xt, 0.0).astype(BF16)
        yield
        outs = []
        for pair in range(GLA_HEADS // 2):
            lanes = slice(pair * LANES, (pair + 1) * LANES)
            a_col, q_col = att[:, lanes], q_in[:, lanes]
            a_swapped = pltpu.roll(a_col, CHUNK, 1)
            q_swapped = pltpu.roll(q_col, CHUNK, 1)
            top = jnp.concatenate([jnp.where(low_half, a_col, q_swapped),
                                   jnp.where(low_half, a_swapped, q_col)], axis=1).astype(BF16)
            heads = (2 * pair, 2 * pair + 1)
            bottom = jnp.concatenate([kst_t[h * GLA_DK:(h + 1) * GLA_DK, :] for h in heads], axis=1)
            s_in = [s_ref[h] for h in heads]
            zero = jnp.zeros((LANES, GLA_DV), BF16)
            vs = [jnp.concatenate([v[:, h * GLA_DV:(h + 1) * GLA_DV], s.astype(BF16)], axis=0)
                  for h, s in zip(heads, s_in)]
            rhs = jnp.concatenate([jnp.concatenate([vs[0], zero], axis=1),
                                   jnp.concatenate([zero, vs[1]], axis=1)], axis=0)
            res = _dot(jnp.concatenate([top, bottom], axis=0), rhs)
            outs.append(res[:CHUNK, :])
            for i, (h, s) in enumerate(zip(heads, s_in)):
                decay = xt[h * GLA_DK:(h + 1) * GLA_DK, CHUNK:CHUNK + 1]
                s_ref[h] = decay * s + res[CHUNK:, i * GLA_DV:(i + 1) * GLA_DV]
        o_ref[rows, :] = jnp.concatenate(outs, axis=1).astype(o_ref.dtype)

    return [chunk(c) for c in (range(n_chunks - 1, -1, -1) if reverse else range(n_chunks))]


def _run_pipelined(two_stage_gens, depth):
    started = []
    for gen in two_stage_gens:
        next(gen)
        started.append(gen)
        if len(started) > depth:
            for _ in started.pop(0):
                pass
    for gen in started:
        for _ in gen:
            pass


def _gla_body(qf_ref, kf_ref, vf_ref, zf_ref, qb_ref, kb_ref, vb_ref, zb_ref, wgf_ref, wgb_ref, gbf_ref, gbb_ref,
              of_ref, ob_ref, sf_ref, sb_ref):
    @pl.when(pl.program_id(1) == 0)
    def _():
        sf_ref[...] = jnp.zeros_like(sf_ref)
        sb_ref[...] = jnp.zeros_like(sb_ref)

    fwd = _gla_chunks(qf_ref, kf_ref, vf_ref, zf_ref, wgf_ref, gbf_ref, of_ref, sf_ref, reverse=False)
    bwd = _gla_chunks(qb_ref, kb_ref, vb_ref, zb_ref, wgb_ref, gbb_ref, ob_ref, sb_ref, reverse=True)
    _run_pipelined([gen for pair in zip(fwd, bwd) for gen in pair], GLA_PIPELINE_DEPTH)


def _gla(q, k, v, z, wgate_f, wgate_b, gbias_f, gbias_b, batch, seq):
    nb = seq // GLA_BLOCK
    fwd = lambda w: pl.BlockSpec((GLA_BLOCK, w), lambda b, j: (b * nb + j, 0))
    bwd = lambda w: pl.BlockSpec((GLA_BLOCK, w), lambda b, j: (b * nb + nb - 1 - j, 0))
    t = batch * seq
    return pl.pallas_call(
        _gla_body,
        grid=(batch, nb),
        in_specs=[fwd(GLA_KEY_W), fwd(GLA_KEY_W), fwd(GLA_VAL_W), fwd(LANES),
                  bwd(GLA_KEY_W), bwd(GLA_KEY_W), bwd(GLA_VAL_W), bwd(LANES),
                  _const_spec(wgate_f.shape), _const_spec(wgate_b.shape),
                  _const_spec(gbias_f.shape), _const_spec(gbias_b.shape)],
        out_specs=[fwd(GLA_VAL_W), bwd(GLA_VAL_W)],
        out_shape=[jax.ShapeDtypeStruct((t, GLA_VAL_W), BF16)] * 2,
        scratch_shapes=[pltpu.VMEM((GLA_HEADS, GLA_DK, GLA_DV), F32)] * 2,
        compiler_params=_cparams(2),
        name="gla",
    )(q, k, v, z, q, k, v, z, wgate_f, wgate_b, gbias_f, gbias_b)


def _memory_kv(m_ref, n_ref, w_ref, k_ref, v_ref):
    d = m_ref.shape[-1]
    h = _rms(m_ref[...], n_ref[...]).astype(BF16)
    kv = _dot(h, w_ref[...])
    k_ref[...] = kv[:, :d].astype(BF16)
    v_ref[...] = kv[:, d:].astype(BF16)


def _mixed_heads(rows, tile, of_ref, ob_ref, r_ref, bg_ref, u_ref, edge_before, edge_after, gn_ref, cw_ref, cb_ref,
                 seq):
    tm = u_ref.shape[0]
    n = rows.stop - rows.start
    o = of_ref[rows, :].astype(F32) + ob_ref[rows, :].astype(F32)
    heads = []
    for h in range(GLA_HEADS):
        oh = o[:, h * GLA_DV:(h + 1) * GLA_DV]
        heads.append(oh * lax.rsqrt(jnp.mean(oh * oh, axis=-1, keepdims=True) + EPS))
    r = r_ref[rows, :].astype(F32)
    a_out = jnp.concatenate(heads, axis=1) * gn_ref[...] * (r * jax.nn.sigmoid(r))
    u = u_ref[rows, :].astype(F32)
    if rows.start == 0:
        at_seq_start = (tile * tm) % seq == 0
        before = jnp.where(at_seq_start, 0.0, edge_before.astype(F32))
    else:
        before = u_ref[rows.start - 1:rows.start, :].astype(F32)
    if rows.stop == tm:
        at_seq_end = ((tile + 1) * tm) % seq == 0
        after = jnp.where(at_seq_end, 0.0, edge_after.astype(F32))
    else:
        after = u_ref[rows.stop:rows.stop + 1, :].astype(F32)
    row = lax.broadcasted_iota(jnp.int32, u.shape, 0)
    u_prev = jnp.where(row == 0, before, pltpu.roll(u, 1, 0))
    u_next = jnp.where(row == n - 1, after, pltpu.roll(u, n - 1, 0))
    conv = cw_ref[0:1, :] * u_prev + cw_ref[1:2, :] * u + cw_ref[2:3, :] * u_next + cb_ref[...]
    c_out = bg_ref[rows, :].astype(F32) * conv
    return jnp.concatenate([a_out, c_out], axis=1).astype(BF16)


def _cross_attention_stages(x, n_ref, wq_ref, k_ref, v_ref, mem_rows, wo_ref, out):
    d = x.shape[-1]
    hd = d // XATTN_HEADS
    h = _rms(x, n_ref[...]).astype(BF16)
    yield
    q = _dot(h, wq_ref[...])
    yield
    outs = []
    for a in range(XATTN_HEADS):
        sl = slice(a * hd, (a + 1) * hd)
        s = _dot_nt(q[:, sl].astype(BF16), k_ref[mem_rows, sl]) * (hd ** -0.5)
        p = jnp.exp(s - jnp.max(s, axis=-1, keepdims=True))
        denom = jnp.sum(p, axis=-1, keepdims=True)
        outs.append(_dot(p.astype(BF16), v_ref[mem_rows, sl]) / denom)
    o = jnp.concatenate(outs, axis=1).astype(BF16)
    yield
    out["y"] = x + _dot(o, wo_ref[...])


def _post_body(*refs, seq, n_batch, final):
    (x_ref, of_ref, ob_ref, r_ref, bg_ref, u_ref, up_ref, un_ref, gn_ref, cw_ref, cb_ref, wo_ref,
     xn_ref, wq_ref, mem_ref, mn_ref, wkv_ref, wxo_ref, fn_ref, wg_ref, wu_ref, wd_ref) = refs[:22]
    refs = refs[22:]
    if final:
        last_ref, refs = refs[0], refs[1:]
    o_ref, a_ref, k_ref, v_ref = refs
    tm = x_ref.shape[0]
    i = pl.program_id(0)

    @pl.when(i == 0)
    def _():
        _memory_kv(mem_ref, mn_ref, wkv_ref, k_ref, v_ref)

    n_mem = mem_ref.shape[0] // n_batch
    mem_rows = pl.ds(pl.multiple_of((i * tm // seq) * n_mem, n_mem), n_mem)

    def stages(rows):
        mixed = _mixed_heads(rows, i, of_ref, ob_ref, r_ref, bg_ref, u_ref, up_ref[BF16_ROWS - 1:BF16_ROWS, :],
                             un_ref[0:1, :], gn_ref, cw_ref, cb_ref, seq)
        yield
        x2 = x_ref[rows, :] + _dot(mixed, wo_ref[...])
        yield
        box = {}
        yield from _cross_attention_stages(x2, xn_ref, wq_ref, k_ref, v_ref, mem_rows, wxo_ref, box)
        yield
        x3 = box["y"]
        yield from _swiglu_stages(x3, rows, fn_ref, wg_ref, wu_ref, wd_ref, a_ref, box)
        y = box["y"]
        if final:
            yield
            y = _rms(y, last_ref[...])
        o_ref[rows, :] = y

    _run_staggered(stages(rows) for rows in _row_slices(tm))


def _post(x, o_f, o_b, r, bg, u, gla_norm, conv_w, conv_b, w_out, xattn_norm, w_q, mem, mem_norm, w_kv, w_xo,
          ffn_norm, wg, wu, wd, seq, final_norm=None):
    t, d = x.shape
    tm = TOKEN_TILE
    d_ff = wg.shape[1]
    n_batch, n_mem, _ = mem.shape
    assert seq % tm == 0 and tm % BF16_ROWS == 0 and d_ff % FFN_CHUNK == 0 and n_mem % BF16_ROWS == 0
    final = final_norm is not None
    tile = lambda w: pl.BlockSpec((tm, w), lambda i: (i, 0))
    hb = tm // BF16_ROWS
    n_halo = t // BF16_ROWS
    halo_prev = pl.BlockSpec((BF16_ROWS, CONV_W), lambda i: (jnp.maximum(i * hb - 1, 0), 0))
    halo_next = pl.BlockSpec((BF16_ROWS, CONV_W), lambda i: (jnp.minimum((i + 1) * hb, n_halo - 1), 0))
    for w in (w_out, w_q, w_kv, w_xo, wg, wu, wd):
        assert w.dtype == BF16
    mem_rows = n_batch * n_mem
    in_specs = [tile(d), tile(GLA_VAL_W), tile(GLA_VAL_W), tile(GLA_VAL_W), tile(CONV_W), tile(CONV_W),
                halo_prev, halo_next, _const_spec((1, GLA_VAL_W)), _const_spec(conv_w.shape),
                _const_spec((1, CONV_W)), _const_spec(w_out.shape),
                _const_spec((1, d)), _const_spec(w_q.shape), _const_spec((mem_rows, d)), _const_spec((1, d)),
                _const_spec(w_kv.shape), _const_spec(w_xo.shape),
                _const_spec((1, d)), _const_spec(wg.shape), _const_spec(wu.shape), _const_spec(wd.shape)]
    args = [x, o_f, o_b, r, bg, u, u, u, gla_norm.reshape(1, -1), conv_w, conv_b.reshape(1, -1), w_out,
            xattn_norm.reshape(1, d), w_q, mem.reshape(mem_rows, d), mem_norm.reshape(1, d), w_kv, w_xo,
            ffn_norm.reshape(1, d), wg, wu, wd]
    if final:
        in_specs.append(_const_spec((1, d)))
        args.append(final_norm.reshape(1, d))
    return pl.pallas_call(
        functools.partial(_post_body, seq=seq, n_batch=n_batch, final=final),
        grid=(t // tm,),
        in_specs=in_specs,
        out_specs=tile(d),
        out_shape=jax.ShapeDtypeStruct((t, d), F32),
        scratch_shapes=[pltpu.VMEM((tm, d_ff), BF16), pltpu.VMEM((mem_rows, d), BF16),
                        pltpu.VMEM((mem_rows, d), BF16)],
        compiler_params=_cparams(1),
        name="post_final" if final else "post",
    )(*args)


def _pack_gate(gate_w, gate_b, z_row0):
    w = jnp.pad(gate_w, ((z_row0, LANES - z_row0 - GATE_RANK), (0, 0)))
    return w.astype(BF16), gate_b.reshape(1, -1).astype(F32)


def kernel(x, mem, ffn1_norm, ffn1_w_gate, ffn1_w_up, ffn1_w_down, mix_norm, w_in, gate_fwd_w, gate_fwd_b,
           gate_bwd_w, gate_bwd_b, gla_norm, conv_w, conv_b, w_out, xattn_norm, mem_norm, xattn_w_q,
           xattn_w_kv, xattn_w_o, ffn2_norm, ffn2_w_gate, ffn2_w_up, ffn2_w_down, final_norm):
    batch, seq, d = x.shape
    depth = w_in.shape[0]
    assert seq % GLA_BLOCK == 0 and GLA_BLOCK % CHUNK == 0 and seq % TOKEN_TILE == 0
    h = x.reshape(batch * seq, d)
    for l in range(depth):
        post_weights = [w_out[l], xattn_w_q[l], xattn_w_kv[l], xattn_w_o[l],
                        ffn2_w_gate[l], ffn2_w_up[l], ffn2_w_down[l]]
        (h, q, k, v, r, bg, u, z), (w_o, w_q, w_kv, w_xo, wg2, wu2, wd2) = _pre(
            h, ffn1_norm[l], ffn1_w_gate[l], ffn1_w_up[l], ffn1_w_down[l], mix_norm[l], w_in[l], post_weights)
        wgate_f, gbias_f = _pack_gate(gate_fwd_w[l], gate_fwd_b[l], 0)
        wgate_b, gbias_b = _pack_gate(gate_bwd_w[l], gate_bwd_b[l], GATE_RANK)
        o_f, o_b = _gla(q, k, v, z, wgate_f, wgate_b, gbias_f, gbias_b, batch, seq)
        h = _post(h, o_f, o_b, r, bg, u, gla_norm[l], conv_w[l], conv_b[l], w_o,
                  xattn_norm[l], w_q, mem, mem_norm[l], w_kv, w_xo, ffn2_norm[l], wg2, wu2, wd2, seq,
                  final_norm=final_norm if l == depth - 1 else None)
    return h.reshape(batch, seq, d)
```

```python
import functools

import jax
import jax.numpy as jnp
from jax import lax
from jax.experimental import pallas as pl
from jax.experimental.pallas import tpu as pltpu

F32 = jnp.float32
BF16 = jnp.bfloat16

XATTN_HEADS = 4
GLA_HEADS = 4
GLA_DK = 64
GLA_DV = 128
GLA_KEY_W = GLA_HEADS * GLA_DK
GLA_VAL_W = GLA_HEADS * GLA_DV
GATE_RANK = 16
GATE_TAU = 16.0
CHUNK = 64
CONV_W = 512
EPS = 1e-6
LOG2_E = 1.4426950408889634

LANES = 128
SUBLANES = 8
BF16_ROWS = 16
TOKEN_TILE = 512
TILE_SPLIT = 2
GLA_BLOCK = 1024
GLA_PIPELINE_DEPTH = 2
FFN_CHUNK = 256
STAGE_ROWS = 128
STAGE_SLOTS = 6
VMEM_LIMIT_BYTES = 58 * 1024 * 1024


def _cparams(n_axes):
    return pltpu.CompilerParams(
        dimension_semantics=("arbitrary",) * n_axes,
        vmem_limit_bytes=VMEM_LIMIT_BYTES,
    )


def _const_spec(shape):
    nd = len(shape)
    return pl.BlockSpec(shape, lambda *_: (0,) * nd, pipeline_mode=pl.Buffered(1))


_ANY_SPEC = pl.BlockSpec(memory_space=pl.ANY)


def _load_weights_bf16(jobs, stage_ref, sem_ref):
    n_slots, rows = stage_ref.shape[0], stage_ref.shape[1]
    ahead = n_slots - 1
    counts = [n_rows // rows for _, _, n_rows, _, _ in jobs]
    for (src, _, n_rows, dst, _), n in zip(jobs, counts):
        assert n_rows % rows == 0 and n >= ahead and src.shape[1] == dst.shape[1] <= stage_ref.shape[2]
    bases = [sum(counts[:j]) for j in range(len(jobs))]
    stream = [(j, c) for j, n in enumerate(counts) for c in range(n)]

    def copy(j, c, g):
        src, src_row0, _, _, _ = jobs[j]
        r0 = pl.multiple_of(src_row0 + c * rows, SUBLANES)
        return pltpu.make_async_copy(src.at[pl.ds(r0, rows), :],
                                     stage_ref.at[g % n_slots, :, pl.ds(0, src.shape[1])], sem_ref.at[g % n_slots])

    def finish(j, c, g):
        _, _, _, dst, dst_row0 = jobs[j]
        copy(j, c, g).wait()
        r0 = pl.multiple_of(dst_row0 + c * rows, BF16_ROWS)
        dst[pl.ds(r0, rows), :] = stage_ref[g % n_slots, :, 0:dst.shape[1]].astype(BF16)

    for g in range(ahead):
        copy(*stream[g], g).start()
    for j, n in enumerate(counts):
        def body(c, carry, j=j):
            copy(j, c + ahead, bases[j] + c + ahead).start()
            finish(j, c, bases[j] + c)
            return carry

        lax.fori_loop(0, n - ahead, body, 0)
        for c in range(n - ahead, n):
            g = bases[j] + c
            if g + ahead < len(stream):
                copy(*stream[g + ahead], g + ahead).start()
            finish(j, c, g)


def _convert_weights_across_steps(jobs, step, n_steps, in_sem, out_sem):
    def chunks(job):
        return job[0].shape[0] // job[2].shape[1]

    def copy_in(j, c, slot):
        src, _, stage, _ = jobs[j]
        r = stage.shape[1]
        return pltpu.make_async_copy(src.at[pl.ds(pl.multiple_of(c * r, r), r), :], stage.at[slot],
                                     in_sem.at[j, slot])

    def copy_out(j, c, slot):
        _, dst, stage, bstage = jobs[j]
        r = stage.shape[1]
        return pltpu.make_async_copy(bstage.at[slot], dst.at[pl.ds(pl.multiple_of(c * r, r), r), :],
                                     out_sem.at[j, slot])

    for job in jobs:
        assert job[0].shape[0] % job[2].shape[1] == 0 and 2 <= chunks(job) <= n_steps
    slot = step % 2
    for n in sorted({chunks(job) for job in jobs}):
        group = [j for j, job in enumerate(jobs) if chunks(job) == n]

        @pl.when(step == 0)
        def _():
            for j in group:
                copy_in(j, 0, 0).start()

        @pl.when(step + 1 < n)
        def _():
            for j in group:
                copy_in(j, step + 1, 1 - slot).start()

        @pl.when(jnp.logical_and(step >= 2, step - 2 < n))
        def _():
            for j in group:
                copy_out(j, step - 2, slot).wait()

        @pl.when(step < n)
        def _():
            for j in group:
                _, _, stage, bstage = jobs[j]
                copy_in(j, step, slot).wait()
                bstage[slot] = stage[slot].astype(BF16)
                copy_out(j, step, slot).start()

        late = [c for c in (n_steps - 2, n_steps - 1) if c < n]
        if late:
            @pl.when(step == n_steps - 1)
            def _():
                for j in group:
                    for c in late:
                        copy_out(j, c, c % 2).wait()


def _rms(x, g):
    ms = jnp.mean(x * x, axis=-1, keepdims=True)
    return x * lax.rsqrt(ms + EPS) * g


def _dot(a, b):
    return jnp.dot(a, b, preferred_element_type=F32)


def _dot_nt(a, b):
    return lax.dot_general(a, b, (((1,), (1,)), ((), ())), preferred_element_type=F32)


def _row_slices(tile_rows):
    assert tile_rows % (TILE_SPLIT * BF16_ROWS) == 0
    part = tile_rows // TILE_SPLIT
    return [slice(i * part, (i + 1) * part) for i in range(TILE_SPLIT)]


def _run_staggered(stage_gens):
    active = list(stage_gens)
    next(active[0])
    while active:
        for gen in list(active):
            try:
                next(gen)
            except StopIteration:
                active.remove(gen)


def _swiglu_stages(x, rows, n_ref, wg_ref, wu_ref, wd_ref, a_ref, out):
    h = _rms(x, n_ref[...]).astype(BF16)
    yield
    for c in range(wg_ref.shape[1] // FFN_CHUNK):
        sl = slice(c * FFN_CHUNK, (c + 1) * FFN_CHUNK)
        g = _dot(h, wg_ref[:, sl])
        u = _dot(h, wu_ref[:, sl])
        a_ref[rows, sl] = (g * jax.nn.sigmoid(g) * u).astype(BF16)
    yield
    out["y"] = x + 0.5 * _dot(a_ref[rows, :], wd_ref[...])


_P_Q, _P_K, _P_V, _P_R = 0, GLA_KEY_W, 2 * GLA_KEY_W, 2 * GLA_KEY_W + GLA_VAL_W
_P_BG = _P_R + GLA_VAL_W
_P_CG = _P_BG + CONV_W
_P_XV = _P_CG + CONV_W
_P_Z = _P_XV + CONV_W
_P_END = _P_Z + LANES


def _pre_body(*refs, n_later, n_steps):
    (x_ref, fn_ref, wg_hbm, wu_hbm, wd_hbm, mn_ref, wint_hbm, gn_ref), refs = refs[:8], refs[8:]
    later_f32, refs = refs[:n_later], refs[n_later:]
    (x1_ref, q_ref, k_ref, v_ref, gr_ref, bg_ref, u_ref, z_ref), refs = refs[:8], refs[8:]
    later_bf16, refs = refs[:n_later], refs[n_later:]
    (a_ref, wg_ref, wu_ref, wd_ref, wp_ref, stage_ref, sem_ref), refs = refs[:7], refs[7:]
    later_stage, later_bstage = refs[:n_later], refs[n_later:2 * n_later]
    later_in_sem, later_out_sem = refs[2 * n_later:]

    _convert_weights_across_steps(list(zip(later_f32, later_bf16, later_stage, later_bstage)),
                                  pl.program_id(0), n_steps, later_in_sem, later_out_sem)

    @pl.when(pl.program_id(0) == 0)
    def _():
        d, d_ff = wg_hbm.shape
        z_w = 2 * GATE_RANK
        main = _P_BG
        conv = _P_Z - _P_BG
        _load_weights_bf16(
            [(wg_hbm, 0, d, wg_ref, 0), (wu_hbm, 0, d, wu_ref, 0), (wd_hbm, 0, d_ff, wd_ref, 0),
             (wint_hbm, 0, main, wp_ref, 0), (wint_hbm, main + z_w, conv, wp_ref, _P_BG)],
            stage_ref, sem_ref)
        z_copy = pltpu.make_async_copy(wint_hbm.at[pl.ds(main, z_w), :],
                                       stage_ref.at[0, pl.ds(0, z_w), pl.ds(0, d)], sem_ref.at[0])
        z_copy.start()
        z_copy.wait()
        wp_ref[_P_Z:_P_Z + z_w, :] = stage_ref[0, 0:z_w, 0:d].astype(BF16)
        wp_ref[_P_Z + z_w:_P_END, :] = jnp.zeros((LANES - z_w, d), BF16)

    def stages(rows):
        ffn = {}
        yield from _swiglu_stages(x_ref[rows, :], rows, fn_ref, wg_ref, wu_ref, wd_ref, a_ref, ffn)
        x1 = ffn["y"]
        x1_ref[rows, :] = x1
        yield
        h = _rms(x1, mn_ref[...]).astype(BF16)
        yield
        p = _dot_nt(h, wp_ref[...])
        q_ref[rows, :] = p[:, _P_Q:_P_K] * (GLA_DK ** -0.5)
        k_ref[rows, :] = p[:, _P_K:_P_V]
        v_ref[rows, :] = p[:, _P_V:_P_R].astype(BF16)
        r = p[:, _P_R:_P_BG]
        gr_ref[rows, :] = (gn_ref[...] * (r * jax.nn.sigmoid(r))).astype(BF16)
        bg_ref[rows, :] = p[:, _P_BG:_P_CG].astype(BF16)
        u_ref[rows, :] = (p[:, _P_CG:_P_XV] * p[:, _P_XV:_P_Z]).astype(BF16)
        z_ref[rows, :] = p[:, _P_Z:_P_END].astype(BF16)

    _run_staggered(stages(rows) for rows in _row_slices(x_ref.shape[0]))


def _rows_per_step(n_rows, n_steps):
    r = -(-n_rows // n_steps)
    r += -r % BF16_ROWS
    while n_rows % r:
        r += BF16_ROWS
    return r


def _pre(x, ffn_norm, wg, wu, wd, mix_norm, w_in, gla_norm, later_weights):
    t, d = x.shape
    d_ff = wg.shape[1]
    assert t % TOKEN_TILE == 0 and d_ff % FFN_CHUNK == 0 and TOKEN_TILE % CHUNK == 0
    n_steps = t // TOKEN_TILE
    tile = lambda w: pl.BlockSpec((TOKEN_TILE, w), lambda i: (i, 0))
    out_w = [(d, F32), (GLA_KEY_W, F32), (GLA_KEY_W, F32), (GLA_VAL_W, BF16), (GLA_VAL_W, BF16),
             (CONV_W, BF16), (CONV_W, BF16), (LANES, BF16)]
    assert w_in.shape[1] == _P_Z + 2 * GATE_RANK
    n_later = len(later_weights)
    later_stage = [(2, _rows_per_step(w.shape[0], n_steps), w.shape[1]) for w in later_weights]
    outs = pl.pallas_call(
        functools.partial(_pre_body, n_later=n_later, n_steps=n_steps),
        grid=(n_steps,),
        in_specs=[tile(d), _const_spec((1, d)), _ANY_SPEC, _ANY_SPEC, _ANY_SPEC, _const_spec((1, d)), _ANY_SPEC,
                  _const_spec((1, GLA_VAL_W))] + [_ANY_SPEC] * n_later,
        out_specs=[tile(w) for w, _ in out_w] + [_ANY_SPEC] * n_later,
        out_shape=[jax.ShapeDtypeStruct((t, w), dt) for w, dt in out_w]
        + [jax.ShapeDtypeStruct(w.shape, BF16) for w in later_weights],
        scratch_shapes=[pltpu.VMEM((TOKEN_TILE, d_ff), BF16),
                        pltpu.VMEM((d, d_ff), BF16), pltpu.VMEM((d, d_ff), BF16), pltpu.VMEM((d_ff, d), BF16),
                        pltpu.VMEM((_P_END, d), BF16),
                        pltpu.VMEM((STAGE_SLOTS, STAGE_ROWS, max(d_ff, d)), F32),
                        pltpu.SemaphoreType.DMA((STAGE_SLOTS,))]
        + [pltpu.VMEM(s, F32) for s in later_stage] + [pltpu.VMEM(s, BF16) for s in later_stage]
        + [pltpu.SemaphoreType.DMA((n_later, 2)), pltpu.SemaphoreType.DMA((n_later, 2))],
        compiler_params=_cparams(1),
        name="pre",
    )(x, ffn_norm.reshape(1, d), wg, wu, wd, mix_norm.reshape(1, d), w_in.T, gla_norm.reshape(1, -1),
      *later_weights)
    return outs[:len(out_w)], outs[len(out_w):]


def _chunk_cumsum(x, reverse):
    n, w = x.shape
    groups = n // SUBLANES
    x3 = x.reshape(groups, SUBLANES, w)
    pos = lax.broadcasted_iota(jnp.int32, x3.shape, 1)
    step = 1
    while step < SUBLANES:
        if reverse:
            x3 = x3 + jnp.where(pos < SUBLANES - step, pltpu.roll(x3, SUBLANES - step, 1), 0.0)
        else:
            x3 = x3 + jnp.where(pos >= step, pltpu.roll(x3, step, 1), 0.0)
        step *= 2
    per = CHUNK // SUBLANES
    x4 = x3.reshape(n // CHUNK, per, SUBLANES, w)
    edge = x4[:, :, 0:1, :] if reverse else x4[:, :, SUBLANES - 1:SUBLANES, :]
    outs = [None] * per
    carry = None
    for j in (range(per - 1, -1, -1) if reverse else range(per)):
        outs[j] = x4[:, j] if carry is None else x4[:, j] + carry
        carry = edge[:, j] if carry is None else carry + edge[:, j]
    return jnp.stack(outs, axis=1).reshape(n, w)


def _gla_chunks(q_ref, k_ref, v_ref, z_ref, wgate_ref, gbias_ref, o_ref, s_ref, *, reverse):
    n_chunks = q_ref.shape[0] // CHUNK
    kw = GLA_KEY_W
    assert GLA_DK == CHUNK and 2 * CHUNK == LANES == GLA_DV and CHUNK & (CHUNK - 1) == 0
    shift = CHUNK.bit_length() - 1
    gz = _dot(z_ref[...], wgate_ref[...]) + gbias_ref[...]

    row_head = lax.broadcasted_iota(jnp.int32, (kw, kw), 0) >> shift
    lane_head = lax.broadcasted_iota(jnp.int32, (kw, kw), 1) >> shift
    same_head = row_head.astype(F32).astype(BF16) == lane_head.astype(F32).astype(BF16)
    pos_q = lax.broadcasted_iota(jnp.int32, (CHUNK, kw), 0)
    pos_k = lax.broadcasted_iota(jnp.int32, (CHUNK, kw), 1) & (CHUNK - 1)
    causal = (pos_k >= pos_q) if reverse else (pos_k <= pos_q)
    low_half = lax.broadcasted_iota(jnp.int32, (CHUNK, LANES), 1) < CHUNK
    low_feat = lax.broadcasted_iota(jnp.int32, (kw, LANES), 1) < CHUNK

    def chunk(c):
        rows = slice(c * CHUNK, (c + 1) * CHUNK)
        g = gz[rows, :]
        log2_gate = (jnp.minimum(g, 0.0) - jnp.log(1.0 + jnp.exp2(jnp.abs(g) * -LOG2_E))) * (LOG2_E / GATE_TAU)
        bc = _chunk_cumsum(log2_gate, reverse)
        q = q_ref[rows, :]
        k = k_ref[rows, :]
        v = v_ref[rows, :]
        b_last = bc[0:1, :] if reverse else bc[CHUNK - 1:CHUNK, :]
        q_in = q * jnp.exp2(bc)
        k_in = (k * jnp.exp2(-bc)).astype(BF16)
        k_st = k * jnp.exp2(b_last - bc)
        k_blk = jnp.where(same_head, jnp.concatenate([k_in] * GLA_HEADS, axis=0), jnp.zeros((), BF16))
        att = _dot_nt(q_in.astype(BF16), k_blk)
        att = jnp.where(causal, att, 0.0)
        decay_rows = jnp.broadcast_to(jnp.exp2(b_last), (SUBLANES, kw))
        pad = jnp.zeros((LANES - CHUNK - SUBLANES, kw), F32)
        xt = jnp.concatenate([k_st, decay_rows, pad], axis=0).T
        kst_t = jnp.where(low_feat, xt, 0.0).astype(BF16)
        yield
        outs = []
        for pair in range(GLA_HEADS // 2):
            lanes = slice(pair * LANES, (pair + 1) * LANES)
            a_col, q_col = att[:, lanes], q_in[:, lanes]
            a_swapped = pltpu.roll(a_col, CHUNK, 1)
            q_swapped = pltpu.roll(q_col, CHUNK, 1)
            top = jnp.concatenate([jnp.where(low_half, a_col, q_swapped),
                                   jnp.where(low_half, a_swapped, q_col)], axis=1).astype(BF16)
            heads = (2 * pair, 2 * pair + 1)
            bottom = jnp.concatenate([kst_t[h * GLA_DK:(h + 1) * GLA_DK, :] for h in heads], axis=1)
            s_in = [s_ref[h] for h in heads]
            zero = jnp.zeros((LANES, GLA_DV), BF16)
            vs = [jnp.concatenate([v[:, h * GLA_DV:(h + 1) * GLA_DV], s.astype(BF16)], axis=0)
                  for h, s in zip(heads, s_in)]
            rhs = jnp.concatenate([jnp.concatenate([vs[0], zero], axis=1),
                                   jnp.concatenate([zero, vs[1]], axis=1)], axis=0)
            res = _dot(jnp.concatenate([top, bottom], axis=0), rhs)
            outs.append(res[:CHUNK, :])
            for i, (h, s) in enumerate(zip(heads, s_in)):
                decay = xt[h * GLA_DK:(h + 1) * GLA_DK, CHUNK:CHUNK + 1]
                s_ref[h] = decay * s + res[CHUNK:, i * GLA_DV:(i + 1) * GLA_DV]
        o_ref[rows, :] = jnp.concatenate(outs, axis=1).astype(o_ref.dtype)

    return [chunk(c) for c in (range(n_chunks - 1, -1, -1) if reverse else range(n_chunks))]


def _run_pipelined(two_stage_gens, depth):
    started = []
    for gen in two_stage_gens:
        next(gen)
        started.append(gen)
        if len(started) > depth:
            for _ in started.pop(0):
                pass
    for gen in started:
        for _ in gen:
            pass


def _gla_body(qf_ref, kf_ref, vf_ref, zf_ref, qb_ref, kb_ref, vb_ref, zb_ref, wgf_ref, wgb_ref, gbf_ref, gbb_ref,
              of_ref, ob_ref, sf_ref, sb_ref):
    @pl.when(pl.program_id(1) == 0)
    def _():
        sf_ref[...] = jnp.zeros_like(sf_ref)
        sb_ref[...] = jnp.zeros_like(sb_ref)

    fwd = _gla_chunks(qf_ref, kf_ref, vf_ref, zf_ref, wgf_ref, gbf_ref, of_ref, sf_ref, reverse=False)
    bwd = _gla_chunks(qb_ref, kb_ref, vb_ref, zb_ref, wgb_ref, gbb_ref, ob_ref, sb_ref, reverse=True)
    _run_pipelined([gen for pair in zip(fwd, bwd) for gen in pair], GLA_PIPELINE_DEPTH)


def _gla(q, k, v, z, wgate_f, wgate_b, gbias_f, gbias_b, batch, seq):
    nb = seq // GLA_BLOCK
    fwd = lambda w: pl.BlockSpec((GLA_BLOCK, w), lambda b, j: (b * nb + j, 0))
    bwd = lambda w: pl.BlockSpec((GLA_BLOCK, w), lambda b, j: (b * nb + nb - 1 - j, 0))
    t = batch * seq
    return pl.pallas_call(
        _gla_body,
        grid=(batch, nb),
        in_specs=[fwd(GLA_KEY_W), fwd(GLA_KEY_W), fwd(GLA_VAL_W), fwd(LANES),
                  bwd(GLA_KEY_W), bwd(GLA_KEY_W), bwd(GLA_VAL_W), bwd(LANES),
                  _const_spec(wgate_f.shape), _const_spec(wgate_b.shape),
                  _const_spec(gbias_f.shape), _const_spec(gbias_b.shape)],
        out_specs=[fwd(GLA_VAL_W), bwd(GLA_VAL_W)],
        out_shape=[jax.ShapeDtypeStruct((t, GLA_VAL_W), BF16)] * 2,
        scratch_shapes=[pltpu.VMEM((GLA_HEADS, GLA_DK, GLA_DV), F32)] * 2,
        compiler_params=_cparams(2),
        name="gla",
    )(q, k, v, z, q, k, v, z, wgate_f, wgate_b, gbias_f, gbias_b)


def _memory_kv(m_ref, n_ref, w_ref, k_ref, v_ref):
    d = m_ref.shape[-1]
    h = _rms(m_ref[...], n_ref[...]).astype(BF16)
    kv = _dot(h, w_ref[...])
    k_ref[...] = kv[:, :d].astype(BF16)
    v_ref[...] = kv[:, d:].astype(BF16)


def _mixed_heads(rows, tile, of_ref, ob_ref, gr_ref, bg_ref, u_ref, edge_before, edge_after, cw_ref, cb_ref, seq):
    tm = u_ref.shape[0]
    n = rows.stop - rows.start
    o = of_ref[rows, :].astype(F32) + ob_ref[rows, :].astype(F32)
    heads = []
    for h in range(GLA_HEADS):
        oh = o[:, h * GLA_DV:(h + 1) * GLA_DV]
        heads.append(oh * lax.rsqrt(jnp.mean(oh * oh, axis=-1, keepdims=True) + EPS))
    a_out = jnp.concatenate(heads, axis=1) * gr_ref[rows, :].astype(F32)
    u = u_ref[rows, :].astype(F32)
    if rows.start == 0:
        at_seq_start = (tile * tm) % seq == 0
        before = jnp.where(at_seq_start, 0.0, edge_before.astype(F32))
    else:
        before = u_ref[rows.start - 1:rows.start, :].astype(F32)
    if rows.stop == tm:
        at_seq_end = ((tile + 1) * tm) % seq == 0
        after = jnp.where(at_seq_end, 0.0, edge_after.astype(F32))
    else:
        after = u_ref[rows.stop:rows.stop + 1, :].astype(F32)
    row = lax.broadcasted_iota(jnp.int32, (SUBLANES, u.shape[1]), 0)
    down, up = pltpu.roll(u, 1, 0), pltpu.roll(u, n - 1, 0)
    u_prev = jnp.concatenate([jnp.where(row == 0, before, down[:SUBLANES]), down[SUBLANES:]], axis=0)
    u_next = jnp.concatenate([up[:n - SUBLANES], jnp.where(row == SUBLANES - 1, after, up[n - SUBLANES:])], axis=0)
    conv = cw_ref[0:1, :] * u_prev + cw_ref[1:2, :] * u + cw_ref[2:3, :] * u_next + cb_ref[...]
    c_out = bg_ref[rows, :].astype(F32) * conv
    return jnp.concatenate([a_out, c_out], axis=1).astype(BF16)


def _cross_attention_stages(x, n_ref, wq_ref, k_ref, v_ref, mem_rows, wo_ref, out):
    d = x.shape[-1]
    hd = d // XATTN_HEADS
    h = _rms(x, n_ref[...]).astype(BF16)
    yield
    q = _dot(h, wq_ref[...])
    yield
    heads = [slice(a * hd, (a + 1) * hd) for a in range(XATTN_HEADS)]
    scores = [_dot_nt(q[:, sl].astype(BF16), k_ref[mem_rows, sl]) * (hd ** -0.5) for sl in heads]
    probs = [jnp.exp(s - jnp.max(s, axis=-1, keepdims=True)) for s in scores]
    outs = [_dot(p.astype(BF16), v_ref[mem_rows, sl]) / jnp.sum(p, axis=-1, keepdims=True)
            for p, sl in zip(probs, heads)]
    o = jnp.concatenate(outs, axis=1).astype(BF16)
    yield
    out["y"] = x + _dot(o, wo_ref[...])


def _post_body(*refs, seq, n_batch, final):
    (x_ref, of_ref, ob_ref, gr_ref, bg_ref, u_ref, up_ref, un_ref, cw_ref, cb_ref, wo_ref,
     xn_ref, wq_ref, mem_ref, mn_ref, wkv_ref, wxo_ref, fn_ref, wg_ref, wu_ref, wd_ref) = refs[:21]
    refs = refs[21:]
    if final:
        last_ref, refs = refs[0], refs[1:]
    o_ref, a_ref, k_ref, v_ref = refs
    tm = x_ref.shape[0]
    i = pl.program_id(0)

    @pl.when(i == 0)
    def _():
        _memory_kv(mem_ref, mn_ref, wkv_ref, k_ref, v_ref)

    n_mem = mem_ref.shape[0] // n_batch
    mem_rows = pl.ds(pl.multiple_of((i * tm // seq) * n_mem, n_mem), n_mem)

    def stages(rows):
        mixed = _mixed_heads(rows, i, of_ref, ob_ref, gr_ref, bg_ref, u_ref, up_ref[BF16_ROWS - 1:BF16_ROWS, :],
                             un_ref[0:1, :], cw_ref, cb_ref, seq)
        yield
        x2 = x_ref[rows, :] + _dot(mixed, wo_ref[...])
        yield
        box = {}
        yield from _cross_attention_stages(x2, xn_ref, wq_ref, k_ref, v_ref, mem_rows, wxo_ref, box)
        yield
        x3 = box["y"]
        yield from _swiglu_stages(x3, rows, fn_ref, wg_ref, wu_ref, wd_ref, a_ref, box)
        y = box["y"]
        if final:
            yield
            y = _rms(y, last_ref[...])
        o_ref[rows, :] = y

    _run_staggered(stages(rows) for rows in _row_slices(tm))


def _post(x, o_f, o_b, gr, bg, u, conv_w, conv_b, w_out, xattn_norm, w_q, mem, mem_norm, w_kv, w_xo,
          ffn_norm, wg, wu, wd, seq, final_norm=None):
    t, d = x.shape
    tm = TOKEN_TILE
    d_ff = wg.shape[1]
    n_batch, n_mem, _ = mem.shape
    assert seq % tm == 0 and tm % BF16_ROWS == 0 and d_ff % FFN_CHUNK == 0 and n_mem % BF16_ROWS == 0
    final = final_norm is not None
    tile = lambda w: pl.BlockSpec((tm, w), lambda i: (i, 0))
    hb = tm // BF16_ROWS
    n_halo = t // BF16_ROWS
    halo_prev = pl.BlockSpec((BF16_ROWS, CONV_W), lambda i: (jnp.maximum(i * hb - 1, 0), 0))
    halo_next = pl.BlockSpec((BF16_ROWS, CONV_W), lambda i: (jnp.minimum((i + 1) * hb, n_halo - 1), 0))
    for w in (w_out, w_q, w_kv, w_xo, wg, wu, wd):
        assert w.dtype == BF16
    mem_rows = n_batch * n_mem
    in_specs = [tile(d), tile(GLA_VAL_W), tile(GLA_VAL_W), tile(GLA_VAL_W), tile(CONV_W), tile(CONV_W),
                halo_prev, halo_next, _const_spec(conv_w.shape),
                _const_spec((1, CONV_W)), _const_spec(w_out.shape),
                _const_spec((1, d)), _const_spec(w_q.shape), _const_spec((mem_rows, d)), _const_spec((1, d)),
                _const_spec(w_kv.shape), _const_spec(w_xo.shape),
                _const_spec((1, d)), _const_spec(wg.shape), _const_spec(wu.shape), _const_spec(wd.shape)]
    args = [x, o_f, o_b, gr, bg, u, u, u, conv_w, conv_b.reshape(1, -1), w_out,
            xattn_norm.reshape(1, d), w_q, mem.reshape(mem_rows, d), mem_norm.reshape(1, d), w_kv, w_xo,
            ffn_norm.reshape(1, d), wg, wu, wd]
    if final:
        in_specs.append(_const_spec((1, d)))
        args.append(final_norm.reshape(1, d))
    return pl.pallas_call(
        functools.partial(_post_body, seq=seq, n_batch=n_batch, final=final),
        grid=(t // tm,),
        in_specs=in_specs,
        out_specs=tile(d),
        out_shape=jax.ShapeDtypeStruct((t, d), F32),
        scratch_shapes=[pltpu.VMEM((tm, d_ff), BF16), pltpu.VMEM((mem_rows, d), BF16),
                        pltpu.VMEM((mem_rows, d), BF16)],
        compiler_params=_cparams(1),
        name="post_final" if final else "post",
    )(*args)


def _pack_gate(gate_w, gate_b, z_row0):
    w = jnp.pad(gate_w, ((z_row0, LANES - z_row0 - GATE_RANK), (0, 0)))
    return w.astype(BF16), gate_b.reshape(1, -1).astype(F32)


def kernel(x, mem, ffn1_norm, ffn1_w_gate, ffn1_w_up, ffn1_w_down, mix_norm, w_in, gate_fwd_w, gate_fwd_b,
           gate_bwd_w, gate_bwd_b, gla_norm, conv_w, conv_b, w_out, xattn_norm, mem_norm, xattn_w_q,
           xattn_w_kv, xattn_w_o, ffn2_norm, ffn2_w_gate, ffn2_w_up, ffn2_w_down, final_norm):
    batch, seq, d = x.shape
    depth = w_in.shape[0]
    assert seq % GLA_BLOCK == 0 and GLA_BLOCK % CHUNK == 0 and seq % TOKEN_TILE == 0
    h = x.reshape(batch * seq, d)
    for l in range(depth):
        post_weights = [w_out[l], xattn_w_q[l], xattn_w_kv[l], xattn_w_o[l],
                        ffn2_w_gate[l], ffn2_w_up[l], ffn2_w_down[l]]
        (h, q, k, v, gr, bg, u, z), (w_o, w_q, w_kv, w_xo, wg2, wu2, wd2) = _pre(
            h, ffn1_norm[l], ffn1_w_gate[l], ffn1_w_up[l], ffn1_w_down[l], mix_norm[l], w_in[l], gla_norm[l],
            post_weights)
        wgate_f, gbias_f = _pack_gate(gate_fwd_w[l], gate_fwd_b[l], 0)
        wgate_b, gbias_b = _pack_gate(gate_bwd_w[l], gate_bwd_b[l], GATE_RANK)
        o_f, o_b = _gla(q, k, v, z, wgate_f, wgate_b, gbias_f, gbias_b, batch, seq)
        h = _post(h, o_f, o_b, gr, bg, u, conv_w[l], conv_b[l], w_o,
                  xattn_norm[l], w_q, mem, mem_norm[l], w_kv, w_xo, ffn2_norm[l], wg2, wu2, wd2, seq,
                  final_norm=final_norm if l == depth - 1 else None)
    return h.reshape(batch, seq, d)
```

```python
import functools

import jax
import jax.numpy as jnp
from jax import lax
from jax.experimental import pallas as pl
from jax.experimental.pallas import tpu as pltpu

F32 = jnp.float32
BF16 = jnp.bfloat16

XATTN_HEADS = 4
GLA_HEADS = 4
GLA_DK = 64
GLA_DV = 128
GLA_KEY_W = GLA_HEADS * GLA_DK
GLA_VAL_W = GLA_HEADS * GLA_DV
GATE_RANK = 16
GATE_TAU = 16.0
CHUNK = 64
CONV_W = 512
EPS = 1e-6
LOG2_E = 1.4426950408889634

LANES = 128
SUBLANES = 8
BF16_ROWS = 16
TOKEN_TILE = 512
TILE_SPLIT = 2
GLA_BLOCK = 2048
GLA_PIPELINE_DEPTH = 2
FFN_CHUNK = 256
STAGE_ROWS = 64
STAGE_SLOTS = 12
VMEM_LIMIT_BYTES = 58 * 1024 * 1024


def _cparams(n_axes):
    return pltpu.CompilerParams(
        dimension_semantics=("arbitrary",) * n_axes,
        vmem_limit_bytes=VMEM_LIMIT_BYTES,
    )


def _const_spec(shape):
    nd = len(shape)
    return pl.BlockSpec(shape, lambda *_: (0,) * nd, pipeline_mode=pl.Buffered(1))


_ANY_SPEC = pl.BlockSpec(memory_space=pl.ANY)


def _load_weights_bf16(jobs, stage_ref, sem_ref):
    n_slots, rows = stage_ref.shape[0], stage_ref.shape[1]
    ahead = n_slots - 1
    counts = [n_rows // rows for _, _, n_rows, _, _ in jobs]
    for (src, _, n_rows, dst, _), n in zip(jobs, counts):
        assert n_rows % rows == 0 and n >= ahead and src.shape[1] == dst.shape[1] <= stage_ref.shape[2]
    bases = [sum(counts[:j]) for j in range(len(jobs))]
    stream = [(j, c) for j, n in enumerate(counts) for c in range(n)]

    def copy(j, c, g):
        src, src_row0, _, _, _ = jobs[j]
        r0 = pl.multiple_of(src_row0 + c * rows, SUBLANES)
        return pltpu.make_async_copy(src.at[pl.ds(r0, rows), :],
                                     stage_ref.at[g % n_slots, :, pl.ds(0, src.shape[1])], sem_ref.at[g % n_slots])

    def finish(j, c, g):
        _, _, _, dst, dst_row0 = jobs[j]
        copy(j, c, g).wait()
        r0 = pl.multiple_of(dst_row0 + c * rows, BF16_ROWS)
        dst[pl.ds(r0, rows), :] = stage_ref[g % n_slots, :, 0:dst.shape[1]].astype(BF16)

    for g in range(ahead):
        copy(*stream[g], g).start()
    for j, n in enumerate(counts):
        def body(c, carry, j=j):
            copy(j, c + ahead, bases[j] + c + ahead).start()
            finish(j, c, bases[j] + c)
            return carry

        lax.fori_loop(0, n - ahead, body, 0)
        for c in range(n - ahead, n):
            g = bases[j] + c
            if g + ahead < len(stream):
                copy(*stream[g + ahead], g + ahead).start()
            finish(j, c, g)


def _convert_weights_across_steps(jobs, step, n_steps, in_sem, out_sem):
    def chunks(job):
        return job[0].shape[0] // job[2].shape[1]

    def copy_in(j, c, slot):
        src, _, stage, _ = jobs[j]
        r = stage.shape[1]
        return pltpu.make_async_copy(src.at[pl.ds(pl.multiple_of(c * r, r), r), :], stage.at[slot],
                                     in_sem.at[j, slot])

    def copy_out(j, c, slot):
        _, dst, stage, bstage = jobs[j]
        r = stage.shape[1]
        return pltpu.make_async_copy(bstage.at[slot], dst.at[pl.ds(pl.multiple_of(c * r, r), r), :],
                                     out_sem.at[j, slot])

    for job in jobs:
        assert job[0].shape[0] % job[2].shape[1] == 0 and 2 <= chunks(job) <= n_steps
    slot = step % 2
    for n in sorted({chunks(job) for job in jobs}):
        group = [j for j, job in enumerate(jobs) if chunks(job) == n]

        @pl.when(step == 0)
        def _():
            for j in group:
                copy_in(j, 0, 0).start()

        @pl.when(step + 1 < n)
        def _():
            for j in group:
                copy_in(j, step + 1, 1 - slot).start()

        @pl.when(jnp.logical_and(step >= 2, step - 2 < n))
        def _():
            for j in group:
                copy_out(j, step - 2, slot).wait()

        @pl.when(step < n)
        def _():
            for j in group:
                _, _, stage, bstage = jobs[j]
                copy_in(j, step, slot).wait()
                bstage[slot] = stage[slot].astype(BF16)
                copy_out(j, step, slot).start()

        late = [c for c in (n_steps - 2, n_steps - 1) if c < n]
        if late:
            @pl.when(step == n_steps - 1)
            def _():
                for j in group:
                    for c in late:
                        copy_out(j, c, c % 2).wait()


def _rms(x, g):
    ms = jnp.mean(x * x, axis=-1, keepdims=True)
    return x * lax.rsqrt(ms + EPS) * g


def _dot(a, b):
    return jnp.dot(a, b, preferred_element_type=F32)


def _dot_nt(a, b):
    return lax.dot_general(a, b, (((1,), (1,)), ((), ())), preferred_element_type=F32)


def _row_slices(tile_rows):
    assert tile_rows % (TILE_SPLIT * BF16_ROWS) == 0
    part = tile_rows // TILE_SPLIT
    return [slice(i * part, (i + 1) * part) for i in range(TILE_SPLIT)]


def _run_staggered(stage_gens):
    active = list(stage_gens)
    next(active[0])
    while active:
        for gen in list(active):
            try:
                next(gen)
            except StopIteration:
                active.remove(gen)


def _swiglu_stages(x, rows, n_ref, wg_ref, wu_ref, wd_ref, a_ref, out):
    h = _rms(x, n_ref[...]).astype(BF16)
    yield
    for c in range(wg_ref.shape[1] // FFN_CHUNK):
        sl = slice(c * FFN_CHUNK, (c + 1) * FFN_CHUNK)
        g = _dot(h, wg_ref[:, sl])
        u = _dot(h, wu_ref[:, sl])
        a_ref[rows, sl] = (g * jax.nn.sigmoid(g) * u).astype(BF16)
    yield
    out["y"] = x + 0.5 * _dot(a_ref[rows, :], wd_ref[...])


_P_Q, _P_K, _P_V, _P_R = 0, GLA_KEY_W, 2 * GLA_KEY_W, 2 * GLA_KEY_W + GLA_VAL_W
_P_BG = _P_R + GLA_VAL_W
_P_CG = _P_BG + CONV_W
_P_XV = _P_CG + CONV_W
_P_Z = _P_XV + CONV_W
_P_END = _P_Z + LANES


def _pre_body(*refs, n_later, n_steps):
    (x_ref, fn_ref, wg_hbm, wu_hbm, wd_hbm, mn_ref, wint_hbm, gn_ref), refs = refs[:8], refs[8:]
    later_f32, refs = refs[:n_later], refs[n_later:]
    (x1_ref, q_ref, k_ref, v_ref, gr_ref, bg_ref, u_ref, z_ref), refs = refs[:8], refs[8:]
    later_bf16, refs = refs[:n_later], refs[n_later:]
    (a_ref, wg_ref, wu_ref, wd_ref, wp_ref, stage_ref, sem_ref), refs = refs[:7], refs[7:]
    later_stage, later_bstage = refs[:n_later], refs[n_later:2 * n_later]
    later_in_sem, later_out_sem = refs[2 * n_later:]

    _convert_weights_across_steps(list(zip(later_f32, later_bf16, later_stage, later_bstage)),
                                  pl.program_id(0), n_steps, later_in_sem, later_out_sem)

    @pl.when(pl.program_id(0) == 0)
    def _():
        d, d_ff = wg_hbm.shape
        z_w = 2 * GATE_RANK
        main = _P_BG
        conv = _P_Z - _P_BG
        _load_weights_bf16(
            [(wg_hbm, 0, d, wg_ref, 0), (wu_hbm, 0, d, wu_ref, 0), (wd_hbm, 0, d_ff, wd_ref, 0),
             (wint_hbm, 0, main, wp_ref, 0), (wint_hbm, main + z_w, conv, wp_ref, _P_BG)],
            stage_ref, sem_ref)
        z_copy = pltpu.make_async_copy(wint_hbm.at[pl.ds(main, z_w), :],
                                       stage_ref.at[0, pl.ds(0, z_w), pl.ds(0, d)], sem_ref.at[0])
        z_copy.start()
        z_copy.wait()
        wp_ref[_P_Z:_P_Z + z_w, :] = stage_ref[0, 0:z_w, 0:d].astype(BF16)
        wp_ref[_P_Z + z_w:_P_END, :] = jnp.zeros((LANES - z_w, d), BF16)

    def stages(rows):
        ffn = {}
        yield from _swiglu_stages(x_ref[rows, :], rows, fn_ref, wg_ref, wu_ref, wd_ref, a_ref, ffn)
        x1 = ffn["y"]
        x1_ref[rows, :] = x1
        yield
        h = _rms(x1, mn_ref[...]).astype(BF16)
        yield
        p = _dot_nt(h, wp_ref[...])
        q_ref[rows, :] = p[:, _P_Q:_P_K] * (GLA_DK ** -0.5)
        k_ref[rows, :] = p[:, _P_K:_P_V]
        v_ref[rows, :] = p[:, _P_V:_P_R].astype(BF16)
        r = p[:, _P_R:_P_BG]
        gr_ref[rows, :] = (gn_ref[...] * (r * jax.nn.sigmoid(r))).astype(BF16)
        bg_ref[rows, :] = p[:, _P_BG:_P_CG].astype(BF16)
        u_ref[rows, :] = (p[:, _P_CG:_P_XV] * p[:, _P_XV:_P_Z]).astype(BF16)
        z_ref[rows, :] = p[:, _P_Z:_P_END].astype(BF16)

    _run_staggered(stages(rows) for rows in _row_slices(x_ref.shape[0]))


def _rows_per_step(n_rows, n_steps):
    r = -(-n_rows // n_steps)
    r += -r % BF16_ROWS
    while n_rows % r:
        r += BF16_ROWS
    return r


def _pre(x, ffn_norm, wg, wu, wd, mix_norm, w_in, gla_norm, later_weights):
    t, d = x.shape
    d_ff = wg.shape[1]
    assert t % TOKEN_TILE == 0 and d_ff % FFN_CHUNK == 0 and TOKEN_TILE % CHUNK == 0
    n_steps = t // TOKEN_TILE
    tile = lambda w: pl.BlockSpec((TOKEN_TILE, w), lambda i: (i, 0))
    out_w = [(d, F32), (GLA_KEY_W, F32), (GLA_KEY_W, F32), (GLA_VAL_W, BF16), (GLA_VAL_W, BF16),
             (CONV_W, BF16), (CONV_W, BF16), (LANES, BF16)]
    assert w_in.shape[1] == _P_Z + 2 * GATE_RANK
    n_later = len(later_weights)
    later_stage = [(2, _rows_per_step(w.shape[0], n_steps), w.shape[1]) for w in later_weights]
    outs = pl.pallas_call(
        functools.partial(_pre_body, n_later=n_later, n_steps=n_steps),
        grid=(n_steps,),
        in_specs=[tile(d), _const_spec((1, d)), _ANY_SPEC, _ANY_SPEC, _ANY_SPEC, _const_spec((1, d)), _ANY_SPEC,
                  _const_spec((1, GLA_VAL_W))] + [_ANY_SPEC] * n_later,
        out_specs=[tile(w) for w, _ in out_w] + [_ANY_SPEC] * n_later,
        out_shape=[jax.ShapeDtypeStruct((t, w), dt) for w, dt in out_w]
        + [jax.ShapeDtypeStruct(w.shape, BF16) for w in later_weights],
        scratch_shapes=[pltpu.VMEM((TOKEN_TILE, d_ff), BF16),
                        pltpu.VMEM((d, d_ff), BF16), pltpu.VMEM((d, d_ff), BF16), pltpu.VMEM((d_ff, d), BF16),
                        pltpu.VMEM((_P_END, d), BF16),
                        pltpu.VMEM((STAGE_SLOTS, STAGE_ROWS, max(d_ff, d)), F32),
                        pltpu.SemaphoreType.DMA((STAGE_SLOTS,))]
        + [pltpu.VMEM(s, F32) for s in later_stage] + [pltpu.VMEM(s, BF16) for s in later_stage]
        + [pltpu.SemaphoreType.DMA((n_later, 2)), pltpu.SemaphoreType.DMA((n_later, 2))],
        compiler_params=_cparams(1),
        name="pre",
    )(x, ffn_norm.reshape(1, d), wg, wu, wd, mix_norm.reshape(1, d), w_in.T, gla_norm.reshape(1, -1),
      *later_weights)
    return outs[:len(out_w)], outs[len(out_w):]


def _chunk_cumsum(x, reverse):
    n, w = x.shape
    groups = n // SUBLANES
    x3 = x.reshape(groups, SUBLANES, w)
    pos = lax.broadcasted_iota(jnp.int32, x3.shape, 1)
    step = 1
    while step < SUBLANES:
        if reverse:
            x3 = x3 + jnp.where(pos < SUBLANES - step, pltpu.roll(x3, SUBLANES - step, 1), 0.0)
        else:
            x3 = x3 + jnp.where(pos >= step, pltpu.roll(x3, step, 1), 0.0)
        step *= 2
    per = CHUNK // SUBLANES
    x4 = x3.reshape(n // CHUNK, per, SUBLANES, w)
    edge = x4[:, :, 0:1, :] if reverse else x4[:, :, SUBLANES - 1:SUBLANES, :]
    outs = [None] * per
    carry = None
    for j in (range(per - 1, -1, -1) if reverse else range(per)):
        outs[j] = x4[:, j] if carry is None else x4[:, j] + carry
        carry = edge[:, j] if carry is None else carry + edge[:, j]
    return jnp.stack(outs, axis=1).reshape(n, w)


def _gla_chunks(q_ref, k_ref, v_ref, z_ref, wgate_ref, gbias_ref, o_ref, s_ref, *, reverse):
    n_chunks = q_ref.shape[0] // CHUNK
    kw = GLA_KEY_W
    assert GLA_DK == CHUNK and 2 * CHUNK == LANES == GLA_DV and CHUNK & (CHUNK - 1) == 0
    shift = CHUNK.bit_length() - 1
    gz = _dot(z_ref[...], wgate_ref[...]) + gbias_ref[...]

    row_head = lax.broadcasted_iota(jnp.int32, (kw, kw), 0) >> shift
    lane_head = lax.broadcasted_iota(jnp.int32, (kw, kw), 1) >> shift
    same_head = row_head.astype(F32).astype(BF16) == lane_head.astype(F32).astype(BF16)
    pos_q = lax.broadcasted_iota(jnp.int32, (CHUNK, kw), 0)
    pos_k = lax.broadcasted_iota(jnp.int32, (CHUNK, kw), 1) & (CHUNK - 1)
    causal = (pos_k >= pos_q) if reverse else (pos_k <= pos_q)
    low_half = lax.broadcasted_iota(jnp.int32, (CHUNK, LANES), 1) < CHUNK
    low_feat = lax.broadcasted_iota(jnp.int32, (kw, LANES), 1) < CHUNK

    def chunk(c):
        rows = slice(c * CHUNK, (c + 1) * CHUNK)
        g = gz[rows, :]
        log2_gate = (jnp.minimum(g, 0.0) - jnp.log(1.0 + jnp.exp2(jnp.abs(g) * -LOG2_E))) * (LOG2_E / GATE_TAU)
        bc = _chunk_cumsum(log2_gate, reverse)
        q = q_ref[rows, :]
        k = k_ref[rows, :]
        v = v_ref[rows, :]
        b_last = bc[0:1, :] if reverse else bc[CHUNK - 1:CHUNK, :]
        q_in = q * jnp.exp2(bc)
        k_in = (k * jnp.exp2(-bc)).astype(BF16)
        k_st = k * jnp.exp2(b_last - bc)
        k_blk = jnp.where(same_head, jnp.concatenate([k_in] * GLA_HEADS, axis=0), jnp.zeros((), BF16))
        att = _dot_nt(q_in.astype(BF16), k_blk)
        att = jnp.where(causal, att, 0.0)
        decay_rows = jnp.broadcast_to(jnp.exp2(b_last), (SUBLANES, kw))
        pad = jnp.zeros((LANES - CHUNK - SUBLANES, kw), F32)
        xt = jnp.concatenate([k_st, decay_rows, pad], axis=0).T
        kst_t = jnp.where(low_feat, xt, 0.0).astype(BF16)
        yield
        outs = []
        for pair in range(GLA_HEADS // 2):
            lanes = slice(pair * LANES, (pair + 1) * LANES)
            a_col, q_col = att[:, lanes], q_in[:, lanes]
            a_swapped = pltpu.roll(a_col, CHUNK, 1)
            q_swapped = pltpu.roll(q_col, CHUNK, 1)
            top = jnp.concatenate([jnp.where(low_half, a_col, q_swapped),
                                   jnp.where(low_half, a_swapped, q_col)], axis=1).astype(BF16)
            heads = (2 * pair, 2 * pair + 1)
            bottom = jnp.concatenate([kst_t[h * GLA_DK:(h + 1) * GLA_DK, :] for h in heads], axis=1)
            s_in = [s_ref[h] for h in heads]
            zero = jnp.zeros((LANES, GLA_DV), BF16)
            vs = [jnp.concatenate([v[:, h * GLA_DV:(h + 1) * GLA_DV], s.astype(BF16)], axis=0)
                  for h, s in zip(heads, s_in)]
            rhs = jnp.concatenate([jnp.concatenate([vs[0], zero], axis=1),
                                   jnp.concatenate([zero, vs[1]], axis=1)], axis=0)
            res = _dot(jnp.concatenate([top, bottom], axis=0), rhs)
            outs.append(res[:CHUNK, :])
            for i, (h, s) in enumerate(zip(heads, s_in)):
                decay = xt[h * GLA_DK:(h + 1) * GLA_DK, CHUNK:CHUNK + 1]
                s_ref[h] = decay * s + res[CHUNK:, i * GLA_DV:(i + 1) * GLA_DV]
        o_ref[rows, :] = jnp.concatenate(outs, axis=1).astype(o_ref.dtype)

    return [chunk(c) for c in (range(n_chunks - 1, -1, -1) if reverse else range(n_chunks))]


def _run_pipelined(two_stage_gens, depth):
    started = []
    for gen in two_stage_gens:
        next(gen)
        started.append(gen)
        if len(started) > depth:
            for _ in started.pop(0):
                pass
    for gen in started:
        for _ in gen:
            pass


def _gla_body(qf_ref, kf_ref, vf_ref, zf_ref, qb_ref, kb_ref, vb_ref, zb_ref, wgf_ref, wgb_ref, gbf_ref, gbb_ref,
              of_ref, ob_ref, sf_ref, sb_ref):
    @pl.when(pl.program_id(1) == 0)
    def _():
        sf_ref[...] = jnp.zeros_like(sf_ref)
        sb_ref[...] = jnp.zeros_like(sb_ref)

    fwd = _gla_chunks(qf_ref, kf_ref, vf_ref, zf_ref, wgf_ref, gbf_ref, of_ref, sf_ref, reverse=False)
    bwd = _gla_chunks(qb_ref, kb_ref, vb_ref, zb_ref, wgb_ref, gbb_ref, ob_ref, sb_ref, reverse=True)
    _run_pipelined([gen for pair in zip(fwd, bwd) for gen in pair], GLA_PIPELINE_DEPTH)


def _gla(q, k, v, z, wgate_f, wgate_b, gbias_f, gbias_b, batch, seq):
    nb = seq // GLA_BLOCK
    fwd = lambda w: pl.BlockSpec((GLA_BLOCK, w), lambda b, j: (b * nb + j, 0))
    bwd = lambda w: pl.BlockSpec((GLA_BLOCK, w), lambda b, j: (b * nb + nb - 1 - j, 0))
    t = batch * seq
    return pl.pallas_call(
        _gla_body,
        grid=(batch, nb),
        in_specs=[fwd(GLA_KEY_W), fwd(GLA_KEY_W), fwd(GLA_VAL_W), fwd(LANES),
                  bwd(GLA_KEY_W), bwd(GLA_KEY_W), bwd(GLA_VAL_W), bwd(LANES),
                  _const_spec(wgate_f.shape), _const_spec(wgate_b.shape),
                  _const_spec(gbias_f.shape), _const_spec(gbias_b.shape)],
        out_specs=[fwd(GLA_VAL_W), bwd(GLA_VAL_W)],
        out_shape=[jax.ShapeDtypeStruct((t, GLA_VAL_W), BF16)] * 2,
        scratch_shapes=[pltpu.VMEM((GLA_HEADS, GLA_DK, GLA_DV), F32)] * 2,
        compiler_params=_cparams(2),
        name="gla",
    )(q, k, v, z, q, k, v, z, wgate_f, wgate_b, gbias_f, gbias_b)


def _memory_kv(m_ref, n_ref, w_ref, k_ref, v_ref):
    d = m_ref.shape[-1]
    h = _rms(m_ref[...], n_ref[...]).astype(BF16)
    kv = _dot(h, w_ref[...])
    k_ref[...] = kv[:, :d].astype(BF16)
    v_ref[...] = kv[:, d:].astype(BF16)


def _mixed_heads(rows, tile, of_ref, ob_ref, gr_ref, bg_ref, u_ref, edge_before, edge_after, cw_ref, cb_ref, seq):
    tm = u_ref.shape[0]
    n = rows.stop - rows.start
    o = of_ref[rows, :].astype(F32) + ob_ref[rows, :].astype(F32)
    heads = []
    for h in range(GLA_HEADS):
        oh = o[:, h * GLA_DV:(h + 1) * GLA_DV]
        heads.append(oh * lax.rsqrt(jnp.mean(oh * oh, axis=-1, keepdims=True) + EPS))
    a_out = jnp.concatenate(heads, axis=1) * gr_ref[rows, :].astype(F32)
    u = u_ref[rows, :].astype(F32)
    if rows.start == 0:
        at_seq_start = (tile * tm) % seq == 0
        before = jnp.where(at_seq_start, 0.0, edge_before.astype(F32))
    else:
        before = u_ref[rows.start - 1:rows.start, :].astype(F32)
    if rows.stop == tm:
        at_seq_end = ((tile + 1) * tm) % seq == 0
        after = jnp.where(at_seq_end, 0.0, edge_after.astype(F32))
    else:
        after = u_ref[rows.stop:rows.stop + 1, :].astype(F32)
    row = lax.broadcasted_iota(jnp.int32, (SUBLANES, u.shape[1]), 0)
    down, up = pltpu.roll(u, 1, 0), pltpu.roll(u, n - 1, 0)
    u_prev = jnp.concatenate([jnp.where(row == 0, before, down[:SUBLANES]), down[SUBLANES:]], axis=0)
    u_next = jnp.concatenate([up[:n - SUBLANES], jnp.where(row == SUBLANES - 1, after, up[n - SUBLANES:])], axis=0)
    conv = cw_ref[0:1, :] * u_prev + cw_ref[1:2, :] * u + cw_ref[2:3, :] * u_next + cb_ref[...]
    c_out = bg_ref[rows, :].astype(F32) * conv
    return jnp.concatenate([a_out, c_out], axis=1).astype(BF16)


def _cross_attention_stages(x, n_ref, wq_ref, k_ref, v_ref, mem_rows, wo_ref, out):
    d = x.shape[-1]
    hd = d // XATTN_HEADS
    h = _rms(x, n_ref[...]).astype(BF16)
    yield
    q = _dot(h, wq_ref[...])
    yield
    heads = [slice(a * hd, (a + 1) * hd) for a in range(XATTN_HEADS)]
    scores = [_dot_nt(q[:, sl].astype(BF16), k_ref[mem_rows, sl]) * (hd ** -0.5) for sl in heads]
    probs = [jnp.exp(s - jnp.max(s, axis=-1, keepdims=True)) for s in scores]
    outs = [_dot(p.astype(BF16), v_ref[mem_rows, sl]) / jnp.sum(p, axis=-1, keepdims=True)
            for p, sl in zip(probs, heads)]
    o = jnp.concatenate(outs, axis=1).astype(BF16)
    yield
    out["y"] = x + _dot(o, wo_ref[...])


def _post_body(*refs, seq, n_batch, final):
    (x_ref, of_ref, ob_ref, gr_ref, bg_ref, u_ref, up_ref, un_ref, cw_ref, cb_ref, wo_ref,
     xn_ref, wq_ref, mem_ref, mn_ref, wkv_ref, wxo_ref, fn_ref, wg_ref, wu_ref, wd_ref) = refs[:21]
    refs = refs[21:]
    if final:
        last_ref, refs = refs[0], refs[1:]
    o_ref, a_ref, k_ref, v_ref = refs
    tm = x_ref.shape[0]
    i = pl.program_id(0)

    @pl.when(i == 0)
    def _():
        _memory_kv(mem_ref, mn_ref, wkv_ref, k_ref, v_ref)

    n_mem = mem_ref.shape[0] // n_batch
    mem_rows = pl.ds(pl.multiple_of((i * tm // seq) * n_mem, n_mem), n_mem)

    def stages(rows):
        mixed = _mixed_heads(rows, i, of_ref, ob_ref, gr_ref, bg_ref, u_ref, up_ref[BF16_ROWS - 1:BF16_ROWS, :],
                             un_ref[0:1, :], cw_ref, cb_ref, seq)
        yield
        x2 = x_ref[rows, :] + _dot(mixed, wo_ref[...])
        yield
        box = {}
        yield from _cross_attention_stages(x2, xn_ref, wq_ref, k_ref, v_ref, mem_rows, wxo_ref, box)
        yield
        x3 = box["y"]
        yield from _swiglu_stages(x3, rows, fn_ref, wg_ref, wu_ref, wd_ref, a_ref, box)
        y = box["y"]
        if final:
            yield
            y = _rms(y, last_ref[...])
        o_ref[rows, :] = y

    _run_staggered(stages(rows) for rows in _row_slices(tm))


def _post(x, o_f, o_b, gr, bg, u, conv_w, conv_b, w_out, xattn_norm, w_q, mem, mem_norm, w_kv, w_xo,
          ffn_norm, wg, wu, wd, seq, final_norm=None):
    t, d = x.shape
    tm = TOKEN_TILE
    d_ff = wg.shape[1]
    n_batch, n_mem, _ = mem.shape
    assert seq % tm == 0 and tm % BF16_ROWS == 0 and d_ff % FFN_CHUNK == 0 and n_mem % BF16_ROWS == 0
    final = final_norm is not None
    tile = lambda w: pl.BlockSpec((tm, w), lambda i: (i, 0))
    hb = tm // BF16_ROWS
    n_halo = t // BF16_ROWS
    halo_prev = pl.BlockSpec((BF16_ROWS, CONV_W), lambda i: (jnp.maximum(i * hb - 1, 0), 0))
    halo_next = pl.BlockSpec((BF16_ROWS, CONV_W), lambda i: (jnp.minimum((i + 1) * hb, n_halo - 1), 0))
    for w in (w_out, w_q, w_kv, w_xo, wg, wu, wd):
        assert w.dtype == BF16
    mem_rows = n_batch * n_mem
    in_specs = [tile(d), tile(GLA_VAL_W), tile(GLA_VAL_W), tile(GLA_VAL_W), tile(CONV_W), tile(CONV_W),
                halo_prev, halo_next, _const_spec(conv_w.shape),
                _const_spec((1, CONV_W)), _const_spec(w_out.shape),
                _const_spec((1, d)), _const_spec(w_q.shape), _const_spec((mem_rows, d)), _const_spec((1, d)),
                _const_spec(w_kv.shape), _const_spec(w_xo.shape),
                _const_spec((1, d)), _const_spec(wg.shape), _const_spec(wu.shape), _const_spec(wd.shape)]
    args = [x, o_f, o_b, gr, bg, u, u, u, conv_w, conv_b.reshape(1, -1), w_out,
            xattn_norm.reshape(1, d), w_q, mem.reshape(mem_rows, d), mem_norm.reshape(1, d), w_kv, w_xo,
            ffn_norm.reshape(1, d), wg, wu, wd]
    if final:
        in_specs.append(_const_spec((1, d)))
        args.append(final_norm.reshape(1, d))
    return pl.pallas_call(
        functools.partial(_post_body, seq=seq, n_batch=n_batch, final=final),
        grid=(t // tm,),
        in_specs=in_specs,
        out_specs=tile(d),
        out_shape=jax.ShapeDtypeStruct((t, d), F32),
        scratch_shapes=[pltpu.VMEM((tm, d_ff), BF16), pltpu.VMEM((mem_rows, d), BF16),
                        pltpu.VMEM((mem_rows, d), BF16)],
        compiler_params=_cparams(1),
        name="post_final" if final else "post",
    )(*args)


def _pack_gate(gate_w, gate_b, z_row0):
    w = jnp.pad(gate_w, ((z_row0, LANES - z_row0 - GATE_RANK), (0, 0)))
    return w.astype(BF16), gate_b.reshape(1, -1).astype(F32)


def kernel(x, mem, ffn1_norm, ffn1_w_gate, ffn1_w_up, ffn1_w_down, mix_norm, w_in, gate_fwd_w, gate_fwd_b,
           gate_bwd_w, gate_bwd_b, gla_norm, conv_w, conv_b, w_out, xattn_norm, mem_norm, xattn_w_q,
           xattn_w_kv, xattn_w_o, ffn2_norm, ffn2_w_gate, ffn2_w_up, ffn2_w_down, final_norm):
    batch, seq, d = x.shape
    depth = w_in.shape[0]
    assert seq % GLA_BLOCK == 0 and GLA_BLOCK % CHUNK == 0 and seq % TOKEN_TILE == 0
    h = x.reshape(batch * seq, d)
    for l in range(depth):
        post_weights = [w_out[l], xattn_w_q[l], xattn_w_kv[l], xattn_w_o[l],
                        ffn2_w_gate[l], ffn2_w_up[l], ffn2_w_down[l]]
        (h, q, k, v, gr, bg, u, z), (w_o, w_q, w_kv, w_xo, wg2, wu2, wd2) = _pre(
            h, ffn1_norm[l], ffn1_w_gate[l], ffn1_w_up[l], ffn1_w_down[l], mix_norm[l], w_in[l], gla_norm[l],
            post_weights)
        wgate_f, gbias_f = _pack_gate(gate_fwd_w[l], gate_fwd_b[l], 0)
        wgate_b, gbias_b = _pack_gate(gate_bwd_w[l], gate_bwd_b[l], GATE_RANK)
        o_f, o_b = _gla(q, k, v, z, wgate_f, wgate_b, gbias_f, gbias_b, batch, seq)
        h = _post(h, o_f, o_b, gr, bg, u, conv_w[l], conv_b[l], w_o,
                  xattn_norm[l], w_q, mem, mem_norm[l], w_kv, w_xo, ffn2_norm[l], wg2, wu2, wd2, seq,
                  final_norm=final_norm if l == depth - 1 else None)
    return h.reshape(batch, seq, d)
```

```python
import functools

import jax
import jax.numpy as jnp
from jax import lax
from jax.experimental import pallas as pl
from jax.experimental.pallas import tpu as pltpu

F32 = jnp.float32
BF16 = jnp.bfloat16

XATTN_HEADS = 4
GLA_HEADS = 4
GLA_DK = 64
GLA_DV = 128
GLA_KEY_W = GLA_HEADS * GLA_DK
GLA_VAL_W = GLA_HEADS * GLA_DV
GATE_RANK = 16
GATE_TAU = 16.0
CHUNK = 64
CONV_W = 512
EPS = 1e-6
LOG2_E = 1.4426950408889634

LANES = 128
SUBLANES = 8
BF16_ROWS = 16
TOKEN_TILE = 512
TILE_SPLIT = 2
GLA_BLOCK = 1024
GLA_PIPELINE_DEPTH = 2
FFN_CHUNK = 256
STAGE_ROWS = 128
STAGE_SLOTS = 6
VMEM_LIMIT_BYTES = 58 * 1024 * 1024


def _cparams(n_axes):
    return pltpu.CompilerParams(
        dimension_semantics=("arbitrary",) * n_axes,
        vmem_limit_bytes=VMEM_LIMIT_BYTES,
    )


def _const_spec(shape):
    nd = len(shape)
    return pl.BlockSpec(shape, lambda *_: (0,) * nd, pipeline_mode=pl.Buffered(1))


_ANY_SPEC = pl.BlockSpec(memory_space=pl.ANY)


def _load_weights_bf16(jobs, stage_ref, sem_ref):
    n_slots, rows = stage_ref.shape[0], stage_ref.shape[1]
    ahead = n_slots - 1
    counts = [n_rows // rows for _, _, n_rows, _, _ in jobs]
    for (src, _, n_rows, dst, _), n in zip(jobs, counts):
        assert n_rows % rows == 0 and n >= ahead and src.shape[1] == dst.shape[1] <= stage_ref.shape[2]
    bases = [sum(counts[:j]) for j in range(len(jobs))]
    stream = [(j, c) for j, n in enumerate(counts) for c in range(n)]

    def copy(j, c, g):
        src, src_row0, _, _, _ = jobs[j]
        r0 = pl.multiple_of(src_row0 + c * rows, SUBLANES)
        return pltpu.make_async_copy(src.at[pl.ds(r0, rows), :],
                                     stage_ref.at[g % n_slots, :, pl.ds(0, src.shape[1])], sem_ref.at[g % n_slots])

    def finish(j, c, g):
        _, _, _, dst, dst_row0 = jobs[j]
        copy(j, c, g).wait()
        r0 = pl.multiple_of(dst_row0 + c * rows, BF16_ROWS)
        dst[pl.ds(r0, rows), :] = stage_ref[g % n_slots, :, 0:dst.shape[1]].astype(BF16)

    for g in range(ahead):
        copy(*stream[g], g).start()
    for j, n in enumerate(counts):
        def body(c, carry, j=j):
            copy(j, c + ahead, bases[j] + c + ahead).start()
            finish(j, c, bases[j] + c)
            return carry

        lax.fori_loop(0, n - ahead, body, 0)
        for c in range(n - ahead, n):
            g = bases[j] + c
            if g + ahead < len(stream):
                copy(*stream[g + ahead], g + ahead).start()
            finish(j, c, g)


def _convert_weights_across_steps(jobs, step, n_steps, in_sem, out_sem):
    def chunks(job):
        return job[0].shape[0] // job[2].shape[1]

    def copy_in(j, c, slot):
        src, _, stage, _ = jobs[j]
        r = stage.shape[1]
        return pltpu.make_async_copy(src.at[pl.ds(pl.multiple_of(c * r, r), r), :], stage.at[slot],
                                     in_sem.at[j, slot])

    def copy_out(j, c, slot):
        _, dst, stage, bstage = jobs[j]
        r = stage.shape[1]
        return pltpu.make_async_copy(bstage.at[slot], dst.at[pl.ds(pl.multiple_of(c * r, r), r), :],
                                     out_sem.at[j, slot])

    for job in jobs:
        assert job[0].shape[0] % job[2].shape[1] == 0 and 2 <= chunks(job) <= n_steps
    slot = step % 2
    for n in sorted({chunks(job) for job in jobs}):
        group = [j for j, job in enumerate(jobs) if chunks(job) == n]

        @pl.when(step == 0)
        def _():
            for j in group:
                copy_in(j, 0, 0).start()

        @pl.when(step + 1 < n)
        def _():
            for j in group:
                copy_in(j, step + 1, 1 - slot).start()

        @pl.when(jnp.logical_and(step >= 2, step - 2 < n))
        def _():
            for j in group:
                copy_out(j, step - 2, slot).wait()

        @pl.when(step < n)
        def _():
            for j in group:
                _, _, stage, bstage = jobs[j]
                copy_in(j, step, slot).wait()
                bstage[slot] = stage[slot].astype(BF16)
                copy_out(j, step, slot).start()

        late = [c for c in (n_steps - 2, n_steps - 1) if c < n]
        if late:
            @pl.when(step == n_steps - 1)
            def _():
                for j in group:
                    for c in late:
                        copy_out(j, c, c % 2).wait()


def _rms(x, g):
    ms = jnp.mean(x * x, axis=-1, keepdims=True)
    return x * lax.rsqrt(ms + EPS) * g


def _dot(a, b):
    return jnp.dot(a, b, preferred_element_type=F32)


def _dot_nt(a, b):
    return lax.dot_general(a, b, (((1,), (1,)), ((), ())), preferred_element_type=F32)


def _row_slices(tile_rows):
    assert tile_rows % (TILE_SPLIT * BF16_ROWS) == 0
    part = tile_rows // TILE_SPLIT
    return [slice(i * part, (i + 1) * part) for i in range(TILE_SPLIT)]


def _run_staggered(stage_gens):
    active = list(stage_gens)
    next(active[0])
    while active:
        for gen in list(active):
            try:
                next(gen)
            except StopIteration:
                active.remove(gen)


def _swiglu_stages(x, rows, n_ref, wg_ref, wu_ref, wd_ref, a_ref, out):
    h = _rms(x, n_ref[...]).astype(BF16)
    yield
    for c in range(wg_ref.shape[1] // FFN_CHUNK):
        sl = slice(c * FFN_CHUNK, (c + 1) * FFN_CHUNK)
        g = _dot(h, wg_ref[:, sl])
        u = _dot(h, wu_ref[:, sl])
        a_ref[rows, sl] = (g * jax.nn.sigmoid(g) * u).astype(BF16)
    yield
    out["y"] = x + 0.5 * _dot(a_ref[rows, :], wd_ref[...])


_P_Q, _P_K, _P_V, _P_R = 0, GLA_KEY_W, 2 * GLA_KEY_W, 2 * GLA_KEY_W + GLA_VAL_W
_P_BG = _P_R + GLA_VAL_W
_P_CG = _P_BG + CONV_W
_P_XV = _P_CG + CONV_W
_P_Z = _P_XV + CONV_W
_P_END = _P_Z + LANES


def _pre_body(*refs, n_later, n_steps):
    (x_ref, fn_ref, wg_hbm, wu_hbm, wd_hbm, mn_ref, wint_hbm, gn_ref, cw_ref, cb_ref), refs = refs[:10], refs[10:]
    later_f32, refs = refs[:n_later], refs[n_later:]
    (x1_ref, q_ref, k_ref, v_ref, gr_ref, c_ref, edge_ref, z_ref), refs = refs[:8], refs[8:]
    later_bf16, refs = refs[:n_later], refs[n_later:]
    (a_ref, wg_ref, wu_ref, wd_ref, wp_ref, stage_ref, sem_ref), refs = refs[:7], refs[7:]
    later_stage, later_bstage = refs[:n_later], refs[n_later:2 * n_later]
    later_in_sem, later_out_sem = refs[2 * n_later:]

    _convert_weights_across_steps(list(zip(later_f32, later_bf16, later_stage, later_bstage)),
                                  pl.program_id(0), n_steps, later_in_sem, later_out_sem)

    @pl.when(pl.program_id(0) == 0)
    def _():
        d, d_ff = wg_hbm.shape
        z_w = 2 * GATE_RANK
        main = _P_BG
        conv = _P_Z - _P_BG
        _load_weights_bf16(
            [(wg_hbm, 0, d, wg_ref, 0), (wu_hbm, 0, d, wu_ref, 0), (wd_hbm, 0, d_ff, wd_ref, 0),
             (wint_hbm, 0, main, wp_ref, 0), (wint_hbm, main + z_w, conv, wp_ref, _P_BG)],
            stage_ref, sem_ref)
        z_copy = pltpu.make_async_copy(wint_hbm.at[pl.ds(main, z_w), :],
                                       stage_ref.at[0, pl.ds(0, z_w), pl.ds(0, d)], sem_ref.at[0])
        z_copy.start()
        z_copy.wait()
        wp_ref[_P_Z:_P_Z + z_w, :] = stage_ref[0, 0:z_w, 0:d].astype(BF16)
        wp_ref[_P_Z + z_w:_P_END, :] = jnp.zeros((LANES - z_w, d), BF16)

    def stages(rows):
        ffn = {}
        yield from _swiglu_stages(x_ref[rows, :], rows, fn_ref, wg_ref, wu_ref, wd_ref, a_ref, ffn)
        x1 = ffn["y"]
        x1_ref[rows, :] = x1
        yield
        h = _rms(x1, mn_ref[...]).astype(BF16)
        yield
        p = _dot_nt(h, wp_ref[...])
        q_ref[rows, :] = p[:, _P_Q:_P_K] * (GLA_DK ** -0.5)
        k_ref[rows, :] = p[:, _P_K:_P_V]
        v_ref[rows, :] = p[:, _P_V:_P_R].astype(BF16)
        r = p[:, _P_R:_P_BG]
        gr_ref[rows, :] = (gn_ref[...] * (r * jax.nn.sigmoid(r))).astype(BF16)
        bg = p[:, _P_BG:_P_CG]
        u = p[:, _P_CG:_P_XV] * p[:, _P_XV:_P_Z]
        n = u.shape[0]
        row = lax.broadcasted_iota(jnp.int32, (SUBLANES, CONV_W), 0)
        down, up = pltpu.roll(u, 1, 0), pltpu.roll(u, n - 1, 0)
        u_prev = jnp.concatenate([jnp.where(row == 0, 0.0, down[:SUBLANES]), down[SUBLANES:]], axis=0)
        u_next = jnp.concatenate([up[:n - SUBLANES], jnp.where(row == SUBLANES - 1, 0.0, up[n - SUBLANES:])],
                                 axis=0)
        conv = cw_ref[0:1, :] * u_prev + cw_ref[1:2, :] * u + cw_ref[2:3, :] * u_next + cb_ref[...]
        c_ref[rows, :] = (bg * conv).astype(BF16)
        edge0 = rows.start // n * SUBLANES
        edge_ref[edge0:edge0 + SUBLANES, :] = jnp.concatenate(
            [u[0:1], u[n - 1:n], bg[0:1], bg[n - 1:n], jnp.zeros((SUBLANES - 4, CONV_W), F32)], axis=0)
        z_ref[rows, :] = p[:, _P_Z:_P_END].astype(BF16)

    _run_staggered(stages(rows) for rows in _row_slices(x_ref.shape[0]))


def _rows_per_step(n_rows, n_steps):
    r = -(-n_rows // n_steps)
    r += -r % BF16_ROWS
    while n_rows % r:
        r += BF16_ROWS
    return r


def _pre(x, ffn_norm, wg, wu, wd, mix_norm, w_in, gla_norm, conv_w, conv_b, later_weights):
    t, d = x.shape
    d_ff = wg.shape[1]
    assert t % TOKEN_TILE == 0 and d_ff % FFN_CHUNK == 0 and TOKEN_TILE % CHUNK == 0
    n_steps = t // TOKEN_TILE
    tile = lambda w: pl.BlockSpec((TOKEN_TILE, w), lambda i: (i, 0))
    out_w = [(d, F32), (GLA_KEY_W, F32), (GLA_KEY_W, F32), (GLA_VAL_W, BF16), (GLA_VAL_W, BF16), (CONV_W, BF16)]
    edge_rows = TILE_SPLIT * SUBLANES
    out_specs = [tile(w) for w, _ in out_w] + [pl.BlockSpec((edge_rows, CONV_W), lambda i: (i, 0)), tile(LANES)]
    out_shape = [jax.ShapeDtypeStruct((t, w), dt) for w, dt in out_w] + [
        jax.ShapeDtypeStruct((n_steps * edge_rows, CONV_W), F32), jax.ShapeDtypeStruct((t, LANES), BF16)]
    assert w_in.shape[1] == _P_Z + 2 * GATE_RANK
    n_later = len(later_weights)
    later_stage = [(2, _rows_per_step(w.shape[0], n_steps), w.shape[1]) for w in later_weights]
    outs = pl.pallas_call(
        functools.partial(_pre_body, n_later=n_later, n_steps=n_steps),
        grid=(n_steps,),
        in_specs=[tile(d), _const_spec((1, d)), _ANY_SPEC, _ANY_SPEC, _ANY_SPEC, _const_spec((1, d)), _ANY_SPEC,
                  _const_spec((1, GLA_VAL_W)), _const_spec(conv_w.shape), _const_spec((1, CONV_W))]
        + [_ANY_SPEC] * n_later,
        out_specs=out_specs + [_ANY_SPEC] * n_later,
        out_shape=out_shape + [jax.ShapeDtypeStruct(w.shape, BF16) for w in later_weights],
        scratch_shapes=[pltpu.VMEM((TOKEN_TILE, d_ff), BF16),
                        pltpu.VMEM((d, d_ff), BF16), pltpu.VMEM((d, d_ff), BF16), pltpu.VMEM((d_ff, d), BF16),
                        pltpu.VMEM((_P_END, d), BF16),
                        pltpu.VMEM((STAGE_SLOTS, STAGE_ROWS, max(d_ff, d)), F32),
                        pltpu.SemaphoreType.DMA((STAGE_SLOTS,))]
        + [pltpu.VMEM(s, F32) for s in later_stage] + [pltpu.VMEM(s, BF16) for s in later_stage]
        + [pltpu.SemaphoreType.DMA((n_later, 2)), pltpu.SemaphoreType.DMA((n_later, 2))],
        compiler_params=_cparams(1),
        name="pre",
    )(x, ffn_norm.reshape(1, d), wg, wu, wd, mix_norm.reshape(1, d), w_in.T, gla_norm.reshape(1, -1),
      conv_w, conv_b.reshape(1, -1), *later_weights)
    return outs[:len(out_shape)], outs[len(out_shape):]


def _chunk_cumsum(x, reverse):
    n, w = x.shape
    groups = n // SUBLANES
    x3 = x.reshape(groups, SUBLANES, w)
    pos = lax.broadcasted_iota(jnp.int32, x3.shape, 1)
    step = 1
    while step < SUBLANES:
        if reverse:
            x3 = x3 + jnp.where(pos < SUBLANES - step, pltpu.roll(x3, SUBLANES - step, 1), 0.0)
        else:
            x3 = x3 + jnp.where(pos >= step, pltpu.roll(x3, step, 1), 0.0)
        step *= 2
    per = CHUNK // SUBLANES
    x4 = x3.reshape(n // CHUNK, per, SUBLANES, w)
    edge = x4[:, :, 0:1, :] if reverse else x4[:, :, SUBLANES - 1:SUBLANES, :]
    outs = [None] * per
    carry = None
    for j in (range(per - 1, -1, -1) if reverse else range(per)):
        outs[j] = x4[:, j] if carry is None else x4[:, j] + carry
        carry = edge[:, j] if carry is None else carry + edge[:, j]
    return jnp.stack(outs, axis=1).reshape(n, w)


def _gla_chunks(q_ref, k_ref, v_ref, z_ref, wgate_ref, gbias_ref, o_ref, s_ref, *, reverse):
    n_chunks = q_ref.shape[0] // CHUNK
    kw = GLA_KEY_W
    assert GLA_DK == CHUNK and 2 * CHUNK == LANES == GLA_DV and CHUNK & (CHUNK - 1) == 0
    shift = CHUNK.bit_length() - 1
    gz = _dot(z_ref[...], wgate_ref[...]) + gbias_ref[...]

    row_head = lax.broadcasted_iota(jnp.int32, (kw, kw), 0) >> shift
    lane_head = lax.broadcasted_iota(jnp.int32, (kw, kw), 1) >> shift
    same_head = row_head.astype(F32).astype(BF16) == lane_head.astype(F32).astype(BF16)
    pos_q = lax.broadcasted_iota(jnp.int32, (CHUNK, kw), 0)
    pos_k = lax.broadcasted_iota(jnp.int32, (CHUNK, kw), 1) & (CHUNK - 1)
    causal = (pos_k >= pos_q) if reverse else (pos_k <= pos_q)
    low_half = lax.broadcasted_iota(jnp.int32, (CHUNK, LANES), 1) < CHUNK
    low_feat = lax.broadcasted_iota(jnp.int32, (kw, LANES), 1) < CHUNK

    def chunk(c):
        rows = slice(c * CHUNK, (c + 1) * CHUNK)
        g = gz[rows, :]
        log2_gate = (jnp.minimum(g, 0.0) - jnp.log(1.0 + jnp.exp2(jnp.abs(g) * -LOG2_E))) * (LOG2_E / GATE_TAU)
        bc = _chunk_cumsum(log2_gate, reverse)
        q = q_ref[rows, :]
        k = k_ref[rows, :]
        v = v_ref[rows, :]
        b_last = bc[0:1, :] if reverse else bc[CHUNK - 1:CHUNK, :]
        q_in = q * jnp.exp2(bc)
        k_in = (k * jnp.exp2(-bc)).astype(BF16)
        k_st = k * jnp.exp2(b_last - bc)
        k_blk = jnp.where(same_head, jnp.concatenate([k_in] * GLA_HEADS, axis=0), jnp.zeros((), BF16))
        att = _dot_nt(q_in.astype(BF16), k_blk)
        att = jnp.where(causal, att, 0.0)
        decay_rows = jnp.broadcast_to(jnp.exp2(b_last), (SUBLANES, kw))
        pad = jnp.zeros((LANES - CHUNK - SUBLANES, kw), F32)
        xt = jnp.concatenate([k_st, decay_rows, pad], axis=0).T
        kst_t = jnp.where(low_feat, xt, 0.0).astype(BF16)
        yield
        outs = []
        for pair in range(GLA_HEADS // 2):
            lanes = slice(pair * LANES, (pair + 1) * LANES)
            a_col, q_col = att[:, lanes], q_in[:, lanes]
            a_swapped = pltpu.roll(a_col, CHUNK, 1)
            q_swapped = pltpu.roll(q_col, CHUNK, 1)
            top = jnp.concatenate([jnp.where(low_half, a_col, q_swapped),
                                   jnp.where(low_half, a_swapped, q_col)], axis=1).astype(BF16)
            heads = (2 * pair, 2 * pair + 1)
            bottom = jnp.concatenate([kst_t[h * GLA_DK:(h + 1) * GLA_DK, :] for h in heads], axis=1)
            s_in = [s_ref[h] for h in heads]
            zero = jnp.zeros((LANES, GLA_DV), BF16)
            vs = [jnp.concatenate([v[:, h * GLA_DV:(h + 1) * GLA_DV], s.astype(BF16)], axis=0)
                  for h, s in zip(heads, s_in)]
            rhs = jnp.concatenate([jnp.concatenate([vs[0], zero], axis=1),
                                   jnp.concatenate([zero, vs[1]], axis=1)], axis=0)
            res = _dot(jnp.concatenate([top, bottom], axis=0), rhs)
            outs.append(res[:CHUNK, :])
            for i, (h, s) in enumerate(zip(heads, s_in)):
                decay = xt[h * GLA_DK:(h + 1) * GLA_DK, CHUNK:CHUNK + 1]
                s_ref[h] = decay * s + res[CHUNK:, i * GLA_DV:(i + 1) * GLA_DV]
        o_ref[rows, :] = jnp.concatenate(outs, axis=1).astype(o_ref.dtype)

    return [chunk(c) for c in (range(n_chunks - 1, -1, -1) if reverse else range(n_chunks))]


def _run_pipelined(two_stage_gens, depth):
    started = []
    for gen in two_stage_gens:
        next(gen)
        started.append(gen)
        if len(started) > depth:
            for _ in started.pop(0):
                pass
    for gen in started:
        for _ in gen:
            pass


def _gla_body(qf_ref, kf_ref, vf_ref, zf_ref, qb_ref, kb_ref, vb_ref, zb_ref, wgf_ref, wgb_ref, gbf_ref, gbb_ref,
              of_ref, ob_ref, sf_ref, sb_ref):
    @pl.when(pl.program_id(1) == 0)
    def _():
        sf_ref[...] = jnp.zeros_like(sf_ref)
        sb_ref[...] = jnp.zeros_like(sb_ref)

    fwd = _gla_chunks(qf_ref, kf_ref, vf_ref, zf_ref, wgf_ref, gbf_ref, of_ref, sf_ref, reverse=False)
    bwd = _gla_chunks(qb_ref, kb_ref, vb_ref, zb_ref, wgb_ref, gbb_ref, ob_ref, sb_ref, reverse=True)
    _run_pipelined([gen for pair in zip(fwd, bwd) for gen in pair], GLA_PIPELINE_DEPTH)


def _gla(q, k, v, z, wgate_f, wgate_b, gbias_f, gbias_b, batch, seq):
    nb = seq // GLA_BLOCK
    fwd = lambda w: pl.BlockSpec((GLA_BLOCK, w), lambda b, j: (b * nb + j, 0))
    bwd = lambda w: pl.BlockSpec((GLA_BLOCK, w), lambda b, j: (b * nb + nb - 1 - j, 0))
    t = batch * seq
    return pl.pallas_call(
        _gla_body,
        grid=(batch, nb),
        in_specs=[fwd(GLA_KEY_W), fwd(GLA_KEY_W), fwd(GLA_VAL_W), fwd(LANES),
                  bwd(GLA_KEY_W), bwd(GLA_KEY_W), bwd(GLA_VAL_W), bwd(LANES),
                  _const_spec(wgate_f.shape), _const_spec(wgate_b.shape),
                  _const_spec(gbias_f.shape), _const_spec(gbias_b.shape)],
        out_specs=[fwd(GLA_VAL_W), bwd(GLA_VAL_W)],
        out_shape=[jax.ShapeDtypeStruct((t, GLA_VAL_W), BF16)] * 2,
        scratch_shapes=[pltpu.VMEM((GLA_HEADS, GLA_DK, GLA_DV), F32)] * 2,
        compiler_params=_cparams(2),
        name="gla",
    )(q, k, v, z, q, k, v, z, wgate_f, wgate_b, gbias_f, gbias_b)


def _memory_kv(m_ref, n_ref, w_ref, k_ref, v_ref):
    d = m_ref.shape[-1]
    h = _rms(m_ref[...], n_ref[...]).astype(BF16)
    kv = _dot(h, w_ref[...])
    k_ref[...] = kv[:, :d].astype(BF16)
    v_ref[...] = kv[:, d:].astype(BF16)


def _mixed_heads(rows, tile, of_ref, ob_ref, gr_ref, c_ref, edge_prev, edge_own, edge_next, cw_ref, seq):
    tm = c_ref.shape[0]
    n = rows.stop - rows.start
    o = of_ref[rows, :].astype(F32) + ob_ref[rows, :].astype(F32)
    heads = []
    for h in range(GLA_HEADS):
        oh = o[:, h * GLA_DV:(h + 1) * GLA_DV]
        heads.append(oh * lax.rsqrt(jnp.mean(oh * oh, axis=-1, keepdims=True) + EPS))
    a_out = (jnp.concatenate(heads, axis=1) * gr_ref[rows, :].astype(F32)).astype(BF16)
    own = rows.start // n * SUBLANES
    bg_first, bg_last = edge_own[own + 2:own + 3, :], edge_own[own + 3:own + 4, :]
    u_before = edge_prev[1:2, :] if rows.start == 0 else edge_own[own - SUBLANES + 1:own - SUBLANES + 2, :]
    u_after = edge_next[0:1, :] if rows.stop == tm else edge_own[own + SUBLANES:own + SUBLANES + 1, :]
    at_seq_start = (tile * tm + rows.start) % seq == 0
    at_seq_end = (tile * tm + rows.stop) % seq == 0
    fix_first = jnp.where(at_seq_start, 0.0, bg_first * cw_ref[0:1, :] * u_before)
    fix_last = jnp.where(at_seq_end, 0.0, bg_last * cw_ref[2:3, :] * u_after)
    c = c_ref[rows, :]
    row = lax.broadcasted_iota(jnp.int32, (BF16_ROWS, c.shape[1]), 0)
    first = c[:BF16_ROWS].astype(F32) + jnp.where(row == 0, fix_first, 0.0)
    last = c[n - BF16_ROWS:].astype(F32) + jnp.where(row == BF16_ROWS - 1, fix_last, 0.0)
    c_out = jnp.concatenate([first.astype(BF16), c[BF16_ROWS:n - BF16_ROWS], last.astype(BF16)], axis=0)
    return jnp.concatenate([a_out, c_out], axis=1)


def _cross_attention_stages(x, n_ref, wq_ref, k_ref, v_ref, mem_rows, wo_ref, out):
    d = x.shape[-1]
    hd = d // XATTN_HEADS
    h = _rms(x, n_ref[...]).astype(BF16)
    yield
    q = _dot(h, wq_ref[...])
    yield
    heads = [slice(a * hd, (a + 1) * hd) for a in range(XATTN_HEADS)]
    scores = [_dot_nt(q[:, sl].astype(BF16), k_ref[mem_rows, sl]) * (hd ** -0.5) for sl in heads]
    probs = [jnp.exp(s - jnp.max(s, axis=-1, keepdims=True)) for s in scores]
    outs = [_dot(p.astype(BF16), v_ref[mem_rows, sl]) / jnp.sum(p, axis=-1, keepdims=True)
            for p, sl in zip(probs, heads)]
    o = jnp.concatenate(outs, axis=1).astype(BF16)
    yield
    out["y"] = x + _dot(o, wo_ref[...])


def _post_body(*refs, seq, n_batch, final):
    (x_ref, of_ref, ob_ref, gr_ref, c_ref, ep_ref, eo_ref, en_ref, cw_ref, wo_ref,
     xn_ref, wq_ref, mem_ref, mn_ref, wkv_ref, wxo_ref, fn_ref, wg_ref, wu_ref, wd_ref) = refs[:20]
    refs = refs[20:]
    if final:
        last_ref, refs = refs[0], refs[1:]
    o_ref, a_ref, k_ref, v_ref = refs
    tm = x_ref.shape[0]
    i = pl.program_id(0)

    @pl.when(i == 0)
    def _():
        _memory_kv(mem_ref, mn_ref, wkv_ref, k_ref, v_ref)

    n_mem = mem_ref.shape[0] // n_batch
    mem_rows = pl.ds(pl.multiple_of((i * tm // seq) * n_mem, n_mem), n_mem)

    def stages(rows):
        mixed = _mixed_heads(rows, i, of_ref, ob_ref, gr_ref, c_ref, ep_ref, eo_ref, en_ref, cw_ref, seq)
        yield
        x2 = x_ref[rows, :] + _dot(mixed, wo_ref[...])
        yield
        box = {}
        yield from _cross_attention_stages(x2, xn_ref, wq_ref, k_ref, v_ref, mem_rows, wxo_ref, box)
        yield
        x3 = box["y"]
        yield from _swiglu_stages(x3, rows, fn_ref, wg_ref, wu_ref, wd_ref, a_ref, box)
        y = box["y"]
        if final:
            yield
            y = _rms(y, last_ref[...])
        o_ref[rows, :] = y

    _run_staggered(stages(rows) for rows in _row_slices(tm))


def _post(x, o_f, o_b, gr, c_part, edges, conv_w, w_out, xattn_norm, w_q, mem, mem_norm, w_kv, w_xo,
          ffn_norm, wg, wu, wd, seq, final_norm=None):
    t, d = x.shape
    tm = TOKEN_TILE
    d_ff = wg.shape[1]
    n_batch, n_mem, _ = mem.shape
    assert seq % tm == 0 and tm % BF16_ROWS == 0 and d_ff % FFN_CHUNK == 0 and n_mem % BF16_ROWS == 0
    assert seq % (tm // TILE_SPLIT) == 0
    final = final_norm is not None
    tile = lambda w: pl.BlockSpec((tm, w), lambda i: (i, 0))
    n_slices = t // tm * TILE_SPLIT
    edge_prev = pl.BlockSpec((SUBLANES, CONV_W), lambda i: (jnp.maximum(i * TILE_SPLIT - 1, 0), 0))
    edge_own = pl.BlockSpec((TILE_SPLIT * SUBLANES, CONV_W), lambda i: (i, 0))
    edge_next = pl.BlockSpec((SUBLANES, CONV_W), lambda i: (jnp.minimum((i + 1) * TILE_SPLIT, n_slices - 1), 0))
    for w in (w_out, w_q, w_kv, w_xo, wg, wu, wd):
        assert w.dtype == BF16
    mem_rows = n_batch * n_mem
    in_specs = [tile(d), tile(GLA_VAL_W), tile(GLA_VAL_W), tile(GLA_VAL_W), tile(CONV_W),
                edge_prev, edge_own, edge_next, _const_spec(conv_w.shape), _const_spec(w_out.shape),
                _const_spec((1, d)), _const_spec(w_q.shape), _const_spec((mem_rows, d)), _const_spec((1, d)),
                _const_spec(w_kv.shape), _const_spec(w_xo.shape),
                _const_spec((1, d)), _const_spec(wg.shape), _const_spec(wu.shape), _const_spec(wd.shape)]
    args = [x, o_f, o_b, gr, c_part, edges, edges, edges, conv_w, w_out,
            xattn_norm.reshape(1, d), w_q, mem.reshape(mem_rows, d), mem_norm.reshape(1, d), w_kv, w_xo,
            ffn_norm.reshape(1, d), wg, wu, wd]
    if final:
        in_specs.append(_const_spec((1, d)))
        args.append(final_norm.reshape(1, d))
    return pl.pallas_call(
        functools.partial(_post_body, seq=seq, n_batch=n_batch, final=final),
        grid=(t // tm,),
        in_specs=in_specs,
        out_specs=tile(d),
        out_shape=jax.ShapeDtypeStruct((t, d), F32),
        scratch_shapes=[pltpu.VMEM((tm, d_ff), BF16), pltpu.VMEM((mem_rows, d), BF16),
                        pltpu.VMEM((mem_rows, d), BF16)],
        compiler_params=_cparams(1),
        name="post_final" if final else "post",
    )(*args)


def _pack_gate(gate_w, gate_b, z_row0):
    w = jnp.pad(gate_w, ((z_row0, LANES - z_row0 - GATE_RANK), (0, 0)))
    return w.astype(BF16), gate_b.reshape(1, -1).astype(F32)


def kernel(x, mem, ffn1_norm, ffn1_w_gate, ffn1_w_up, ffn1_w_down, mix_norm, w_in, gate_fwd_w, gate_fwd_b,
           gate_bwd_w, gate_bwd_b, gla_norm, conv_w, conv_b, w_out, xattn_norm, mem_norm, xattn_w_q,
           xattn_w_kv, xattn_w_o, ffn2_norm, ffn2_w_gate, ffn2_w_up, ffn2_w_down, final_norm):
    batch, seq, d = x.shape
    depth = w_in.shape[0]
    assert seq % GLA_BLOCK == 0 and GLA_BLOCK % CHUNK == 0 and seq % TOKEN_TILE == 0
    h = x.reshape(batch * seq, d)
    for l in range(depth):
        post_weights = [w_out[l], xattn_w_q[l], xattn_w_kv[l], xattn_w_o[l],
                        ffn2_w_gate[l], ffn2_w_up[l], ffn2_w_down[l]]
        (h, q, k, v, gr, c_part, edges, z), (w_o, w_q, w_kv, w_xo, wg2, wu2, wd2) = _pre(
            h, ffn1_norm[l], ffn1_w_gate[l], ffn1_w_up[l], ffn1_w_down[l], mix_norm[l], w_in[l], gla_norm[l],
            conv_w[l], conv_b[l], post_weights)
        wgate_f, gbias_f = _pack_gate(gate_fwd_w[l], gate_fwd_b[l], 0)
        wgate_b, gbias_b = _pack_gate(gate_bwd_w[l], gate_bwd_b[l], GATE_RANK)
        o_f, o_b = _gla(q, k, v, z, wgate_f, wgate_b, gbias_f, gbias_b, batch, seq)
        h = _post(h, o_f, o_b, gr, c_part, edges, conv_w[l], w_o,
                  xattn_norm[l], w_q, mem, mem_norm[l], w_kv, w_xo, ffn2_norm[l], wg2, wu2, wd2, seq,
                  final_norm=final_norm if l == depth - 1 else None)
    return h.reshape(batch, seq, d)
```

```python
import functools

import jax
import jax.numpy as jnp
from jax import lax
from jax.experimental import pallas as pl
from jax.experimental.pallas import tpu as pltpu

F32 = jnp.float32
BF16 = jnp.bfloat16

XATTN_HEADS = 4
GLA_HEADS = 4
GLA_DK = 64
GLA_DV = 128
GLA_KEY_W = GLA_HEADS * GLA_DK
GLA_VAL_W = GLA_HEADS * GLA_DV
GATE_RANK = 16
GATE_TAU = 16.0
CHUNK = 64
CONV_W = 512
EPS = 1e-6
LOG2_E = 1.4426950408889634

LANES = 128
SUBLANES = 8
BF16_ROWS = 16
TOKEN_TILE = 512
TILE_SPLIT = 2
GLA_BLOCK = 1024
GLA_PIPELINE_DEPTH = 2
FFN_CHUNK = 256
STAGE_ROWS = 128
STAGE_SLOTS = 6
VMEM_LIMIT_BYTES = 58 * 1024 * 1024


def _cparams(n_axes):
    return pltpu.CompilerParams(
        dimension_semantics=("arbitrary",) * n_axes,
        vmem_limit_bytes=VMEM_LIMIT_BYTES,
    )


def _const_spec(shape):
    nd = len(shape)
    return pl.BlockSpec(shape, lambda *_: (0,) * nd, pipeline_mode=pl.Buffered(1))


_ANY_SPEC = pl.BlockSpec(memory_space=pl.ANY)


def _load_weights_bf16(jobs, stage_ref, sem_ref):
    n_slots, rows = stage_ref.shape[0], stage_ref.shape[1]
    ahead = n_slots - 1
    counts = [n_rows // rows for _, _, n_rows, _, _ in jobs]
    for (src, _, n_rows, dst, _), n in zip(jobs, counts):
        assert n_rows % rows == 0 and n >= ahead and src.shape[1] == dst.shape[1] <= stage_ref.shape[2]
    bases = [sum(counts[:j]) for j in range(len(jobs))]
    stream = [(j, c) for j, n in enumerate(counts) for c in range(n)]

    def copy(j, c, g):
        src, src_row0, _, _, _ = jobs[j]
        r0 = pl.multiple_of(src_row0 + c * rows, SUBLANES)
        return pltpu.make_async_copy(src.at[pl.ds(r0, rows), :],
                                     stage_ref.at[g % n_slots, :, pl.ds(0, src.shape[1])], sem_ref.at[g % n_slots])

    def finish(j, c, g):
        _, _, _, dst, dst_row0 = jobs[j]
        copy(j, c, g).wait()
        r0 = pl.multiple_of(dst_row0 + c * rows, BF16_ROWS)
        dst[pl.ds(r0, rows), :] = stage_ref[g % n_slots, :, 0:dst.shape[1]].astype(BF16)

    for g in range(ahead):
        copy(*stream[g], g).start()
    for j, n in enumerate(counts):
        def body(c, carry, j=j):
            copy(j, c + ahead, bases[j] + c + ahead).start()
            finish(j, c, bases[j] + c)
            return carry

        lax.fori_loop(0, n - ahead, body, 0)
        for c in range(n - ahead, n):
            g = bases[j] + c
            if g + ahead < len(stream):
                copy(*stream[g + ahead], g + ahead).start()
            finish(j, c, g)


def _convert_weights_across_steps(jobs, step, n_steps, in_sem, out_sem):
    def chunks(job):
        return job[0].shape[0] // job[2].shape[1]

    def copy_in(j, c, slot):
        src, _, stage, _ = jobs[j]
        r = stage.shape[1]
        return pltpu.make_async_copy(src.at[pl.ds(pl.multiple_of(c * r, r), r), :], stage.at[slot],
                                     in_sem.at[j, slot])

    def copy_out(j, c, slot):
        _, dst, stage, bstage = jobs[j]
        r = stage.shape[1]
        return pltpu.make_async_copy(bstage.at[slot], dst.at[pl.ds(pl.multiple_of(c * r, r), r), :],
                                     out_sem.at[j, slot])

    for job in jobs:
        assert job[0].shape[0] % job[2].shape[1] == 0 and 2 <= chunks(job) <= n_steps
    slot = step % 2
    for n in sorted({chunks(job) for job in jobs}):
        group = [j for j, job in enumerate(jobs) if chunks(job) == n]

        @pl.when(step == 0)
        def _():
            for j in group:
                copy_in(j, 0, 0).start()

        @pl.when(step + 1 < n)
        def _():
            for j in group:
                copy_in(j, step + 1, 1 - slot).start()

        @pl.when(jnp.logical_and(step >= 2, step - 2 < n))
        def _():
            for j in group:
                copy_out(j, step - 2, slot).wait()

        @pl.when(step < n)
        def _():
            for j in group:
                _, _, stage, bstage = jobs[j]
                copy_in(j, step, slot).wait()
                bstage[slot] = stage[slot].astype(BF16)
                copy_out(j, step, slot).start()

        late = [c for c in (n_steps - 2, n_steps - 1) if c < n]
        if late:
            @pl.when(step == n_steps - 1)
            def _():
                for j in group:
                    for c in late:
                        copy_out(j, c, c % 2).wait()


def _rms(x, g):
    ms = jnp.mean(x * x, axis=-1, keepdims=True)
    return x * lax.rsqrt(ms + EPS) * g


def _dot(a, b):
    return jnp.dot(a, b, preferred_element_type=F32)


def _dot_nt(a, b):
    return lax.dot_general(a, b, (((1,), (1,)), ((), ())), preferred_element_type=F32)


def _row_slices(tile_rows):
    assert tile_rows % (TILE_SPLIT * BF16_ROWS) == 0
    part = tile_rows // TILE_SPLIT
    return [slice(i * part, (i + 1) * part) for i in range(TILE_SPLIT)]


def _run_staggered(stage_gens):
    active = list(stage_gens)
    next(active[0])
    while active:
        for gen in list(active):
            try:
                next(gen)
            except StopIteration:
                active.remove(gen)


def _swiglu_stages(x, rows, n_ref, wg_ref, wu_ref, wd_ref, a_ref, out):
    h = _rms(x, n_ref[...]).astype(BF16)
    yield
    for c in range(wg_ref.shape[1] // FFN_CHUNK):
        sl = slice(c * FFN_CHUNK, (c + 1) * FFN_CHUNK)
        g = _dot(h, wg_ref[:, sl])
        u = _dot(h, wu_ref[:, sl])
        a_ref[rows, sl] = (g * jax.nn.sigmoid(g) * u).astype(BF16)
    yield
    out["y"] = x + 0.5 * _dot(a_ref[rows, :], wd_ref[...])


_P_Q, _P_K, _P_V, _P_R = 0, GLA_KEY_W, 2 * GLA_KEY_W, 2 * GLA_KEY_W + GLA_VAL_W
_P_BG = _P_R + GLA_VAL_W
_P_CG = _P_BG + CONV_W
_P_XV = _P_CG + CONV_W
_P_Z = _P_XV + CONV_W
_P_END = _P_Z + LANES


def _pre_body(*refs, n_later, n_steps):
    (x_ref, fn_ref, wg_hbm, wu_hbm, wd_hbm, mn_ref, wint_hbm, gn_ref, cw_ref, cb_ref), refs = refs[:10], refs[10:]
    later_f32, refs = refs[:n_later], refs[n_later:]
    (x1_ref, q_ref, k_ref, v_ref, gr_ref, c_ref, edge_ref, z_ref), refs = refs[:8], refs[8:]
    later_bf16, refs = refs[:n_later], refs[n_later:]
    (a_ref, wg_ref, wu_ref, wd_ref, wp_ref, stage_ref, sem_ref), refs = refs[:7], refs[7:]
    later_stage, later_bstage = refs[:n_later], refs[n_later:2 * n_later]
    later_in_sem, later_out_sem = refs[2 * n_later:]

    _convert_weights_across_steps(list(zip(later_f32, later_bf16, later_stage, later_bstage)),
                                  pl.program_id(0), n_steps, later_in_sem, later_out_sem)

    @pl.when(pl.program_id(0) == 0)
    def _():
        d, d_ff = wg_hbm.shape
        z_w = 2 * GATE_RANK
        main = _P_BG
        conv = _P_Z - _P_BG
        _load_weights_bf16(
            [(wg_hbm, 0, d, wg_ref, 0), (wu_hbm, 0, d, wu_ref, 0), (wd_hbm, 0, d_ff, wd_ref, 0),
             (wint_hbm, 0, main, wp_ref, 0), (wint_hbm, main + z_w, conv, wp_ref, _P_BG)],
            stage_ref, sem_ref)
        z_copy = pltpu.make_async_copy(wint_hbm.at[pl.ds(main, z_w), :],
                                       stage_ref.at[0, pl.ds(0, z_w), pl.ds(0, d)], sem_ref.at[0])
        z_copy.start()
        z_copy.wait()
        wp_ref[_P_Z:_P_Z + z_w, :] = stage_ref[0, 0:z_w, 0:d].astype(BF16)
        wp_ref[_P_Z + z_w:_P_END, :] = jnp.zeros((LANES - z_w, d), BF16)

    def stages(rows):
        ffn = {}
        yield from _swiglu_stages(x_ref[rows, :], rows, fn_ref, wg_ref, wu_ref, wd_ref, a_ref, ffn)
        x1 = ffn["y"]
        x1_ref[rows, :] = x1
        yield
        h = _rms(x1, mn_ref[...]).astype(BF16)
        yield
        p = _dot_nt(h, wp_ref[_P_R:_P_END, :])
        col = lambda start, stop: slice(start - _P_R, stop - _P_R)
        r = p[:, col(_P_R, _P_BG)]
        gr_ref[rows, :] = (gn_ref[...] * (r * jax.nn.sigmoid(r))).astype(BF16)
        bg = p[:, col(_P_BG, _P_CG)]
        u = p[:, col(_P_CG, _P_XV)] * p[:, col(_P_XV, _P_Z)]
        n = u.shape[0]
        row = lax.broadcasted_iota(jnp.int32, (SUBLANES, CONV_W), 0)
        down, up = pltpu.roll(u, 1, 0), pltpu.roll(u, n - 1, 0)
        u_prev = jnp.concatenate([jnp.where(row == 0, 0.0, down[:SUBLANES]), down[SUBLANES:]], axis=0)
        u_next = jnp.concatenate([up[:n - SUBLANES], jnp.where(row == SUBLANES - 1, 0.0, up[n - SUBLANES:])],
                                 axis=0)
        conv = cw_ref[0:1, :] * u_prev + cw_ref[1:2, :] * u + cw_ref[2:3, :] * u_next + cb_ref[...]
        c_ref[rows, :] = (bg * conv).astype(BF16)
        edge0 = rows.start // n * SUBLANES
        edge_ref[edge0:edge0 + SUBLANES, :] = jnp.concatenate(
            [u[0:1], u[n - 1:n], bg[0:1], bg[n - 1:n], jnp.zeros((SUBLANES - 4, CONV_W), F32)], axis=0)
        z_ref[rows, :] = p[:, col(_P_Z, _P_END)].astype(BF16)
        yield
        p = _dot_nt(h, wp_ref[0:_P_R, :])
        q_ref[rows, :] = p[:, _P_Q:_P_K] * (GLA_DK ** -0.5)
        k_ref[rows, :] = p[:, _P_K:_P_V]
        v_ref[rows, :] = p[:, _P_V:_P_R].astype(BF16)

    _run_staggered(stages(rows) for rows in _row_slices(x_ref.shape[0]))


def _rows_per_step(n_rows, n_steps):
    r = -(-n_rows // n_steps)
    r += -r % BF16_ROWS
    while n_rows % r:
        r += BF16_ROWS
    return r


def _pre(x, ffn_norm, wg, wu, wd, mix_norm, w_in, gla_norm, conv_w, conv_b, later_weights):
    t, d = x.shape
    d_ff = wg.shape[1]
    assert t % TOKEN_TILE == 0 and d_ff % FFN_CHUNK == 0 and TOKEN_TILE % CHUNK == 0
    n_steps = t // TOKEN_TILE
    tile = lambda w: pl.BlockSpec((TOKEN_TILE, w), lambda i: (i, 0))
    out_w = [(d, F32), (GLA_KEY_W, F32), (GLA_KEY_W, F32), (GLA_VAL_W, BF16), (GLA_VAL_W, BF16), (CONV_W, BF16)]
    edge_rows = TILE_SPLIT * SUBLANES
    out_specs = [tile(w) for w, _ in out_w] + [pl.BlockSpec((edge_rows, CONV_W), lambda i: (i, 0)), tile(LANES)]
    out_shape = [jax.ShapeDtypeStruct((t, w), dt) for w, dt in out_w] + [
        jax.ShapeDtypeStruct((n_steps * edge_rows, CONV_W), F32), jax.ShapeDtypeStruct((t, LANES), BF16)]
    assert w_in.shape[1] == _P_Z + 2 * GATE_RANK
    n_later = len(later_weights)
    later_stage = [(2, _rows_per_step(w.shape[0], n_steps), w.shape[1]) for w in later_weights]
    outs = pl.pallas_call(
        functools.partial(_pre_body, n_later=n_later, n_steps=n_steps),
        grid=(n_steps,),
        in_specs=[tile(d), _const_spec((1, d)), _ANY_SPEC, _ANY_SPEC, _ANY_SPEC, _const_spec((1, d)), _ANY_SPEC,
                  _const_spec((1, GLA_VAL_W)), _const_spec(conv_w.shape), _const_spec((1, CONV_W))]
        + [_ANY_SPEC] * n_later,
        out_specs=out_specs + [_ANY_SPEC] * n_later,
        out_shape=out_shape + [jax.ShapeDtypeStruct(w.shape, BF16) for w in later_weights],
        scratch_shapes=[pltpu.VMEM((TOKEN_TILE, d_ff), BF16),
                        pltpu.VMEM((d, d_ff), BF16), pltpu.VMEM((d, d_ff), BF16), pltpu.VMEM((d_ff, d), BF16),
                        pltpu.VMEM((_P_END, d), BF16),
                        pltpu.VMEM((STAGE_SLOTS, STAGE_ROWS, max(d_ff, d)), F32),
                        pltpu.SemaphoreType.DMA((STAGE_SLOTS,))]
        + [pltpu.VMEM(s, F32) for s in later_stage] + [pltpu.VMEM(s, BF16) for s in later_stage]
        + [pltpu.SemaphoreType.DMA((n_later, 2)), pltpu.SemaphoreType.DMA((n_later, 2))],
        compiler_params=_cparams(1),
        name="pre",
    )(x, ffn_norm.reshape(1, d), wg, wu, wd, mix_norm.reshape(1, d), w_in.T, gla_norm.reshape(1, -1),
      conv_w, conv_b.reshape(1, -1), *later_weights)
    return outs[:len(out_shape)], outs[len(out_shape):]


def _chunk_cumsum(x, reverse):
    n, w = x.shape
    groups = n // SUBLANES
    x3 = x.reshape(groups, SUBLANES, w)
    pos = lax.broadcasted_iota(jnp.int32, x3.shape, 1)
    step = 1
    while step < SUBLANES:
        if reverse:
            x3 = x3 + jnp.where(pos < SUBLANES - step, pltpu.roll(x3, SUBLANES - step, 1), 0.0)
        else:
            x3 = x3 + jnp.where(pos >= step, pltpu.roll(x3, step, 1), 0.0)
        step *= 2
    per = CHUNK // SUBLANES
    x4 = x3.reshape(n // CHUNK, per, SUBLANES, w)
    edge = x4[:, :, 0:1, :] if reverse else x4[:, :, SUBLANES - 1:SUBLANES, :]
    outs = [None] * per
    carry = None
    for j in (range(per - 1, -1, -1) if reverse else range(per)):
        outs[j] = x4[:, j] if carry is None else x4[:, j] + carry
        carry = edge[:, j] if carry is None else carry + edge[:, j]
    return jnp.stack(outs, axis=1).reshape(n, w)


def _gla_chunks(q_ref, k_ref, v_ref, z_ref, wgate_ref, gbias_ref, o_ref, s_ref, *, reverse):
    n_chunks = q_ref.shape[0] // CHUNK
    kw = GLA_KEY_W
    assert GLA_DK == CHUNK and 2 * CHUNK == LANES == GLA_DV and CHUNK & (CHUNK - 1) == 0
    shift = CHUNK.bit_length() - 1
    gz = _dot(z_ref[...], wgate_ref[...]) + gbias_ref[...]

    row_head = lax.broadcasted_iota(jnp.int32, (kw, kw), 0) >> shift
    lane_head = lax.broadcasted_iota(jnp.int32, (kw, kw), 1) >> shift
    same_head = row_head.astype(F32).astype(BF16) == lane_head.astype(F32).astype(BF16)
    pos_q = lax.broadcasted_iota(jnp.int32, (CHUNK, kw), 0)
    pos_k = lax.broadcasted_iota(jnp.int32, (CHUNK, kw), 1) & (CHUNK - 1)
    causal = (pos_k >= pos_q) if reverse else (pos_k <= pos_q)
    low_half = lax.broadcasted_iota(jnp.int32, (CHUNK, LANES), 1) < CHUNK
    low_feat = lax.broadcasted_iota(jnp.int32, (kw, LANES), 1) < CHUNK

    def chunk(c):
        rows = slice(c * CHUNK, (c + 1) * CHUNK)
        g = gz[rows, :]
        log2_gate = (jnp.minimum(g, 0.0) - jnp.log(1.0 + jnp.exp2(jnp.abs(g) * -LOG2_E))) * (LOG2_E / GATE_TAU)
        bc = _chunk_cumsum(log2_gate, reverse)
        q = q_ref[rows, :]
        k = k_ref[rows, :]
        v = v_ref[rows, :]
        b_last = bc[0:1, :] if reverse else bc[CHUNK - 1:CHUNK, :]
        q_in = q * jnp.exp2(bc)
        k_in = (k * jnp.exp2(-bc)).astype(BF16)
        k_st = k * jnp.exp2(b_last - bc)
        k_blk = jnp.where(same_head, jnp.concatenate([k_in] * GLA_HEADS, axis=0), jnp.zeros((), BF16))
        att = _dot_nt(q_in.astype(BF16), k_blk)
        att = jnp.where(causal, att, 0.0)
        decay_rows = jnp.broadcast_to(jnp.exp2(b_last), (SUBLANES, kw))
        pad = jnp.zeros((LANES - CHUNK - SUBLANES, kw), F32)
        xt = jnp.concatenate([k_st, decay_rows, pad], axis=0).T
        kst_t = jnp.where(low_feat, xt, 0.0).astype(BF16)
        yield
        outs = []
        for pair in range(GLA_HEADS // 2):
            lanes = slice(pair * LANES, (pair + 1) * LANES)
            a_col, q_col = att[:, lanes], q_in[:, lanes]
            a_swapped = pltpu.roll(a_col, CHUNK, 1)
            q_swapped = pltpu.roll(q_col, CHUNK, 1)
            top = jnp.concatenate([jnp.where(low_half, a_col, q_swapped),
                                   jnp.where(low_half, a_swapped, q_col)], axis=1).astype(BF16)
            heads = (2 * pair, 2 * pair + 1)
            bottom = jnp.concatenate([kst_t[h * GLA_DK:(h + 1) * GLA_DK, :] for h in heads], axis=1)
            s_in = [s_ref[h] for h in heads]
            zero = jnp.zeros((LANES, GLA_DV), BF16)
            vs = [jnp.concatenate([v[:, h * GLA_DV:(h + 1) * GLA_DV], s.astype(BF16)], axis=0)
                  for h, s in zip(heads, s_in)]
            rhs = jnp.concatenate([jnp.concatenate([vs[0], zero], axis=1),
                                   jnp.concatenate([zero, vs[1]], axis=1)], axis=0)
            res = _dot(jnp.concatenate([top, bottom], axis=0), rhs)
            outs.append(res[:CHUNK, :])
            for i, (h, s) in enumerate(zip(heads, s_in)):
                decay = xt[h * GLA_DK:(h + 1) * GLA_DK, CHUNK:CHUNK + 1]
                s_ref[h] = decay * s + res[CHUNK:, i * GLA_DV:(i + 1) * GLA_DV]
        o_ref[rows, :] = jnp.concatenate(outs, axis=1).astype(o_ref.dtype)

    return [chunk(c) for c in (range(n_chunks - 1, -1, -1) if reverse else range(n_chunks))]


def _run_pipelined(two_stage_gens, depth):
    started = []
    for gen in two_stage_gens:
        next(gen)
        started.append(gen)
        if len(started) > depth:
            for _ in started.pop(0):
                pass
    for gen in started:
        for _ in gen:
            pass


def _gla_body(qf_ref, kf_ref, vf_ref, zf_ref, qb_ref, kb_ref, vb_ref, zb_ref, wgf_ref, wgb_ref, gbf_ref, gbb_ref,
              of_ref, ob_ref, sf_ref, sb_ref):
    @pl.when(pl.program_id(1) == 0)
    def _():
        sf_ref[...] = jnp.zeros_like(sf_ref)
        sb_ref[...] = jnp.zeros_like(sb_ref)

    fwd = _gla_chunks(qf_ref, kf_ref, vf_ref, zf_ref, wgf_ref, gbf_ref, of_ref, sf_ref, reverse=False)
    bwd = _gla_chunks(qb_ref, kb_ref, vb_ref, zb_ref, wgb_ref, gbb_ref, ob_ref, sb_ref, reverse=True)
    _run_pipelined([gen for pair in zip(fwd, bwd) for gen in pair], GLA_PIPELINE_DEPTH)


def _gla(q, k, v, z, wgate_f, wgate_b, gbias_f, gbias_b, batch, seq):
    nb = seq // GLA_BLOCK
    fwd = lambda w: pl.BlockSpec((GLA_BLOCK, w), lambda b, j: (b * nb + j, 0))
    bwd = lambda w: pl.BlockSpec((GLA_BLOCK, w), lambda b, j: (b * nb + nb - 1 - j, 0))
    t = batch * seq
    return pl.pallas_call(
        _gla_body,
        grid=(batch, nb),
        in_specs=[fwd(GLA_KEY_W), fwd(GLA_KEY_W), fwd(GLA_VAL_W), fwd(LANES),
                  bwd(GLA_KEY_W), bwd(GLA_KEY_W), bwd(GLA_VAL_W), bwd(LANES),
                  _const_spec(wgate_f.shape), _const_spec(wgate_b.shape),
                  _const_spec(gbias_f.shape), _const_spec(gbias_b.shape)],
        out_specs=[fwd(GLA_VAL_W), bwd(GLA_VAL_W)],
        out_shape=[jax.ShapeDtypeStruct((t, GLA_VAL_W), BF16)] * 2,
        scratch_shapes=[pltpu.VMEM((GLA_HEADS, GLA_DK, GLA_DV), F32)] * 2,
        compiler_params=_cparams(2),
        name="gla",
    )(q, k, v, z, q, k, v, z, wgate_f, wgate_b, gbias_f, gbias_b)


def _memory_kv(m_ref, n_ref, w_ref, k_ref, v_ref):
    d = m_ref.shape[-1]
    h = _rms(m_ref[...], n_ref[...]).astype(BF16)
    kv = _dot(h, w_ref[...])
    k_ref[...] = kv[:, :d].astype(BF16)
    v_ref[...] = kv[:, d:].astype(BF16)


def _mixed_heads(rows, tile, of_ref, ob_ref, gr_ref, c_ref, edge_prev, edge_own, edge_next, cw_ref, seq):
    tm = c_ref.shape[0]
    n = rows.stop - rows.start
    o = of_ref[rows, :].astype(F32) + ob_ref[rows, :].astype(F32)
    heads = []
    for h in range(GLA_HEADS):
        oh = o[:, h * GLA_DV:(h + 1) * GLA_DV]
        heads.append(oh * lax.rsqrt(jnp.mean(oh * oh, axis=-1, keepdims=True) + EPS))
    a_out = (jnp.concatenate(heads, axis=1) * gr_ref[rows, :].astype(F32)).astype(BF16)
    own = rows.start // n * SUBLANES
    bg_first, bg_last = edge_own[own + 2:own + 3, :], edge_own[own + 3:own + 4, :]
    u_before = edge_prev[1:2, :] if rows.start == 0 else edge_own[own - SUBLANES + 1:own - SUBLANES + 2, :]
    u_after = edge_next[0:1, :] if rows.stop == tm else edge_own[own + SUBLANES:own + SUBLANES + 1, :]
    at_seq_start = (tile * tm + rows.start) % seq == 0
    at_seq_end = (tile * tm + rows.stop) % seq == 0
    fix_first = jnp.where(at_seq_start, 0.0, bg_first * cw_ref[0:1, :] * u_before)
    fix_last = jnp.where(at_seq_end, 0.0, bg_last * cw_ref[2:3, :] * u_after)
    c = c_ref[rows, :]
    row = lax.broadcasted_iota(jnp.int32, (BF16_ROWS, c.shape[1]), 0)
    first = c[:BF16_ROWS].astype(F32) + jnp.where(row == 0, fix_first, 0.0)
    last = c[n - BF16_ROWS:].astype(F32) + jnp.where(row == BF16_ROWS - 1, fix_last, 0.0)
    c_out = jnp.concatenate([first.astype(BF16), c[BF16_ROWS:n - BF16_ROWS], last.astype(BF16)], axis=0)
    return jnp.concatenate([a_out, c_out], axis=1)


def _cross_attention_stages(x, n_ref, wq_ref, k_ref, v_ref, mem_rows, wo_ref, out):
    d = x.shape[-1]
    hd = d // XATTN_HEADS
    h = _rms(x, n_ref[...]).astype(BF16)
    yield
    q = _dot(h, wq_ref[...])
    yield
    heads = [slice(a * hd, (a + 1) * hd) for a in range(XATTN_HEADS)]
    scores = [_dot_nt(q[:, sl].astype(BF16), k_ref[mem_rows, sl]) * (hd ** -0.5) for sl in heads]
    probs = [jnp.exp(s - jnp.max(s, axis=-1, keepdims=True)) for s in scores]
    outs = [_dot(p.astype(BF16), v_ref[mem_rows, sl]) / jnp.sum(p, axis=-1, keepdims=True)
            for p, sl in zip(probs, heads)]
    o = jnp.concatenate(outs, axis=1).astype(BF16)
    yield
    out["y"] = x + _dot(o, wo_ref[...])


def _post_body(*refs, seq, n_batch, final):
    (x_ref, of_ref, ob_ref, gr_ref, c_ref, ep_ref, eo_ref, en_ref, cw_ref, wo_ref,
     xn_ref, wq_ref, mem_ref, mn_ref, wkv_ref, wxo_ref, fn_ref, wg_ref, wu_ref, wd_ref) = refs[:20]
    refs = refs[20:]
    if final:
        last_ref, refs = refs[0], refs[1:]
    o_ref, a_ref, k_ref, v_ref = refs
    tm = x_ref.shape[0]
    i = pl.program_id(0)

    @pl.when(i == 0)
    def _():
        _memory_kv(mem_ref, mn_ref, wkv_ref, k_ref, v_ref)

    n_mem = mem_ref.shape[0] // n_batch
    mem_rows = pl.ds(pl.multiple_of((i * tm // seq) * n_mem, n_mem), n_mem)

    def stages(rows):
        mixed = _mixed_heads(rows, i, of_ref, ob_ref, gr_ref, c_ref, ep_ref, eo_ref, en_ref, cw_ref, seq)
        yield
        x2 = x_ref[rows, :] + _dot(mixed, wo_ref[...])
        yield
        box = {}
        yield from _cross_attention_stages(x2, xn_ref, wq_ref, k_ref, v_ref, mem_rows, wxo_ref, box)
        yield
        x3 = box["y"]
        yield from _swiglu_stages(x3, rows, fn_ref, wg_ref, wu_ref, wd_ref, a_ref, box)
        y = box["y"]
        if final:
            yield
            y = _rms(y, last_ref[...])
        o_ref[rows, :] = y

    _run_staggered(stages(rows) for rows in _row_slices(tm))


def _post(x, o_f, o_b, gr, c_part, edges, conv_w, w_out, xattn_norm, w_q, mem, mem_norm, w_kv, w_xo,
          ffn_norm, wg, wu, wd, seq, final_norm=None):
    t, d = x.shape
    tm = TOKEN_TILE
    d_ff = wg.shape[1]
    n_batch, n_mem, _ = mem.shape
    assert seq % tm == 0 and tm % BF16_ROWS == 0 and d_ff % FFN_CHUNK == 0 and n_mem % BF16_ROWS == 0
    assert seq % (tm // TILE_SPLIT) == 0
    final = final_norm is not None
    tile = lambda w: pl.BlockSpec((tm, w), lambda i: (i, 0))
    n_slices = t // tm * TILE_SPLIT
    edge_prev = pl.BlockSpec((SUBLANES, CONV_W), lambda i: (jnp.maximum(i * TILE_SPLIT - 1, 0), 0))
    edge_own = pl.BlockSpec((TILE_SPLIT * SUBLANES, CONV_W), lambda i: (i, 0))
    edge_next = pl.BlockSpec((SUBLANES, CONV_W), lambda i: (jnp.minimum((i + 1) * TILE_SPLIT, n_slices - 1), 0))
    for w in (w_out, w_q, w_kv, w_xo, wg, wu, wd):
        assert w.dtype == BF16
    mem_rows = n_batch * n_mem
    in_specs = [tile(d), tile(GLA_VAL_W), tile(GLA_VAL_W), tile(GLA_VAL_W), tile(CONV_W),
                edge_prev, edge_own, edge_next, _const_spec(conv_w.shape), _const_spec(w_out.shape),
                _const_spec((1, d)), _const_spec(w_q.shape), _const_spec((mem_rows, d)), _const_spec((1, d)),
                _const_spec(w_kv.shape), _const_spec(w_xo.shape),
                _const_spec((1, d)), _const_spec(wg.shape), _const_spec(wu.shape), _const_spec(wd.shape)]
    args = [x, o_f, o_b, gr, c_part, edges, edges, edges, conv_w, w_out,
            xattn_norm.reshape(1, d), w_q, mem.reshape(mem_rows, d), mem_norm.reshape(1, d), w_kv, w_xo,
            ffn_norm.reshape(1, d), wg, wu, wd]
    if final:
        in_specs.append(_const_spec((1, d)))
        args.append(final_norm.reshape(1, d))
    return pl.pallas_call(
        functools.partial(_post_body, seq=seq, n_batch=n_batch, final=final),
        grid=(t // tm,),
        in_specs=in_specs,
        out_specs=tile(d),
        out_shape=jax.ShapeDtypeStruct((t, d), F32),
        scratch_shapes=[pltpu.VMEM((tm, d_ff), BF16), pltpu.VMEM((mem_rows, d), BF16),
                        pltpu.VMEM((mem_rows, d), BF16)],
        compiler_params=_cparams(1),
        name="post_final" if final else "post",
    )(*args)


def _pack_gate(gate_w, gate_b, z_row0):
    w = jnp.pad(gate_w, ((z_row0, LANES - z_row0 - GATE_RANK), (0, 0)))
    return w.astype(BF16), gate_b.reshape(1, -1).astype(F32)


def kernel(x, mem, ffn1_norm, ffn1_w_gate, ffn1_w_up, ffn1_w_down, mix_norm, w_in, gate_fwd_w, gate_fwd_b,
           gate_bwd_w, gate_bwd_b, gla_norm, conv_w, conv_b, w_out, xattn_norm, mem_norm, xattn_w_q,
           xattn_w_kv, xattn_w_o, ffn2_norm, ffn2_w_gate, ffn2_w_up, ffn2_w_down, final_norm):
    batch, seq, d = x.shape
    depth = w_in.shape[0]
    assert seq % GLA_BLOCK == 0 and GLA_BLOCK % CHUNK == 0 and seq % TOKEN_TILE == 0
    h = x.reshape(batch * seq, d)
    for l in range(depth):
        post_weights = [w_out[l], xattn_w_q[l], xattn_w_kv[l], xattn_w_o[l],
                        ffn2_w_gate[l], ffn2_w_up[l], ffn2_w_down[l]]
        (h, q, k, v, gr, c_part, edges, z), (w_o, w_q, w_kv, w_xo, wg2, wu2, wd2) = _pre(
            h, ffn1_norm[l], ffn1_w_gate[l], ffn1_w_up[l], ffn1_w_down[l], mix_norm[l], w_in[l], gla_norm[l],
            conv_w[l], conv_b[l], post_weights)
        wgate_f, gbias_f = _pack_gate(gate_fwd_w[l], gate_fwd_b[l], 0)
        wgate_b, gbias_b = _pack_gate(gate_bwd_w[l], gate_bwd_b[l], GATE_RANK)
        o_f, o_b = _gla(q, k, v, z, wgate_f, wgate_b, gbias_f, gbias_b, batch, seq)
        h = _post(h, o_f, o_b, gr, c_part, edges, conv_w[l], w_o,
                  xattn_norm[l], w_q, mem, mem_norm[l], w_kv, w_xo, ffn2_norm[l], wg2, wu2, wd2, seq,
                  final_norm=final_norm if l == depth - 1 else None)
    return h.reshape(batch, seq, d)
```

```python
import functools

import jax
import jax.numpy as jnp
from jax import lax
from jax.experimental import pallas as pl
from jax.experimental.pallas import tpu as pltpu

F32 = jnp.float32
BF16 = jnp.bfloat16

XATTN_HEADS = 4
GLA_HEADS = 4
GLA_DK = 64
GLA_DV = 128
GLA_KEY_W = GLA_HEADS * GLA_DK
GLA_VAL_W = GLA_HEADS * GLA_DV
GATE_RANK = 16
GATE_TAU = 16.0
CHUNK = 64
CONV_W = 512
EPS = 1e-6
LOG2_E = 1.4426950408889634

LANES = 128
SUBLANES = 8
BF16_ROWS = 16
TOKEN_TILE = 512
TILE_SPLIT = 2
GLA_BLOCK = 1024
GLA_PIPELINE_DEPTH = 2
FFN_CHUNK = 256
STAGE_ROWS = 128
STAGE_SLOTS = 6
VMEM_LIMIT_BYTES = 58 * 1024 * 1024


def _cparams(n_axes):
    return pltpu.CompilerParams(
        dimension_semantics=("arbitrary",) * n_axes,
        vmem_limit_bytes=VMEM_LIMIT_BYTES,
    )


def _const_spec(shape):
    nd = len(shape)
    return pl.BlockSpec(shape, lambda *_: (0,) * nd, pipeline_mode=pl.Buffered(1))


_ANY_SPEC = pl.BlockSpec(memory_space=pl.ANY)


def _load_weights_bf16(jobs, stage_ref, sem_ref):
    n_slots, rows = stage_ref.shape[0], stage_ref.shape[1]
    ahead = n_slots - 1
    counts = [n_rows // rows for _, _, n_rows, _, _ in jobs]
    for (src, _, n_rows, dst, _), n in zip(jobs, counts):
        assert n_rows % rows == 0 and n >= ahead and src.shape[1] == dst.shape[1] <= stage_ref.shape[2]
    bases = [sum(counts[:j]) for j in range(len(jobs))]
    stream = [(j, c) for j, n in enumerate(counts) for c in range(n)]

    def copy(j, c, g):
        src, src_row0, _, _, _ = jobs[j]
        r0 = pl.multiple_of(src_row0 + c * rows, SUBLANES)
        return pltpu.make_async_copy(src.at[pl.ds(r0, rows), :],
                                     stage_ref.at[g % n_slots, :, pl.ds(0, src.shape[1])], sem_ref.at[g % n_slots])

    def finish(j, c, g):
        _, _, _, dst, dst_row0 = jobs[j]
        copy(j, c, g).wait()
        r0 = pl.multiple_of(dst_row0 + c * rows, BF16_ROWS)
        dst[pl.ds(r0, rows), :] = stage_ref[g % n_slots, :, 0:dst.shape[1]].astype(BF16)

    for g in range(ahead):
        copy(*stream[g], g).start()
    for j, n in enumerate(counts):
        def body(c, carry, j=j):
            copy(j, c + ahead, bases[j] + c + ahead).start()
            finish(j, c, bases[j] + c)
            return carry

        lax.fori_loop(0, n - ahead, body, 0)
        for c in range(n - ahead, n):
            g = bases[j] + c
            if g + ahead < len(stream):
                copy(*stream[g + ahead], g + ahead).start()
            finish(j, c, g)


def _convert_weights_across_steps(jobs, step, n_steps, in_sem, out_sem):
    def chunks(job):
        return job[0].shape[0] // job[2].shape[1]

    def copy_in(j, c, slot):
        src, _, stage, _ = jobs[j]
        r = stage.shape[1]
        return pltpu.make_async_copy(src.at[pl.ds(pl.multiple_of(c * r, r), r), :], stage.at[slot],
                                     in_sem.at[j, slot])

    def copy_out(j, c, slot):
        _, dst, stage, bstage = jobs[j]
        r = stage.shape[1]
        return pltpu.make_async_copy(bstage.at[slot], dst.at[pl.ds(pl.multiple_of(c * r, r), r), :],
                                     out_sem.at[j, slot])

    for job in jobs:
        assert job[0].shape[0] % job[2].shape[1] == 0 and 2 <= chunks(job) <= n_steps
    slot = step % 2
    for n in sorted({chunks(job) for job in jobs}):
        group = [j for j, job in enumerate(jobs) if chunks(job) == n]

        @pl.when(step == 0)
        def _():
            for j in group:
                copy_in(j, 0, 0).start()

        @pl.when(step + 1 < n)
        def _():
            for j in group:
                copy_in(j, step + 1, 1 - slot).start()

        @pl.when(jnp.logical_and(step >= 2, step - 2 < n))
        def _():
            for j in group:
                copy_out(j, step - 2, slot).wait()

        @pl.when(step < n)
        def _():
            for j in group:
                _, _, stage, bstage = jobs[j]
                copy_in(j, step, slot).wait()
                bstage[slot] = stage[slot].astype(BF16)
                copy_out(j, step, slot).start()

        late = [c for c in (n_steps - 2, n_steps - 1) if c < n]
        if late:
            @pl.when(step == n_steps - 1)
            def _():
                for j in group:
                    for c in late:
                        copy_out(j, c, c % 2).wait()


def _rms(x, g):
    ms = jnp.mean(x * x, axis=-1, keepdims=True)
    return x * lax.rsqrt(ms + EPS) * g


def _dot(a, b):
    return jnp.dot(a, b, preferred_element_type=F32)


def _dot_nt(a, b):
    return lax.dot_general(a, b, (((1,), (1,)), ((), ())), preferred_element_type=F32)


def _row_slices(tile_rows):
    assert tile_rows % (TILE_SPLIT * BF16_ROWS) == 0
    part = tile_rows // TILE_SPLIT
    return [slice(i * part, (i + 1) * part) for i in range(TILE_SPLIT)]


def _run_staggered(stage_gens):
    active = list(stage_gens)
    next(active[0])
    while active:
        for gen in list(active):
            try:
                next(gen)
            except StopIteration:
                active.remove(gen)


def _swiglu_stages(x, rows, n_ref, wg_ref, wu_ref, wd_ref, a_ref, out):
    h = _rms(x, n_ref[...]).astype(BF16)
    yield
    for c in range(wg_ref.shape[1] // FFN_CHUNK):
        sl = slice(c * FFN_CHUNK, (c + 1) * FFN_CHUNK)
        g = _dot(h, wg_ref[:, sl])
        u = _dot(h, wu_ref[:, sl])
        a_ref[rows, sl] = (g * jax.nn.sigmoid(g) * u).astype(BF16)
    yield
    out["y"] = x + 0.5 * _dot(a_ref[rows, :], wd_ref[...])


_P_Q, _P_K, _P_V, _P_R = 0, GLA_KEY_W, 2 * GLA_KEY_W, 2 * GLA_KEY_W + GLA_VAL_W
_P_BG = _P_R + GLA_VAL_W
_P_CG = _P_BG + CONV_W
_P_XV = _P_CG + CONV_W
_P_Z = _P_XV + CONV_W
_P_END = _P_Z + LANES


def _pre_body(*refs, n_later, n_steps):
    (x_ref, fn_ref, wg_hbm, wu_hbm, wd_hbm, mn_ref, wint_hbm, gn_ref), refs = refs[:8], refs[8:]
    later_f32, refs = refs[:n_later], refs[n_later:]
    (x1_ref, q_ref, k_ref, v_ref, gr_ref, bg_ref, u_ref, z_ref), refs = refs[:8], refs[8:]
    later_bf16, refs = refs[:n_later], refs[n_later:]
    (a_ref, wg_ref, wu_ref, wd_ref, wp_ref, stage_ref, sem_ref), refs = refs[:7], refs[7:]
    later_stage, later_bstage = refs[:n_later], refs[n_later:2 * n_later]
    later_in_sem, later_out_sem = refs[2 * n_later:]

    _convert_weights_across_steps(list(zip(later_f32, later_bf16, later_stage, later_bstage)),
                                  pl.program_id(0), n_steps, later_in_sem, later_out_sem)

    @pl.when(pl.program_id(0) == 0)
    def _():
        d, d_ff = wg_hbm.shape
        z_w = 2 * GATE_RANK
        main = _P_BG
        conv = _P_Z - _P_BG
        _load_weights_bf16(
            [(wg_hbm, 0, d, wg_ref, 0), (wu_hbm, 0, d, wu_ref, 0), (wd_hbm, 0, d_ff, wd_ref, 0),
             (wint_hbm, 0, main, wp_ref, 0), (wint_hbm, main + z_w, conv, wp_ref, _P_BG)],
            stage_ref, sem_ref)
        z_copy = pltpu.make_async_copy(wint_hbm.at[pl.ds(main, z_w), :],
                                       stage_ref.at[0, pl.ds(0, z_w), pl.ds(0, d)], sem_ref.at[0])
        z_copy.start()
        z_copy.wait()
        wp_ref[_P_Z:_P_Z + z_w, :] = stage_ref[0, 0:z_w, 0:d].astype(BF16)
        wp_ref[_P_Z + z_w:_P_END, :] = jnp.zeros((LANES - z_w, d), BF16)

    def stages(rows):
        ffn = {}
        yield from _swiglu_stages(x_ref[rows, :], rows, fn_ref, wg_ref, wu_ref, wd_ref, a_ref, ffn)
        x1 = ffn["y"]
        x1_ref[rows, :] = x1
        yield
        h = _rms(x1, mn_ref[...]).astype(BF16)
        yield
        p = _dot_nt(h, wp_ref[...])
        q_ref[rows, :] = p[:, _P_Q:_P_K] * (GLA_DK ** -0.5)
        k_ref[rows, :] = p[:, _P_K:_P_V]
        v_ref[rows, :] = p[:, _P_V:_P_R].astype(BF16)
        r = p[:, _P_R:_P_BG]
        gr_ref[rows, :] = (gn_ref[...] * (r * jax.nn.sigmoid(r))).astype(BF16)
        bg_ref[rows, :] = p[:, _P_BG:_P_CG].astype(BF16)
        u_ref[rows, :] = (p[:, _P_CG:_P_XV] * p[:, _P_XV:_P_Z]).astype(BF16)
        z_ref[rows, :] = p[:, _P_Z:_P_END].astype(BF16)

    _run_staggered(stages(rows) for rows in _row_slices(x_ref.shape[0]))


def _rows_per_step(n_rows, n_steps):
    r = -(-n_rows // n_steps)
    r += -r % BF16_ROWS
    while n_rows % r:
        r += BF16_ROWS
    return r


def _pre(x, ffn_norm, wg, wu, wd, mix_norm, w_in, gla_norm, later_weights):
    t, d = x.shape
    d_ff = wg.shape[1]
    assert t % TOKEN_TILE == 0 and d_ff % FFN_CHUNK == 0 and TOKEN_TILE % CHUNK == 0
    n_steps = t // TOKEN_TILE
    tile = lambda w: pl.BlockSpec((TOKEN_TILE, w), lambda i: (i, 0))
    out_w = [(d, F32), (GLA_KEY_W, F32), (GLA_KEY_W, F32), (GLA_VAL_W, BF16), (GLA_VAL_W, BF16),
             (CONV_W, BF16), (CONV_W, BF16), (LANES, BF16)]
    assert w_in.shape[1] == _P_Z + 2 * GATE_RANK
    n_later = len(later_weights)
    later_stage = [(2, _rows_per_step(w.shape[0], n_steps), w.shape[1]) for w in later_weights]
    outs = pl.pallas_call(
        functools.partial(_pre_body, n_later=n_later, n_steps=n_steps),
        grid=(n_steps,),
        in_specs=[tile(d), _const_spec((1, d)), _ANY_SPEC, _ANY_SPEC, _ANY_SPEC, _const_spec((1, d)), _ANY_SPEC,
                  _const_spec((1, GLA_VAL_W))] + [_ANY_SPEC] * n_later,
        out_specs=[tile(w) for w, _ in out_w] + [_ANY_SPEC] * n_later,
        out_shape=[jax.ShapeDtypeStruct((t, w), dt) for w, dt in out_w]
        + [jax.ShapeDtypeStruct(w.shape, BF16) for w in later_weights],
        scratch_shapes=[pltpu.VMEM((TOKEN_TILE, d_ff), BF16),
                        pltpu.VMEM((d, d_ff), BF16), pltpu.VMEM((d, d_ff), BF16), pltpu.VMEM((d_ff, d), BF16),
                        pltpu.VMEM((_P_END, d), BF16),
                        pltpu.VMEM((STAGE_SLOTS, STAGE_ROWS, max(d_ff, d)), F32),
                        pltpu.SemaphoreType.DMA((STAGE_SLOTS,))]
        + [pltpu.VMEM(s, F32) for s in later_stage] + [pltpu.VMEM(s, BF16) for s in later_stage]
        + [pltpu.SemaphoreType.DMA((n_later, 2)), pltpu.SemaphoreType.DMA((n_later, 2))],
        compiler_params=_cparams(1),
        name="pre",
    )(x, ffn_norm.reshape(1, d), wg, wu, wd, mix_norm.reshape(1, d), w_in.T, gla_norm.reshape(1, -1),
      *later_weights)
    return outs[:len(out_w)], outs[len(out_w):]


def _chunk_cumsum(x, reverse):
    n, w = x.shape
    groups = n // SUBLANES
    x3 = x.reshape(groups, SUBLANES, w)
    pos = lax.broadcasted_iota(jnp.int32, x3.shape, 1)
    step = 1
    while step < SUBLANES:
        if reverse:
            x3 = x3 + jnp.where(pos < SUBLANES - step, pltpu.roll(x3, SUBLANES - step, 1), 0.0)
        else:
            x3 = x3 + jnp.where(pos >= step, pltpu.roll(x3, step, 1), 0.0)
        step *= 2
    per = CHUNK // SUBLANES
    x4 = x3.reshape(n // CHUNK, per, SUBLANES, w)
    edge = x4[:, :, 0:1, :] if reverse else x4[:, :, SUBLANES - 1:SUBLANES, :]
    outs = [None] * per
    carry = None
    for j in (range(per - 1, -1, -1) if reverse else range(per)):
        outs[j] = x4[:, j] if carry is None else x4[:, j] + carry
        carry = edge[:, j] if carry is None else carry + edge[:, j]
    return jnp.stack(outs, axis=1).reshape(n, w)


def _gla_chunks(q_ref, k_ref, v_ref, z_ref, wgate_ref, gbias_ref, o_ref, s_ref, *, reverse):
    n_chunks = q_ref.shape[0] // CHUNK
    kw = GLA_KEY_W
    assert GLA_DK == CHUNK and 2 * CHUNK == LANES == GLA_DV and CHUNK & (CHUNK - 1) == 0
    shift = CHUNK.bit_length() - 1
    gz = _dot(z_ref[...], wgate_ref[...]) + gbias_ref[...]

    row_head = lax.broadcasted_iota(jnp.int32, (kw, kw), 0) >> shift
    lane_head = lax.broadcasted_iota(jnp.int32, (kw, kw), 1) >> shift
    same_head = row_head.astype(F32).astype(BF16) == lane_head.astype(F32).astype(BF16)
    pos_q = lax.broadcasted_iota(jnp.int32, (CHUNK, kw), 0)
    pos_k = lax.broadcasted_iota(jnp.int32, (CHUNK, kw), 1) & (CHUNK - 1)
    causal = (pos_k >= pos_q) if reverse else (pos_k <= pos_q)
    low_half = lax.broadcasted_iota(jnp.int32, (CHUNK, LANES), 1) < CHUNK
    low_feat = lax.broadcasted_iota(jnp.int32, (kw, LANES), 1) < CHUNK

    def chunk(c):
        rows = slice(c * CHUNK, (c + 1) * CHUNK)
        g = gz[rows, :]
        log2_gate = (jnp.minimum(g, 0.0) - jnp.log(1.0 + jnp.exp2(jnp.abs(g) * -LOG2_E))) * (LOG2_E / GATE_TAU)
        bc = _chunk_cumsum(log2_gate, reverse)
        q = q_ref[rows, :]
        k = k_ref[rows, :]
        v = v_ref[rows, :]
        b_last = bc[0:1, :] if reverse else bc[CHUNK - 1:CHUNK, :]
        q_in = q * jnp.exp2(bc)
        k_in = (k * jnp.exp2(-bc)).astype(BF16)
        k_st = k * jnp.exp2(b_last - bc)
        k_blk = jnp.where(same_head, jnp.concatenate([k_in] * GLA_HEADS, axis=0), jnp.zeros((), BF16))
        att = _dot_nt(q_in.astype(BF16), k_blk)
        att = jnp.where(causal, att, 0.0)
        decay_rows = jnp.broadcast_to(jnp.exp2(b_last), (SUBLANES, kw))
        pad = jnp.zeros((LANES - CHUNK - SUBLANES, kw), F32)
        xt = jnp.concatenate([k_st, decay_rows, pad], axis=0).T
        kst_t = jnp.where(low_feat, xt, 0.0).astype(BF16)
        yield
        outs = []
        for pair in range(GLA_HEADS // 2):
            lanes = slice(pair * LANES, (pair + 1) * LANES)
            a_col, q_col = att[:, lanes], q_in[:, lanes]
            a_swapped = pltpu.roll(a_col, CHUNK, 1)
            q_swapped = pltpu.roll(q_col, CHUNK, 1)
            top = jnp.concatenate([jnp.where(low_half, a_col, q_swapped),
                                   jnp.where(low_half, a_swapped, q_col)], axis=1).astype(BF16)
            heads = (2 * pair, 2 * pair + 1)
            bottom = jnp.concatenate([kst_t[h * GLA_DK:(h + 1) * GLA_DK, :] for h in heads], axis=1)
            s_in = [s_ref[h] for h in heads]
            zero = jnp.zeros((LANES, GLA_DV), BF16)
            vs = [jnp.concatenate([v[:, h * GLA_DV:(h + 1) * GLA_DV], s.astype(BF16)], axis=0)
                  for h, s in zip(heads, s_in)]
            rhs = jnp.concatenate([jnp.concatenate([vs[0], zero], axis=1),
                                   jnp.concatenate([zero, vs[1]], axis=1)], axis=0)
            res = _dot(jnp.concatenate([top, bottom], axis=0), rhs)
            outs.append(res[:CHUNK, :])
            for i, (h, s) in enumerate(zip(heads, s_in)):
                decay = xt[h * GLA_DK:(h + 1) * GLA_DK, CHUNK:CHUNK + 1]
                s_ref[h] = decay * s + res[CHUNK:, i * GLA_DV:(i + 1) * GLA_DV]
        o_ref[rows, :] = jnp.concatenate(outs, axis=1).astype(o_ref.dtype)

    return [chunk(c) for c in (range(n_chunks - 1, -1, -1) if reverse else range(n_chunks))]


def _run_pipelined(two_stage_gens, depth):
    started = []
    for gen in two_stage_gens:
        next(gen)
        started.append(gen)
        if len(started) > depth:
            for _ in started.pop(0):
                pass
    for gen in started:
        for _ in gen:
            pass


def _gla_body(qf_ref, kf_ref, vf_ref, zf_ref, qb_ref, kb_ref, vb_ref, zb_ref, wgf_ref, wgb_ref, gbf_ref, gbb_ref,
              of_ref, ob_ref, sf_ref, sb_ref):
    @pl.when(pl.program_id(1) == 0)
    def _():
        sf_ref[...] = jnp.zeros_like(sf_ref)
        sb_ref[...] = jnp.zeros_like(sb_ref)

    fwd = _gla_chunks(qf_ref, kf_ref, vf_ref, zf_ref, wgf_ref, gbf_ref, of_ref, sf_ref, reverse=False)
    bwd = _gla_chunks(qb_ref, kb_ref, vb_ref, zb_ref, wgb_ref, gbb_ref, ob_ref, sb_ref, reverse=True)
    _run_pipelined([gen for pair in zip(fwd, bwd) for gen in pair], GLA_PIPELINE_DEPTH)


def _gla(q, k, v, z, wgate_f, wgate_b, gbias_f, gbias_b, batch, seq):
    nb = seq // GLA_BLOCK
    fwd = lambda w: pl.BlockSpec((GLA_BLOCK, w), lambda b, j: (b * nb + j, 0))
    bwd = lambda w: pl.BlockSpec((GLA_BLOCK, w), lambda b, j: (b * nb + nb - 1 - j, 0))
    t = batch * seq
    return pl.pallas_call(
        _gla_body,
        grid=(batch, nb),
        in_specs=[fwd(GLA_KEY_W), fwd(GLA_KEY_W), fwd(GLA_VAL_W), fwd(LANES),
                  bwd(GLA_KEY_W), bwd(GLA_KEY_W), bwd(GLA_VAL_W), bwd(LANES),
                  _const_spec(wgate_f.shape), _const_spec(wgate_b.shape),
                  _const_spec(gbias_f.shape), _const_spec(gbias_b.shape)],
        out_specs=[fwd(GLA_VAL_W), bwd(GLA_VAL_W)],
        out_shape=[jax.ShapeDtypeStruct((t, GLA_VAL_W), BF16)] * 2,
        scratch_shapes=[pltpu.VMEM((GLA_HEADS, GLA_DK, GLA_DV), F32)] * 2,
        compiler_params=_cparams(2),
        name="gla",
    )(q, k, v, z, q, k, v, z, wgate_f, wgate_b, gbias_f, gbias_b)


def _memory_kv(m_ref, n_ref, w_ref, k_ref, v_ref):
    d = m_ref.shape[-1]
    h = _rms(m_ref[...], n_ref[...]).astype(BF16)
    kv = _dot(h, w_ref[...])
    k_ref[...] = kv[:, :d].astype(BF16)
    v_ref[...] = kv[:, d:].astype(BF16)


def _ordering_zero(value):
    bits = pltpu.bitcast(value[-SUBLANES:, -LANES:].astype(F32), jnp.uint32)
    for _ in range(2):
        bits = lax.shift_right_logical(bits, jnp.uint32(16))
    return pltpu.bitcast(bits, F32)


def _mixed_heads(rows, tile, of_ref, ob_ref, gr_ref, bg_ref, u_ref, edge_before, edge_after, cw_ref, cb_ref, seq,
                 behind=None):
    tm = u_ref.shape[0]
    n = rows.stop - rows.start
    eps, conv_w = EPS, cw_ref[...]
    if behind is not None:
        zero = _ordering_zero(behind)
        eps = EPS + zero[0:1, 0:1]
        conv_w = conv_w + jnp.concatenate([zero[0:conv_w.shape[0], :]] * (conv_w.shape[1] // LANES), axis=1)
    o = of_ref[rows, :].astype(F32) + ob_ref[rows, :].astype(F32)
    heads = []
    for h in range(GLA_HEADS):
        oh = o[:, h * GLA_DV:(h + 1) * GLA_DV]
        heads.append(oh * lax.rsqrt(jnp.mean(oh * oh, axis=-1, keepdims=True) + eps))
    a_out = jnp.concatenate(heads, axis=1) * gr_ref[rows, :].astype(F32)
    u = u_ref[rows, :].astype(F32)
    if rows.start == 0:
        at_seq_start = (tile * tm) % seq == 0
        before = jnp.where(at_seq_start, 0.0, edge_before.astype(F32))
    else:
        before = u_ref[rows.start - 1:rows.start, :].astype(F32)
    if rows.stop == tm:
        at_seq_end = ((tile + 1) * tm) % seq == 0
        after = jnp.where(at_seq_end, 0.0, edge_after.astype(F32))
    else:
        after = u_ref[rows.stop:rows.stop + 1, :].astype(F32)
    row = lax.broadcasted_iota(jnp.int32, (SUBLANES, u.shape[1]), 0)
    down, up = pltpu.roll(u, 1, 0), pltpu.roll(u, n - 1, 0)
    u_prev = jnp.concatenate([jnp.where(row == 0, before, down[:SUBLANES]), down[SUBLANES:]], axis=0)
    u_next = jnp.concatenate([up[:n - SUBLANES], jnp.where(row == SUBLANES - 1, after, up[n - SUBLANES:])], axis=0)
    conv = conv_w[0:1, :] * u_prev + conv_w[1:2, :] * u + conv_w[2:3, :] * u_next + cb_ref[...]
    c_out = bg_ref[rows, :].astype(F32) * conv
    return jnp.concatenate([a_out, c_out], axis=1).astype(BF16)


def _cross_attention_stages(x, n_ref, wq_ref, k_ref, v_ref, mem_rows, wo_ref, out):
    d = x.shape[-1]
    hd = d // XATTN_HEADS
    h = _rms(x, n_ref[...]).astype(BF16)
    yield
    q = _dot(h, wq_ref[...])
    yield
    heads = [slice(a * hd, (a + 1) * hd) for a in range(XATTN_HEADS)]
    scores = [_dot_nt(q[:, sl].astype(BF16), k_ref[mem_rows, sl]) * (hd ** -0.5) for sl in heads]
    probs = [jnp.exp(s - jnp.max(s, axis=-1, keepdims=True)) for s in scores]
    outs = [_dot(p.astype(BF16), v_ref[mem_rows, sl]) / jnp.sum(p, axis=-1, keepdims=True)
            for p, sl in zip(probs, heads)]
    o = jnp.concatenate(outs, axis=1).astype(BF16)
    yield
    out["y"] = x + _dot(o, wo_ref[...])


def _post_body(*refs, seq, n_batch, final):
    (x_ref, of_ref, ob_ref, gr_ref, bg_ref, u_ref, up_ref, un_ref, cw_ref, cb_ref, wo_ref,
     xn_ref, wq_ref, mem_ref, mn_ref, wkv_ref, wxo_ref, fn_ref, wg_ref, wu_ref, wd_ref) = refs[:21]
    refs = refs[21:]
    if final:
        last_ref, refs = refs[0], refs[1:]
    o_ref, a_ref, k_ref, v_ref = refs
    tm = x_ref.shape[0]
    i = pl.program_id(0)

    @pl.when(i == 0)
    def _():
        _memory_kv(mem_ref, mn_ref, wkv_ref, k_ref, v_ref)

    n_mem = mem_ref.shape[0] // n_batch
    mem_rows = pl.ds(pl.multiple_of((i * tm // seq) * n_mem, n_mem), n_mem)

    earlier_mixed = []

    def stages(rows):
        mixed = _mixed_heads(rows, i, of_ref, ob_ref, gr_ref, bg_ref, u_ref, up_ref[BF16_ROWS - 1:BF16_ROWS, :],
                             un_ref[0:1, :], cw_ref, cb_ref, seq,
                             behind=earlier_mixed[-1] if earlier_mixed else None)
        earlier_mixed.append(mixed)
        yield
        x2 = x_ref[rows, :] + _dot(mixed, wo_ref[...])
        yield
        box = {}
        yield from _cross_attention_stages(x2, xn_ref, wq_ref, k_ref, v_ref, mem_rows, wxo_ref, box)
        yield
        x3 = box["y"]
        yield from _swiglu_stages(x3, rows, fn_ref, wg_ref, wu_ref, wd_ref, a_ref, box)
        y = box["y"]
        if final:
            yield
            y = _rms(y, last_ref[...])
        o_ref[rows, :] = y

    _run_staggered(stages(rows) for rows in _row_slices(tm))


def _post(x, o_f, o_b, gr, bg, u, conv_w, conv_b, w_out, xattn_norm, w_q, mem, mem_norm, w_kv, w_xo,
          ffn_norm, wg, wu, wd, seq, final_norm=None):
    t, d = x.shape
    tm = TOKEN_TILE
    d_ff = wg.shape[1]
    n_batch, n_mem, _ = mem.shape
    assert seq % tm == 0 and tm % BF16_ROWS == 0 and d_ff % FFN_CHUNK == 0 and n_mem % BF16_ROWS == 0
    final = final_norm is not None
    tile = lambda w: pl.BlockSpec((tm, w), lambda i: (i, 0))
    hb = tm // BF16_ROWS
    n_halo = t // BF16_ROWS
    halo_prev = pl.BlockSpec((BF16_ROWS, CONV_W), lambda i: (jnp.maximum(i * hb - 1, 0), 0))
    halo_next = pl.BlockSpec((BF16_ROWS, CONV_W), lambda i: (jnp.minimum((i + 1) * hb, n_halo - 1), 0))
    for w in (w_out, w_q, w_kv, w_xo, wg, wu, wd):
        assert w.dtype == BF16
    mem_rows = n_batch * n_mem
    in_specs = [tile(d), tile(GLA_VAL_W), tile(GLA_VAL_W), tile(GLA_VAL_W), tile(CONV_W), tile(CONV_W),
                halo_prev, halo_next, _const_spec(conv_w.shape),
                _const_spec((1, CONV_W)), _const_spec(w_out.shape),
                _const_spec((1, d)), _const_spec(w_q.shape), _const_spec((mem_rows, d)), _const_spec((1, d)),
                _const_spec(w_kv.shape), _const_spec(w_xo.shape),
                _const_spec((1, d)), _const_spec(wg.shape), _const_spec(wu.shape), _const_spec(wd.shape)]
    args = [x, o_f, o_b, gr, bg, u, u, u, conv_w, conv_b.reshape(1, -1), w_out,
            xattn_norm.reshape(1, d), w_q, mem.reshape(mem_rows, d), mem_norm.reshape(1, d), w_kv, w_xo,
            ffn_norm.reshape(1, d), wg, wu, wd]
    if final:
        in_specs.append(_const_spec((1, d)))
        args.append(final_norm.reshape(1, d))
    return pl.pallas_call(
        functools.partial(_post_body, seq=seq, n_batch=n_batch, final=final),
        grid=(t // tm,),
        in_specs=in_specs,
        out_specs=tile(d),
        out_shape=jax.ShapeDtypeStruct((t, d), F32),
        scratch_shapes=[pltpu.VMEM((tm, d_ff), BF16), pltpu.VMEM((mem_rows, d), BF16),
                        pltpu.VMEM((mem_rows, d), BF16)],
        compiler_params=_cparams(1),
        name="post_final" if final else "post",
    )(*args)


def _pack_gate(gate_w, gate_b, z_row0):
    w = jnp.pad(gate_w, ((z_row0, LANES - z_row0 - GATE_RANK), (0, 0)))
    return w.astype(BF16), gate_b.reshape(1, -1).astype(F32)


def kernel(x, mem, ffn1_norm, ffn1_w_gate, ffn1_w_up, ffn1_w_down, mix_norm, w_in, gate_fwd_w, gate_fwd_b,
           gate_bwd_w, gate_bwd_b, gla_norm, conv_w, conv_b, w_out, xattn_norm, mem_norm, xattn_w_q,
           xattn_w_kv, xattn_w_o, ffn2_norm, ffn2_w_gate, ffn2_w_up, ffn2_w_down, final_norm):
    batch, seq, d = x.shape
    depth = w_in.shape[0]
    assert seq % GLA_BLOCK == 0 and GLA_BLOCK % CHUNK == 0 and seq % TOKEN_TILE == 0
    h = x.reshape(batch * seq, d)
    for l in range(depth):
        post_weights = [w_out[l], xattn_w_q[l], xattn_w_kv[l], xattn_w_o[l],
                        ffn2_w_gate[l], ffn2_w_up[l], ffn2_w_down[l]]
        (h, q, k, v, gr, bg, u, z), (w_o, w_q, w_kv, w_xo, wg2, wu2, wd2) = _pre(
            h, ffn1_norm[l], ffn1_w_gate[l], ffn1_w_up[l], ffn1_w_down[l], mix_norm[l], w_in[l], gla_norm[l],
            post_weights)
        wgate_f, gbias_f = _pack_gate(gate_fwd_w[l], gate_fwd_b[l], 0)
        wgate_b, gbias_b = _pack_gate(gate_bwd_w[l], gate_bwd_b[l], GATE_RANK)
        o_f, o_b = _gla(q, k, v, z, wgate_f, wgate_b, gbias_f, gbias_b, batch, seq)
        h = _post(h, o_f, o_b, gr, bg, u, conv_w[l], conv_b[l], w_o,
                  xattn_norm[l], w_q, mem, mem_norm[l], w_kv, w_xo, ffn2_norm[l], wg2, wu2, wd2, seq,
                  final_norm=final_norm if l == depth - 1 else None)
    return h.reshape(batch, seq, d)
```

```python
import functools

import jax
import jax.numpy as jnp
from jax import lax
from jax.experimental import pallas as pl
from jax.experimental.pallas import tpu as pltpu

F32 = jnp.float32
BF16 = jnp.bfloat16

XATTN_HEADS = 4
GLA_HEADS = 4
GLA_DK = 64
GLA_DV = 128
GLA_KEY_W = GLA_HEADS * GLA_DK
GLA_VAL_W = GLA_HEADS * GLA_DV
GATE_RANK = 16
GATE_TAU = 16.0
CHUNK = 64
CONV_W = 512
EPS = 1e-6
LOG2_E = 1.4426950408889634

LANES = 128
SUBLANES = 8
BF16_ROWS = 16
TOKEN_TILE = 512
TILE_SPLIT = 2
GLA_BLOCK = 1024
GLA_PIPELINE_DEPTH = 2
FFN_CHUNK = 256
STAGE_ROWS = 128
STAGE_SLOTS = 6
VMEM_LIMIT_BYTES = 58 * 1024 * 1024


def _cparams(n_axes):
    return pltpu.CompilerParams(
        dimension_semantics=("arbitrary",) * n_axes,
        vmem_limit_bytes=VMEM_LIMIT_BYTES,
    )


def _const_spec(shape):
    nd = len(shape)
    return pl.BlockSpec(shape, lambda *_: (0,) * nd, pipeline_mode=pl.Buffered(1))


_ANY_SPEC = pl.BlockSpec(memory_space=pl.ANY)


def _load_weights_bf16(jobs, stage_ref, sem_ref):
    n_slots, rows = stage_ref.shape[0], stage_ref.shape[1]
    ahead = n_slots - 1
    counts = [n_rows // rows for _, _, n_rows, _, _ in jobs]
    for (src, _, n_rows, dst, _), n in zip(jobs, counts):
        assert n_rows % rows == 0 and n >= ahead and src.shape[1] == dst.shape[1] <= stage_ref.shape[2]
    bases = [sum(counts[:j]) for j in range(len(jobs))]
    stream = [(j, c) for j, n in enumerate(counts) for c in range(n)]

    def copy(j, c, g):
        src, src_row0, _, _, _ = jobs[j]
        r0 = pl.multiple_of(src_row0 + c * rows, SUBLANES)
        return pltpu.make_async_copy(src.at[pl.ds(r0, rows), :],
                                     stage_ref.at[g % n_slots, :, pl.ds(0, src.shape[1])], sem_ref.at[g % n_slots])

    def finish(j, c, g):
        _, _, _, dst, dst_row0 = jobs[j]
        copy(j, c, g).wait()
        r0 = pl.multiple_of(dst_row0 + c * rows, BF16_ROWS)
        dst[pl.ds(r0, rows), :] = stage_ref[g % n_slots, :, 0:dst.shape[1]].astype(BF16)

    for g in range(ahead):
        copy(*stream[g], g).start(priority=g % 2)
    for j, n in enumerate(counts):
        def body(c, carry, j=j):
            copy(j, c + ahead, bases[j] + c + ahead).start()
            finish(j, c, bases[j] + c)
            return carry

        lax.fori_loop(0, n - ahead, body, 0)
        for c in range(n - ahead, n):
            g = bases[j] + c
            if g + ahead < len(stream):
                copy(*stream[g + ahead], g + ahead).start(priority=(g + ahead) % 2)
            finish(j, c, g)


def _convert_weights_across_steps(jobs, step, n_steps, in_sem, out_sem):
    def chunks(job):
        return job[0].shape[0] // job[2].shape[1]

    def copy_in(j, c, slot):
        src, _, stage, _ = jobs[j]
        r = stage.shape[1]
        return pltpu.make_async_copy(src.at[pl.ds(pl.multiple_of(c * r, r), r), :], stage.at[slot],
                                     in_sem.at[j, slot])

    def copy_out(j, c, slot):
        _, dst, stage, bstage = jobs[j]
        r = stage.shape[1]
        return pltpu.make_async_copy(bstage.at[slot], dst.at[pl.ds(pl.multiple_of(c * r, r), r), :],
                                     out_sem.at[j, slot])

    for job in jobs:
        assert job[0].shape[0] % job[2].shape[1] == 0 and 2 <= chunks(job) <= n_steps
    slot = step % 2
    for n in sorted({chunks(job) for job in jobs}):
        group = [j for j, job in enumerate(jobs) if chunks(job) == n]

        @pl.when(step == 0)
        def _():
            for j in group:
                copy_in(j, 0, 0).start(priority=j % 2)

        @pl.when(step + 1 < n)
        def _():
            for j in group:
                copy_in(j, step + 1, 1 - slot).start(priority=j % 2)

        @pl.when(jnp.logical_and(step >= 2, step - 2 < n))
        def _():
            for j in group:
                copy_out(j, step - 2, slot).wait()

        @pl.when(step < n)
        def _():
            for j in group:
                _, _, stage, bstage = jobs[j]
                copy_in(j, step, slot).wait()
                bstage[slot] = stage[slot].astype(BF16)
                copy_out(j, step, slot).start(priority=j % 2)

        late = [c for c in (n_steps - 2, n_steps - 1) if c < n]
        if late:
            @pl.when(step == n_steps - 1)
            def _():
                for j in group:
                    for c in late:
                        copy_out(j, c, c % 2).wait()


def _rms(x, g):
    ms = jnp.mean(x * x, axis=-1, keepdims=True)
    return x * lax.rsqrt(ms + EPS) * g


def _dot(a, b):
    return jnp.dot(a, b, preferred_element_type=F32)


def _dot_nt(a, b):
    return lax.dot_general(a, b, (((1,), (1,)), ((), ())), preferred_element_type=F32)


def _row_slices(tile_rows):
    assert tile_rows % (TILE_SPLIT * BF16_ROWS) == 0
    part = tile_rows // TILE_SPLIT
    return [slice(i * part, (i + 1) * part) for i in range(TILE_SPLIT)]


def _run_staggered(stage_gens):
    active = list(stage_gens)
    next(active[0])
    while active:
        for gen in list(active):
            try:
                next(gen)
            except StopIteration:
                active.remove(gen)


def _swiglu_stages(x, rows, n_ref, wg_ref, wu_ref, wd_ref, a_ref, out):
    h = _rms(x, n_ref[...]).astype(BF16)
    yield
    for c in range(wg_ref.shape[1] // FFN_CHUNK):
        sl = slice(c * FFN_CHUNK, (c + 1) * FFN_CHUNK)
        g = _dot(h, wg_ref[:, sl])
        u = _dot(h, wu_ref[:, sl])
        a_ref[rows, sl] = (g * jax.nn.sigmoid(g) * u).astype(BF16)
    yield
    out["y"] = x + 0.5 * _dot(a_ref[rows, :], wd_ref[...])


_P_Q, _P_K, _P_V, _P_R = 0, GLA_KEY_W, 2 * GLA_KEY_W, 2 * GLA_KEY_W + GLA_VAL_W
_P_BG = _P_R + GLA_VAL_W
_P_CG = _P_BG + CONV_W
_P_XV = _P_CG + CONV_W
_P_Z = _P_XV + CONV_W
_P_END = _P_Z + LANES


def _pre_body(*refs, n_later, n_steps):
    (x_ref, fn_ref, wg_hbm, wu_hbm, wd_hbm, mn_ref, wint_hbm, gn_ref), refs = refs[:8], refs[8:]
    later_f32, refs = refs[:n_later], refs[n_later:]
    (x1_ref, q_ref, k_ref, v_ref, gr_ref, bg_ref, u_ref, z_ref), refs = refs[:8], refs[8:]
    later_bf16, refs = refs[:n_later], refs[n_later:]
    (a_ref, wg_ref, wu_ref, wd_ref, wp_ref, stage_ref, sem_ref), refs = refs[:7], refs[7:]
    later_stage, later_bstage = refs[:n_later], refs[n_later:2 * n_later]
    later_in_sem, later_out_sem = refs[2 * n_later:]

    _convert_weights_across_steps(list(zip(later_f32, later_bf16, later_stage, later_bstage)),
                                  pl.program_id(0), n_steps, later_in_sem, later_out_sem)

    @pl.when(pl.program_id(0) == 0)
    def _():
        d, d_ff = wg_hbm.shape
        z_w = 2 * GATE_RANK
        main = _P_BG
        conv = _P_Z - _P_BG
        _load_weights_bf16(
            [(wg_hbm, 0, d, wg_ref, 0), (wu_hbm, 0, d, wu_ref, 0), (wd_hbm, 0, d_ff, wd_ref, 0),
             (wint_hbm, 0, main, wp_ref, 0), (wint_hbm, main + z_w, conv, wp_ref, _P_BG)],
            stage_ref, sem_ref)
        z_copy = pltpu.make_async_copy(wint_hbm.at[pl.ds(main, z_w), :],
                                       stage_ref.at[0, pl.ds(0, z_w), pl.ds(0, d)], sem_ref.at[0])
        z_copy.start()
        z_copy.wait()
        wp_ref[_P_Z:_P_Z + z_w, :] = stage_ref[0, 0:z_w, 0:d].astype(BF16)
        wp_ref[_P_Z + z_w:_P_END, :] = jnp.zeros((LANES - z_w, d), BF16)

    def stages(rows):
        ffn = {}
        yield from _swiglu_stages(x_ref[rows, :], rows, fn_ref, wg_ref, wu_ref, wd_ref, a_ref, ffn)
        x1 = ffn["y"]
        x1_ref[rows, :] = x1
        yield
        h = _rms(x1, mn_ref[...]).astype(BF16)
        yield
        p = _dot_nt(h, wp_ref[...])
        q_ref[rows, :] = p[:, _P_Q:_P_K] * (GLA_DK ** -0.5)
        k_ref[rows, :] = p[:, _P_K:_P_V]
        v_ref[rows, :] = p[:, _P_V:_P_R].astype(BF16)
        r = p[:, _P_R:_P_BG]
        gr_ref[rows, :] = (gn_ref[...] * (r * jax.nn.sigmoid(r))).astype(BF16)
        bg_ref[rows, :] = p[:, _P_BG:_P_CG].astype(BF16)
        u_ref[rows, :] = (p[:, _P_CG:_P_XV] * p[:, _P_XV:_P_Z]).astype(BF16)
        z_ref[rows, :] = p[:, _P_Z:_P_END].astype(BF16)

    _run_staggered(stages(rows) for rows in _row_slices(x_ref.shape[0]))


def _rows_per_step(n_rows, n_steps):
    r = -(-n_rows // n_steps)
    r += -r % BF16_ROWS
    while n_rows % r:
        r += BF16_ROWS
    return r


def _pre(x, ffn_norm, wg, wu, wd, mix_norm, w_in, gla_norm, later_weights):
    t, d = x.shape
    d_ff = wg.shape[1]
    assert t % TOKEN_TILE == 0 and d_ff % FFN_CHUNK == 0 and TOKEN_TILE % CHUNK == 0
    n_steps = t // TOKEN_TILE
    tile = lambda w: pl.BlockSpec((TOKEN_TILE, w), lambda i: (i, 0))
    out_w = [(d, F32), (GLA_KEY_W, F32), (GLA_KEY_W, F32), (GLA_VAL_W, BF16), (GLA_VAL_W, BF16),
             (CONV_W, BF16), (CONV_W, BF16), (LANES, BF16)]
    assert w_in.shape[1] == _P_Z + 2 * GATE_RANK
    n_later = len(later_weights)
    later_stage = [(2, _rows_per_step(w.shape[0], n_steps), w.shape[1]) for w in later_weights]
    outs = pl.pallas_call(
        functools.partial(_pre_body, n_later=n_later, n_steps=n_steps),
        grid=(n_steps,),
        in_specs=[tile(d), _const_spec((1, d)), _ANY_SPEC, _ANY_SPEC, _ANY_SPEC, _const_spec((1, d)), _ANY_SPEC,
                  _const_spec((1, GLA_VAL_W))] + [_ANY_SPEC] * n_later,
        out_specs=[tile(w) for w, _ in out_w] + [_ANY_SPEC] * n_later,
        out_shape=[jax.ShapeDtypeStruct((t, w), dt) for w, dt in out_w]
        + [jax.ShapeDtypeStruct(w.shape, BF16) for w in later_weights],
        scratch_shapes=[pltpu.VMEM((TOKEN_TILE, d_ff), BF16),
                        pltpu.VMEM((d, d_ff), BF16), pltpu.VMEM((d, d_ff), BF16), pltpu.VMEM((d_ff, d), BF16),
                        pltpu.VMEM((_P_END, d), BF16),
                        pltpu.VMEM((STAGE_SLOTS, STAGE_ROWS, max(d_ff, d)), F32),
                        pltpu.SemaphoreType.DMA((STAGE_SLOTS,))]
        + [pltpu.VMEM(s, F32) for s in later_stage] + [pltpu.VMEM(s, BF16) for s in later_stage]
        + [pltpu.SemaphoreType.DMA((n_later, 2)), pltpu.SemaphoreType.DMA((n_later, 2))],
        compiler_params=_cparams(1),
        name="pre",
    )(x, ffn_norm.reshape(1, d), wg, wu, wd, mix_norm.reshape(1, d), w_in.T, gla_norm.reshape(1, -1),
      *later_weights)
    return outs[:len(out_w)], outs[len(out_w):]


def _chunk_cumsum(x, reverse):
    n, w = x.shape
    groups = n // SUBLANES
    x3 = x.reshape(groups, SUBLANES, w)
    pos = lax.broadcasted_iota(jnp.int32, x3.shape, 1)
    step = 1
    while step < SUBLANES:
        if reverse:
            x3 = x3 + jnp.where(pos < SUBLANES - step, pltpu.roll(x3, SUBLANES - step, 1), 0.0)
        else:
            x3 = x3 + jnp.where(pos >= step, pltpu.roll(x3, step, 1), 0.0)
        step *= 2
    per = CHUNK // SUBLANES
    x4 = x3.reshape(n // CHUNK, per, SUBLANES, w)
    edge = x4[:, :, 0:1, :] if reverse else x4[:, :, SUBLANES - 1:SUBLANES, :]
    outs = [None] * per
    carry = None
    for j in (range(per - 1, -1, -1) if reverse else range(per)):
        outs[j] = x4[:, j] if carry is None else x4[:, j] + carry
        carry = edge[:, j] if carry is None else carry + edge[:, j]
    return jnp.stack(outs, axis=1).reshape(n, w)


def _gla_chunks(q_ref, k_ref, v_ref, z_ref, wgate_ref, gbias_ref, o_ref, s_ref, *, reverse):
    n_chunks = q_ref.shape[0] // CHUNK
    kw = GLA_KEY_W
    assert GLA_DK == CHUNK and 2 * CHUNK == LANES == GLA_DV and CHUNK & (CHUNK - 1) == 0
    shift = CHUNK.bit_length() - 1
    gz = _dot(z_ref[...], wgate_ref[...]) + gbias_ref[...]

    row_head = lax.broadcasted_iota(jnp.int32, (kw, kw), 0) >> shift
    lane_head = lax.broadcasted_iota(jnp.int32, (kw, kw), 1) >> shift
    same_head = row_head.astype(F32).astype(BF16) == lane_head.astype(F32).astype(BF16)
    pos_q = lax.broadcasted_iota(jnp.int32, (CHUNK, kw), 0)
    pos_k = lax.broadcasted_iota(jnp.int32, (CHUNK, kw), 1) & (CHUNK - 1)
    causal = (pos_k >= pos_q) if reverse else (pos_k <= pos_q)
    low_half = lax.broadcasted_iota(jnp.int32, (CHUNK, LANES), 1) < CHUNK
    low_feat = lax.broadcasted_iota(jnp.int32, (kw, LANES), 1) < CHUNK

    def chunk(c):
        rows = slice(c * CHUNK, (c + 1) * CHUNK)
        g = gz[rows, :]
        log2_gate = (jnp.minimum(g, 0.0) - jnp.log(1.0 + jnp.exp2(jnp.abs(g) * -LOG2_E))) * (LOG2_E / GATE_TAU)
        bc = _chunk_cumsum(log2_gate, reverse)
        q = q_ref[rows, :]
        k = k_ref[rows, :]
        v = v_ref[rows, :]
        b_last = bc[0:1, :] if reverse else bc[CHUNK - 1:CHUNK, :]
        q_in = q * jnp.exp2(bc)
        k_in = (k * jnp.exp2(-bc)).astype(BF16)
        k_st = k * jnp.exp2(b_last - bc)
        k_blk = jnp.where(same_head, jnp.concatenate([k_in] * GLA_HEADS, axis=0), jnp.zeros((), BF16))
        att = _dot_nt(q_in.astype(BF16), k_blk)
        att = jnp.where(causal, att, 0.0)
        decay_rows = jnp.broadcast_to(jnp.exp2(b_last), (SUBLANES, kw))
        pad = jnp.zeros((LANES - CHUNK - SUBLANES, kw), F32)
        xt = jnp.concatenate([k_st, decay_rows, pad], axis=0).T
        kst_t = jnp.where(low_feat, xt, 0.0).astype(BF16)
        yield
        outs = []
        for pair in range(GLA_HEADS // 2):
            lanes = slice(pair * LANES, (pair + 1) * LANES)
            a_col, q_col = att[:, lanes], q_in[:, lanes]
            a_swapped = pltpu.roll(a_col, CHUNK, 1)
            q_swapped = pltpu.roll(q_col, CHUNK, 1)
            top = jnp.concatenate([jnp.where(low_half, a_col, q_swapped),
                                   jnp.where(low_half, a_swapped, q_col)], axis=1).astype(BF16)
            heads = (2 * pair, 2 * pair + 1)
            bottom = jnp.concatenate([kst_t[h * GLA_DK:(h + 1) * GLA_DK, :] for h in heads], axis=1)
            s_in = [s_ref[h] for h in heads]
            zero = jnp.zeros((LANES, GLA_DV), BF16)
            vs = [jnp.concatenate([v[:, h * GLA_DV:(h + 1) * GLA_DV], s.astype(BF16)], axis=0)
                  for h, s in zip(heads, s_in)]
            rhs = jnp.concatenate([jnp.concatenate([vs[0], zero], axis=1),
                                   jnp.concatenate([zero, vs[1]], axis=1)], axis=0)
            res = _dot(jnp.concatenate([top, bottom], axis=0), rhs)
            outs.append(res[:CHUNK, :])
            for i, (h, s) in enumerate(zip(heads, s_in)):
                decay = xt[h * GLA_DK:(h + 1) * GLA_DK, CHUNK:CHUNK + 1]
                s_ref[h] = decay * s + res[CHUNK:, i * GLA_DV:(i + 1) * GLA_DV]
        o_ref[rows, :] = jnp.concatenate(outs, axis=1).astype(o_ref.dtype)

    return [chunk(c) for c in (range(n_chunks - 1, -1, -1) if reverse else range(n_chunks))]


def _run_pipelined(two_stage_gens, depth):
    started = []
    for gen in two_stage_gens:
        next(gen)
        started.append(gen)
        if len(started) > depth:
            for _ in started.pop(0):
                pass
    for gen in started:
        for _ in gen:
            pass


def _gla_body(qf_ref, kf_ref, vf_ref, zf_ref, qb_ref, kb_ref, vb_ref, zb_ref, wgf_ref, wgb_ref, gbf_ref, gbb_ref,
              of_ref, ob_ref, sf_ref, sb_ref):
    @pl.when(pl.program_id(1) == 0)
    def _():
        sf_ref[...] = jnp.zeros_like(sf_ref)
        sb_ref[...] = jnp.zeros_like(sb_ref)

    fwd = _gla_chunks(qf_ref, kf_ref, vf_ref, zf_ref, wgf_ref, gbf_ref, of_ref, sf_ref, reverse=False)
    bwd = _gla_chunks(qb_ref, kb_ref, vb_ref, zb_ref, wgb_ref, gbb_ref, ob_ref, sb_ref, reverse=True)
    _run_pipelined([gen for pair in zip(fwd, bwd) for gen in pair], GLA_PIPELINE_DEPTH)


def _gla(q, k, v, z, wgate_f, wgate_b, gbias_f, gbias_b, batch, seq):
    nb = seq // GLA_BLOCK
    fwd = lambda w: pl.BlockSpec((GLA_BLOCK, w), lambda b, j: (b * nb + j, 0))
    bwd = lambda w: pl.BlockSpec((GLA_BLOCK, w), lambda b, j: (b * nb + nb - 1 - j, 0))
    t = batch * seq
    return pl.pallas_call(
        _gla_body,
        grid=(batch, nb),
        in_specs=[fwd(GLA_KEY_W), fwd(GLA_KEY_W), fwd(GLA_VAL_W), fwd(LANES),
                  bwd(GLA_KEY_W), bwd(GLA_KEY_W), bwd(GLA_VAL_W), bwd(LANES),
                  _const_spec(wgate_f.shape), _const_spec(wgate_b.shape),
                  _const_spec(gbias_f.shape), _const_spec(gbias_b.shape)],
        out_specs=[fwd(GLA_VAL_W), bwd(GLA_VAL_W)],
        out_shape=[jax.ShapeDtypeStruct((t, GLA_VAL_W), BF16)] * 2,
        scratch_shapes=[pltpu.VMEM((GLA_HEADS, GLA_DK, GLA_DV), F32)] * 2,
        compiler_params=_cparams(2),
        name="gla",
    )(q, k, v, z, q, k, v, z, wgate_f, wgate_b, gbias_f, gbias_b)


def _memory_kv(m_ref, n_ref, w_ref, k_ref, v_ref):
    d = m_ref.shape[-1]
    h = _rms(m_ref[...], n_ref[...]).astype(BF16)
    kv = _dot(h, w_ref[...])
    k_ref[...] = kv[:, :d].astype(BF16)
    v_ref[...] = kv[:, d:].astype(BF16)


def _mixed_heads(rows, tile, of_ref, ob_ref, gr_ref, bg_ref, u_ref, edge_before, edge_after, cw_ref, cb_ref, seq):
    tm = u_ref.shape[0]
    n = rows.stop - rows.start
    o = of_ref[rows, :].astype(F32) + ob_ref[rows, :].astype(F32)
    heads = []
    for h in range(GLA_HEADS):
        oh = o[:, h * GLA_DV:(h + 1) * GLA_DV]
        heads.append(oh * lax.rsqrt(jnp.mean(oh * oh, axis=-1, keepdims=True) + EPS))
    a_out = jnp.concatenate(heads, axis=1) * gr_ref[rows, :].astype(F32)
    u = u_ref[rows, :].astype(F32)
    if rows.start == 0:
        at_seq_start = (tile * tm) % seq == 0
        before = jnp.where(at_seq_start, 0.0, edge_before.astype(F32))
    else:
        before = u_ref[rows.start - 1:rows.start, :].astype(F32)
    if rows.stop == tm:
        at_seq_end = ((tile + 1) * tm) % seq == 0
        after = jnp.where(at_seq_end, 0.0, edge_after.astype(F32))
    else:
        after = u_ref[rows.stop:rows.stop + 1, :].astype(F32)
    row = lax.broadcasted_iota(jnp.int32, (SUBLANES, u.shape[1]), 0)
    down, up = pltpu.roll(u, 1, 0), pltpu.roll(u, n - 1, 0)
    u_prev = jnp.concatenate([jnp.where(row == 0, before, down[:SUBLANES]), down[SUBLANES:]], axis=0)
    u_next = jnp.concatenate([up[:n - SUBLANES], jnp.where(row == SUBLANES - 1, after, up[n - SUBLANES:])], axis=0)
    conv = cw_ref[0:1, :] * u_prev + cw_ref[1:2, :] * u + cw_ref[2:3, :] * u_next + cb_ref[...]
    c_out = bg_ref[rows, :].astype(F32) * conv
    return jnp.concatenate([a_out, c_out], axis=1).astype(BF16)


def _cross_attention_stages(x, n_ref, wq_ref, k_ref, v_ref, mem_rows, wo_ref, out):
    d = x.shape[-1]
    hd = d // XATTN_HEADS
    h = _rms(x, n_ref[...]).astype(BF16)
    yield
    q = _dot(h, wq_ref[...])
    yield
    heads = [slice(a * hd, (a + 1) * hd) for a in range(XATTN_HEADS)]
    scores = [_dot_nt(q[:, sl].astype(BF16), k_ref[mem_rows, sl]) * (hd ** -0.5) for sl in heads]
    probs = [jnp.exp(s - jnp.max(s, axis=-1, keepdims=True)) for s in scores]
    outs = [_dot(p.astype(BF16), v_ref[mem_rows, sl]) / jnp.sum(p, axis=-1, keepdims=True)
            for p, sl in zip(probs, heads)]
    o = jnp.concatenate(outs, axis=1).astype(BF16)
    yield
    out["y"] = x + _dot(o, wo_ref[...])


def _post_body(*refs, seq, n_batch, final):
    (x_ref, of_ref, ob_ref, gr_ref, bg_ref, u_ref, up_ref, un_ref, cw_ref, cb_ref, wo_ref,
     xn_ref, wq_ref, mem_ref, mn_ref, wkv_ref, wxo_ref, fn_ref, wg_ref, wu_ref, wd_ref) = refs[:21]
    refs = refs[21:]
    if final:
        last_ref, refs = refs[0], refs[1:]
    o_ref, a_ref, k_ref, v_ref = refs
    tm = x_ref.shape[0]
    i = pl.program_id(0)

    @pl.when(i == 0)
    def _():
        _memory_kv(mem_ref, mn_ref, wkv_ref, k_ref, v_ref)

    n_mem = mem_ref.shape[0] // n_batch
    mem_rows = pl.ds(pl.multiple_of((i * tm // seq) * n_mem, n_mem), n_mem)

    def stages(rows):
        mixed = _mixed_heads(rows, i, of_ref, ob_ref, gr_ref, bg_ref, u_ref, up_ref[BF16_ROWS - 1:BF16_ROWS, :],
                             un_ref[0:1, :], cw_ref, cb_ref, seq)
        yield
        x2 = x_ref[rows, :] + _dot(mixed, wo_ref[...])
        yield
        box = {}
        yield from _cross_attention_stages(x2, xn_ref, wq_ref, k_ref, v_ref, mem_rows, wxo_ref, box)
        yield
        x3 = box["y"]
        yield from _swiglu_stages(x3, rows, fn_ref, wg_ref, wu_ref, wd_ref, a_ref, box)
        y = box["y"]
        if final:
            yield
            y = _rms(y, last_ref[...])
        o_ref[rows, :] = y

    _run_staggered(stages(rows) for rows in _row_slices(tm))


def _post(x, o_f, o_b, gr, bg, u, conv_w, conv_b, w_out, xattn_norm, w_q, mem, mem_norm, w_kv, w_xo,
          ffn_norm, wg, wu, wd, seq, final_norm=None):
    t, d = x.shape
    tm = TOKEN_TILE
    d_ff = wg.shape[1]
    n_batch, n_mem, _ = mem.shape
    assert seq % tm == 0 and tm % BF16_ROWS == 0 and d_ff % FFN_CHUNK == 0 and n_mem % BF16_ROWS == 0
    final = final_norm is not None
    tile = lambda w: pl.BlockSpec((tm, w), lambda i: (i, 0))
    hb = tm // BF16_ROWS
    n_halo = t // BF16_ROWS
    halo_prev = pl.BlockSpec((BF16_ROWS, CONV_W), lambda i: (jnp.maximum(i * hb - 1, 0), 0))
    halo_next = pl.BlockSpec((BF16_ROWS, CONV_W), lambda i: (jnp.minimum((i + 1) * hb, n_halo - 1), 0))
    for w in (w_out, w_q, w_kv, w_xo, wg, wu, wd):
        assert w.dtype == BF16
    mem_rows = n_batch * n_mem
    in_specs = [tile(d), tile(GLA_VAL_W), tile(GLA_VAL_W), tile(GLA_VAL_W), tile(CONV_W), tile(CONV_W),
                halo_prev, halo_next, _const_spec(conv_w.shape),
                _const_spec((1, CONV_W)), _const_spec(w_out.shape),
                _const_spec((1, d)), _const_spec(w_q.shape), _const_spec((mem_rows, d)), _const_spec((1, d)),
                _const_spec(w_kv.shape), _const_spec(w_xo.shape),
                _const_spec((1, d)), _const_spec(wg.shape), _const_spec(wu.shape), _const_spec(wd.shape)]
    args = [x, o_f, o_b, gr, bg, u, u, u, conv_w, conv_b.reshape(1, -1), w_out,
            xattn_norm.reshape(1, d), w_q, mem.reshape(mem_rows, d), mem_norm.reshape(1, d), w_kv, w_xo,
            ffn_norm.reshape(1, d), wg, wu, wd]
    if final:
        in_specs.append(_const_spec((1, d)))
        args.append(final_norm.reshape(1, d))
    return pl.pallas_call(
        functools.partial(_post_body, seq=seq, n_batch=n_batch, final=final),
        grid=(t // tm,),
        in_specs=in_specs,
        out_specs=tile(d),
        out_shape=jax.ShapeDtypeStruct((t, d), F32),
        scratch_shapes=[pltpu.VMEM((tm, d_ff), BF16), pltpu.VMEM((mem_rows, d), BF16),
                        pltpu.VMEM((mem_rows, d), BF16)],
        compiler_params=_cparams(1),
        name="post_final" if final else "post",
    )(*args)


def _pack_gate(gate_w, gate_b, z_row0):
    w = jnp.pad(gate_w, ((z_row0, LANES - z_row0 - GATE_RANK), (0, 0)))
    return w.astype(BF16), gate_b.reshape(1, -1).astype(F32)


def kernel(x, mem, ffn1_norm, ffn1_w_gate, ffn1_w_up, ffn1_w_down, mix_norm, w_in, gate_fwd_w, gate_fwd_b,
           gate_bwd_w, gate_bwd_b, gla_norm, conv_w, conv_b, w_out, xattn_norm, mem_norm, xattn_w_q,
           xattn_w_kv, xattn_w_o, ffn2_norm, ffn2_w_gate, ffn2_w_up, ffn2_w_down, final_norm):
    batch, seq, d = x.shape
    depth = w_in.shape[0]
    assert seq % GLA_BLOCK == 0 and GLA_BLOCK % CHUNK == 0 and seq % TOKEN_TILE == 0
    h = x.reshape(batch * seq, d)
    for l in range(depth):
        post_weights = [w_out[l], xattn_w_q[l], xattn_w_kv[l], xattn_w_o[l],
                        ffn2_w_gate[l], ffn2_w_up[l], ffn2_w_down[l]]
        (h, q, k, v, gr, bg, u, z), (w_o, w_q, w_kv, w_xo, wg2, wu2, wd2) = _pre(
            h, ffn1_norm[l], ffn1_w_gate[l], ffn1_w_up[l], ffn1_w_down[l], mix_norm[l], w_in[l], gla_norm[l],
            post_weights)
        wgate_f, gbias_f = _pack_gate(gate_fwd_w[l], gate_fwd_b[l], 0)
        wgate_b, gbias_b = _pack_gate(gate_bwd_w[l], gate_bwd_b[l], GATE_RANK)
        o_f, o_b = _gla(q, k, v, z, wgate_f, wgate_b, gbias_f, gbias_b, batch, seq)
        h = _post(h, o_f, o_b, gr, bg, u, conv_w[l], conv_b[l], w_o,
                  xattn_norm[l], w_q, mem, mem_norm[l], w_kv, w_xo, ffn2_norm[l], wg2, wu2, wd2, seq,
                  final_norm=final_norm if l == depth - 1 else None)
    return h.reshape(batch, seq, d)
```
